```python
import jax, jax.numpy as jnp
from jax import lax
import numpy as np

D_MODEL = 1024
BATCH = 4
SEQ = 8192
DEPTH = 4
DEC_BATCH = 8
DEC_SEQ = 16
PAST_LEN = 1024

CHUNK = 64
EPS = 1e-6
FOX_HEADS = 8
HEAD_DIM = 64
FOX_WIDTH = FOX_HEADS * HEAD_DIM
Q_BLOCK = 128
GMLP_GROUPS = 4
GMLP_WIDTH = 256
GMLP_GROUP_DIM = GMLP_WIDTH // GMLP_GROUPS
GMLP_CHUNK = 128
POOL_GROUPS = 4
POOL_WIDTH = 256
POOL_GROUP_DIM = POOL_WIDTH // POOL_GROUPS
POOL_WINDOWS = (2, 4, 8, 16)
POOL_HIST = 15
MIX_WIDTH = FOX_WIDTH + GMLP_WIDTH + POOL_WIDTH
N_BRANCH = 3
IN_WIDTH = 3 * FOX_WIDTH + FOX_HEADS + 2 * GMLP_WIDTH + POOL_WIDTH + N_BRANCH * D_MODEL
PEER_HEADS = 8
N_KEYS = 128
N_EXPERTS = N_KEYS * N_KEYS
PEER_TOPK = 16
PEER_QDIM = 256
PEER_HALF = PEER_QDIM // 2
PEER_BLOCK = 256

kernel_name = "fox_gmlp_pool_peer_stream_step"


def rms_norm(x, g):
    xf = x.astype(jnp.float32)
    y = xf * lax.rsqrt(jnp.mean(xf * xf, axis=-1, keepdims=True) + EPS) * g.astype(jnp.float32)
    return y.astype(x.dtype)


def layer_norm(x, g, b):
    xf = x.astype(jnp.float32)
    mu = jnp.mean(xf, axis=-1, keepdims=True)
    xc = xf - mu
    y = xc * lax.rsqrt(jnp.mean(xc * xc, axis=-1, keepdims=True) + EPS) * g.astype(jnp.float32) + b.astype(jnp.float32)
    return y.astype(x.dtype)


def split_in(z):
    sizes = (FOX_WIDTH, FOX_WIDTH, FOX_WIDTH, FOX_HEADS, GMLP_WIDTH, GMLP_WIDTH, POOL_WIDTH, N_BRANCH * D_MODEL)
    out, start = [], 0
    for n in sizes:
        out.append(z[..., start:start + n])
        start += n
    return out


def mixer_inputs(h, w_in, b_f, ln_g, ln_b):
    b, L, _ = h.shape
    zq, zk, zv, zf, zu, zgv, zp, zg = split_in(h @ w_in)
    q = zq.reshape(b, L, FOX_HEADS, HEAD_DIM)
    k = zk.reshape(b, L, FOX_HEADS, HEAD_DIM)
    v = zv.reshape(b, L, FOX_HEADS, HEAD_DIM)
    logf = jax.nn.log_sigmoid(zf.astype(jnp.float32) + b_f.astype(jnp.float32))
    u = jax.nn.gelu(zu, approximate=False)
    vn = layer_norm(jax.nn.gelu(zgv, approximate=False), ln_g, ln_b)
    return q, k, v, logf, u, vn, zp, zg


def fox_attend(q, cq, qpos, k, ck, kpos, v):
    s = jnp.einsum('bqhd,bkhd->bhqk', q, k).astype(jnp.float32) * (HEAD_DIM ** -0.5)
    s = s + jnp.transpose(cq, (0, 2, 1))[..., None] - jnp.transpose(ck, (0, 2, 1))[:, :, None, :]
    allowed = kpos[None, :] <= qpos[:, None]
    s = jnp.where(allowed[None, None], s, -jnp.inf)
    p = jax.nn.softmax(s, axis=-1).astype(v.dtype)
    return jnp.einsum('bhqk,bkhd->bqhd', p, v)


def fox_prompt(q, k, v, logf):
    b, s = q.shape[:2]
    nb = s // Q_BLOCK
    c = jnp.cumsum(logf, axis=1)
    pos = jnp.arange(s)
    qb = jnp.moveaxis(q.reshape(b, nb, Q_BLOCK, FOX_HEADS, HEAD_DIM), 1, 0)
    cqb = jnp.moveaxis(c.reshape(b, nb, Q_BLOCK, FOX_HEADS), 1, 0)
    pb = pos.reshape(nb, Q_BLOCK)
    out = lax.map(lambda a: fox_attend(a[0], a[1], a[2], k, c, pos, v), (qb, cqb, pb))
    return jnp.moveaxis(out, 0, 1).reshape(b, s, FOX_WIDTH)


def sgu(u, vn, w_s, b_s):
    b, n, lc, _ = vn.shape
    t = jnp.arange(lc)
    mask = (t[None, :] // CHUNK) <= (t[:, None] // CHUNK)
    w = jnp.where(mask[None], w_s[:, :lc, :lc], jnp.zeros((), w_s.dtype))
    vg = vn.reshape(b, n, lc, GMLP_GROUPS, GMLP_GROUP_DIM)
    s = jnp.einsum('gts,bnsgc->bntgc', w, vg) + jnp.transpose(b_s[:, :lc])[None, None, :, :, None]
    return u * s.reshape(b, n, lc, GMLP_WIDTH)


def pool_mix(xp_hist, pos0, w_pool, pool_scale):
    b = xp_hist.shape[0]
    L = xp_hist.shape[1] - POOL_HIST
    xf = xp_hist.astype(jnp.float32)
    cs = jnp.cumsum(jnp.pad(xf, ((0, 0), (1, 0), (0, 0))), axis=1)
    upper = cs[:, POOL_HIST + 1:]
    x_new = xf[:, POOL_HIST:]
    pos = pos0 + jnp.arange(L)
    diffs = []
    for g, w in enumerate(POOL_WINDOWS):
        sl = slice(g * POOL_GROUP_DIM, (g + 1) * POOL_GROUP_DIM)
        lower = cs[:, POOL_HIST + 1 - w:POOL_HIST + 1 - w + L, sl]
        cnt = jnp.minimum(pos + 1, w).astype(jnp.float32)
        diffs.append((upper[..., sl] - lower) / cnt[None, :, None] - x_new[..., sl])
    d = jnp.stack(diffs, axis=2).astype(w_pool.dtype)
    y = jnp.einsum('blgc,gce->blge', d, w_pool).reshape(b, L, POOL_WIDTH)
    return y * pool_scale


def merge_branches(o_a, o_b, o_c, zg, w_br, w_out):
    ya = o_a @ w_br[:FOX_WIDTH]
    yb = o_b @ w_br[FOX_WIDTH:FOX_WIDTH + GMLP_WIDTH]
    yc = o_c.astype(o_a.dtype) @ w_br[FOX_WIDTH + GMLP_WIDTH:]
    g = jax.nn.sigmoid(zg.astype(jnp.float32)).astype(ya.dtype)
    g = g.reshape(zg.shape[:-1] + (N_BRANCH, D_MODEL))
    merged = g[..., 0, :] * ya + g[..., 1, :] * yb + g[..., 2, :] * yc
    return merged @ w_out


def peer_ffn(h, wq, keys, pu, pv):
    b, L, d = h.shape
    t = h.reshape(b * L, d)
    n = t.shape[0]
    nblk = -(-n // PEER_BLOCK)
    t = jnp.pad(t, ((0, nblk * PEER_BLOCK - n), (0, 0))).reshape(nblk, PEER_BLOCK, d)

    def block(xb):
        q = (xb @ wq).reshape(PEER_BLOCK, PEER_HEADS, 2, PEER_HALF)
        sc = jnp.einsum('thpc,hpnc->thpn', q, keys).astype(jnp.float32)
        s_half, i_half = lax.top_k(sc, PEER_TOPK)
        cand = (s_half[:, :, 0, :, None] + s_half[:, :, 1, None, :]).reshape(PEER_BLOCK, PEER_HEADS, PEER_TOPK * PEER_TOPK)
        cidx = (i_half[:, :, 0, :, None] * N_KEYS + i_half[:, :, 1, None, :]).reshape(PEER_BLOCK, PEER_HEADS, PEER_TOPK * PEER_TOPK)
        s_top, sel = lax.top_k(cand, PEER_TOPK)
        expert = jnp.take_along_axis(cidx, sel, axis=-1)
        gate = jax.nn.softmax(s_top, axis=-1)
        a = jnp.einsum('td,thkd->thk', xb, pu[expert])
        act = (jax.nn.gelu(a.astype(jnp.float32), approximate=False) * gate).astype(xb.dtype)
        return jnp.einsum('thk,thkd->td', act, pv[expert])

    out = lax.map(block, t).reshape(nblk * PEER_BLOCK, d)[:n]
    return out.reshape(b, L, d)


def setup_inputs(seed: int = 0) -> dict:
    key = jax.random.key(seed)
    ks = jax.random.split(key, 24)
    f32 = jnp.float32

    def nrm(k, shape, scale):
        return jax.random.normal(k, shape, f32) * scale

    return {
        'x_prompt': nrm(ks[0], (BATCH, SEQ, D_MODEL), 1.0),
        'x_sample': nrm(ks[1], (DEC_BATCH, DEC_SEQ, D_MODEL), 1.0),
        'cache_fox_k': nrm(ks[2], (DEPTH, DEC_BATCH, PAST_LEN, FOX_HEADS, HEAD_DIM), 1.0),
        'cache_fox_v': nrm(ks[3], (DEPTH, DEC_BATCH, PAST_LEN, FOX_HEADS, HEAD_DIM), 1.0),
        'cache_fox_logf': jax.nn.log_sigmoid(3.0 + jax.random.normal(ks[4], (DEPTH, DEC_BATCH, PAST_LEN, FOX_HEADS), f32)),
        'state_pool': nrm(ks[5], (DEPTH, DEC_BATCH, POOL_HIST, POOL_WIDTH), 1.0),
        'norm1_g': 1.0 + nrm(ks[6], (DEPTH, D_MODEL), 0.05),
        'w_in': nrm(ks[7], (DEPTH, D_MODEL, IN_WIDTH), D_MODEL ** -0.5),
        'b_f': 2.0 + 2.0 * jax.random.uniform(ks[8], (DEPTH, FOX_HEADS), f32),
        'ln_g': 1.0 + nrm(ks[9], (DEPTH, GMLP_WIDTH), 0.05),
        'ln_b': nrm(ks[10], (DEPTH, GMLP_WIDTH), 0.02),
        'w_s': nrm(ks[11], (DEPTH, GMLP_GROUPS, GMLP_CHUNK, GMLP_CHUNK), 0.5 * GMLP_CHUNK ** -0.5),
        'b_s': 1.0 + nrm(ks[12], (DEPTH, GMLP_GROUPS, GMLP_CHUNK), 0.1),
        'w_pool': nrm(ks[13], (DEPTH, POOL_GROUPS, POOL_GROUP_DIM, POOL_GROUP_DIM), POOL_GROUP_DIM ** -0.5),
        'pool_scale': 1.0 + nrm(ks[14], (DEPTH, POOL_WIDTH), 0.1),
        'w_br': nrm(ks[15], (DEPTH, MIX_WIDTH, D_MODEL), FOX_WIDTH ** -0.5),
        'w_out': nrm(ks[16], (DEPTH, D_MODEL, D_MODEL), D_MODEL ** -0.5),
        'norm2_g': 1.0 + nrm(ks[17], (DEPTH, D_MODEL), 0.05),
        'peer_wq': nrm(ks[18], (DEPTH, D_MODEL, PEER_HEADS * PEER_QDIM), D_MODEL ** -0.5),
        'peer_keys': nrm(ks[19], (DEPTH, PEER_HEADS, 2, N_KEYS, PEER_HALF), PEER_HALF ** -0.5),
        'peer_u': nrm(ks[20], (DEPTH, N_EXPERTS, D_MODEL), D_MODEL ** -0.5),
        'peer_v': nrm(ks[21], (DEPTH, N_EXPERTS, D_MODEL), 0.5 * PEER_HEADS ** -0.5),
        'final_g': 1.0 + nrm(ks[22], (D_MODEL,), 0.05),
    }


def reference(x_prompt, x_sample, cache_fox_k, cache_fox_v, cache_fox_logf, state_pool,
              norm1_g, w_in, b_f, ln_g, ln_b, w_s, b_s, w_pool, pool_scale, w_br, w_out,
              norm2_g, peer_wq, peer_keys, peer_u, peer_v, final_g):
    xp, xs = x_prompt, x_sample
    bp, sp = xp.shape[:2]
    bd, ls = xs.shape[:2]
    past = cache_fox_k.shape[2]
    pk, pv_, plf, ppool = [], [], [], []
    sk, sv, slf, spool, sgv = [], [], [], [], []
    for l in range(DEPTH):
        h = rms_norm(xp, norm1_g[l])
        q, k, v, logf, u, vn, zp, zg = mixer_inputs(h, w_in[l], b_f[l], ln_g[l], ln_b[l])
        o_a = fox_prompt(q, k, v, logf)
        nch = sp // GMLP_CHUNK
        o_b = sgu(u.reshape(bp, nch, GMLP_CHUNK, GMLP_WIDTH), vn.reshape(bp, nch, GMLP_CHUNK, GMLP_WIDTH),
                  w_s[l], b_s[l]).reshape(bp, sp, GMLP_WIDTH)
        zp_h = jnp.pad(zp, ((0, 0), (POOL_HIST, 0), (0, 0)))
        o_c = pool_mix(zp_h, 0, w_pool[l], pool_scale[l])
        xp = xp + merge_branches(o_a, o_b, o_c, zg, w_br[l], w_out[l])
        xp = xp + peer_ffn(rms_norm(xp, norm2_g[l]), peer_wq[l], peer_keys[l], peer_u[l], peer_v[l])
        pk.append(k)
        pv_.append(v)
        plf.append(logf)
        ppool.append(zp_h[:, -POOL_HIST:])

        h = rms_norm(xs, norm1_g[l])
        q, k, v, logf, u, vn, zp, zg = mixer_inputs(h, w_in[l], b_f[l], ln_g[l], ln_b[l])
        k_all = jnp.concatenate([cache_fox_k[l].astype(k.dtype), k], axis=1)
        v_all = jnp.concatenate([cache_fox_v[l].astype(v.dtype), v], axis=1)
        c_all = jnp.cumsum(jnp.concatenate([cache_fox_logf[l].astype(jnp.float32), logf], axis=1), axis=1)
        kpos = jnp.arange(past + ls)
        qpos = past + jnp.arange(ls)
        o_a = fox_attend(q, c_all[:, past:], qpos, k_all, c_all, kpos, v_all).reshape(bd, ls, FOX_WIDTH)
        o_b = sgu(u[:, None], vn[:, None], w_s[l], b_s[l])[:, 0]
        zp_h = jnp.concatenate([state_pool[l].astype(zp.dtype), zp], axis=1)
        o_c = pool_mix(zp_h, past, w_pool[l], pool_scale[l])
        xs = xs + merge_branches(o_a, o_b, o_c, zg, w_br[l], w_out[l])
        xs = xs + peer_ffn(rms_norm(xs, norm2_g[l]), peer_wq[l], peer_keys[l], peer_u[l], peer_v[l])
        sk.append(k)
        sv.append(v)
        slf.append(logf)
        spool.append(zp_h[:, -POOL_HIST:])
        sgv.append(vn)

    y_prompt = rms_norm(xp, final_g)
    y_sample = rms_norm(xs, final_g)
    new_prompt_fox_k = jnp.stack(pk)
    new_prompt_fox_v = jnp.stack(pv_)
    new_prompt_fox_logf = jnp.stack(plf)
    new_prompt_pool = jnp.stack(ppool)
    new_sample_fox_k = jnp.stack(sk)
    new_sample_fox_v = jnp.stack(sv)
    new_sample_fox_logf = jnp.stack(slf)
    new_sample_pool = jnp.stack(spool)
    new_sample_gmlp_v = jnp.stack(sgv)
    return (y_prompt, y_sample, new_prompt_fox_k, new_prompt_fox_v, new_prompt_fox_logf, new_prompt_pool,
            new_sample_fox_k, new_sample_fox_v, new_sample_fox_logf, new_sample_pool, new_sample_gmlp_v)
```

```python
import functools

import jax
import jax.numpy as jnp
from jax import lax
from jax.experimental import pallas as pl
from jax.experimental.pallas import tpu as pltpu

f32 = jnp.float32
bf16 = jnp.bfloat16

EPS = 1e-6
FOX_HEADS = 8
HEAD_DIM = 64
FOX_WIDTH = FOX_HEADS * HEAD_DIM
GMLP_GROUPS = 4
GMLP_WIDTH = 256
GMLP_CHUNK = 128
SGU_BLOCK = 64
POOL_GROUPS = 4
POOL_WIDTH = 256
POOL_WINDOWS = (2, 4, 8, 16)
POOL_HIST = 15
HALO = 16
N_BRANCH = 3
PEER_HEADS = 8
N_KEYS = 128
PEER_TOPK = 16
PEER_HALF = 128

LANES = 128
SUBLANES = 8
EXPERT_CHUNK = SUBLANES * N_KEYS
VMEM_LIMIT = 56 * 1024 * 1024
NEG_INF = float("-inf")


def _params(*sem):
    return pltpu.CompilerParams(dimension_semantics=sem, vmem_limit_bytes=VMEM_LIMIT)


def _const_spec(shape):
    nd = len(shape)
    return pl.BlockSpec(shape, lambda *_: (0,) * nd)


def _rms(x, g):
    return x * lax.rsqrt(jnp.mean(x * x, axis=-1, keepdims=True) + EPS) * g


def _gelu(x):
    return 0.5 * x * (1.0 + lax.erf(x * 0.7071067811865476))


def _inproj_body(x_ref, g1_ref, wqkv_ref, wf_ref, bf_ref, wugp_ref, lng_ref, lnb_ref, wg_ref,
                 q_ref, k_ref, v_ref, kb_ref, vb_ref, lf_ref, u_ref, vn_ref, zp_ref, gate_ref):
    h = _rms(x_ref[...], g1_ref[...]).astype(bf16)
    zqkv = jnp.dot(h, wqkv_ref[...], preferred_element_type=f32)
    q_ref[...] = (zqkv[:, :FOX_WIDTH] * (HEAD_DIM ** -0.5)).astype(bf16)
    k = zqkv[:, FOX_WIDTH:2 * FOX_WIDTH]
    v = zqkv[:, 2 * FOX_WIDTH:]
    k_ref[...] = k
    v_ref[...] = v
    kb_ref[...] = k.astype(bf16)
    vb_ref[...] = v.astype(bf16)
    zf = jnp.dot(h, wf_ref[...], preferred_element_type=f32) + bf_ref[...]
    lf_ref[...] = jax.nn.log_sigmoid(zf)[:, :FOX_HEADS]
    zugp = jnp.dot(h, wugp_ref[...], preferred_element_type=f32)
    u_ref[...] = _gelu(zugp[:, :GMLP_WIDTH])
    gv = _gelu(zugp[:, GMLP_WIDTH:2 * GMLP_WIDTH])
    mu = jnp.mean(gv, axis=-1, keepdims=True)
    gc = gv - mu
    vn_ref[...] = gc * lax.rsqrt(jnp.mean(gc * gc, axis=-1, keepdims=True) + EPS) * lng_ref[...] + lnb_ref[...]
    zp_ref[...] = zugp[:, 2 * GMLP_WIDTH:]
    zg = jnp.dot(h, wg_ref[...], preferred_element_type=f32)
    gate_ref[...] = 1.0 / (1.0 + jnp.exp(-zg))


def _inproj(x, w, tn):
    n, d = x.shape
    row = lambda width: pl.BlockSpec((tn, width), lambda i: (i, 0))
    outs = [
        (FOX_WIDTH, bf16), (FOX_WIDTH, f32), (FOX_WIDTH, f32), (FOX_WIDTH, bf16), (FOX_WIDTH, bf16),
        (FOX_HEADS, f32), (GMLP_WIDTH, f32), (GMLP_WIDTH, f32), (POOL_WIDTH, f32), (N_BRANCH * d, f32),
    ]
    weights = [w["g1"], w["wqkv"], w["wf"], w["bf"], w["wugp"], w["lng"], w["lnb"], w["wg"]]
    return pl.pallas_call(
        _inproj_body,
        grid=(n // tn,),
        in_specs=[row(d)] + [_const_spec(a.shape) for a in weights],
        out_specs=[row(width) for width, _ in outs],
        out_shape=[jax.ShapeDtypeStruct((n, width), dt) for width, dt in outs],
        compiler_params=_params("parallel"),
        name="inproj",
    )(x, *weights)


def _cumsum_body(x_ref, o_ref):
    x = x_ref[0]
    n = x.shape[-1]
    lane = lax.broadcasted_iota(jnp.int32, x.shape, 1)
    shift = 1
    while shift < n:
        x = x + jnp.where(lane >= shift, pltpu.roll(x, shift, 1), 0.0)
        shift *= 2
    o_ref[0] = x


def _cumsum_lanes(x):
    b, r, n = x.shape
    spec = pl.BlockSpec((1, r, n), lambda i: (i, 0, 0))
    return pl.pallas_call(
        _cumsum_body, grid=(b,), in_specs=[spec], out_specs=spec,
        out_shape=jax.ShapeDtypeStruct(x.shape, f32),
        compiler_params=_params("parallel"), name="cumsum",
    )(x)


def _attn_body(q_ref, k_ref, v_ref, cq_ref, ck_ref, o_ref, *, tq, tk, q_off):
    qi = pl.program_id(2)
    q = q_ref[0]
    lane = lax.broadcasted_iota(jnp.int32, (tq, LANES), 1)
    row0 = q_off + qi * tq
    n_full = (row0 + 1) // tk
    n_all = (row0 + tq + tk - 1) // tk
    accs = []
    for hh in range(2):
        own = (lane < HEAD_DIM) if hh == 0 else (lane >= HEAD_DIM)
        qm = jnp.where(own, q, jnp.zeros_like(q))
        cq = cq_ref[0, 0, :, hh:hh + 1]

        def step(t, carry, masked):
            m, l, acc = carry
            start = pl.multiple_of(t * tk, tk)
            ks = k_ref[0, pl.ds(start, tk), :]
            vs = v_ref[0, pl.ds(start, tk), :]
            s = lax.dot_general(qm, ks, (((1,), (1,)), ((), ())), preferred_element_type=f32)
            s = s + cq - ck_ref[0, 0, hh:hh + 1, pl.ds(start, tk)]
            if masked:
                kpos = start + lax.broadcasted_iota(jnp.int32, (tq, tk), 1)
                qpos = row0 + lax.broadcasted_iota(jnp.int32, (tq, tk), 0)
                s = jnp.where(kpos <= qpos, s, NEG_INF)
            m_new = jnp.maximum(m, jnp.max(s, axis=1, keepdims=True))
            alpha = jnp.exp(m - m_new)
            p = jnp.exp(s - m_new)
            l = alpha * l + jnp.sum(p, axis=1, keepdims=True)
            acc = alpha * acc + jnp.dot(p.astype(bf16), vs, preferred_element_type=f32)
            return m_new, l, acc

        init = (jnp.full((tq, 1), NEG_INF, f32), jnp.zeros((tq, 1), f32), jnp.zeros((tq, LANES), f32))
        carry = lax.fori_loop(0, n_full, functools.partial(step, masked=False), init)
        m, l, acc = lax.fori_loop(n_full, n_all, functools.partial(step, masked=True), carry)
        accs.append(acc / l)
    o_ref[0] = jnp.where(lane < HEAD_DIM, accs[0], accs[1]).astype(o_ref.dtype)


def _attention(q, k, v, cq, ck, *, tq, tk, q_off):
    b, lq, _ = q.shape
    lk = k.shape[1]
    hp = FOX_HEADS // 2
    return pl.pallas_call(
        functools.partial(_attn_body, tq=tq, tk=tk, q_off=q_off),
        grid=(b, hp, lq // tq),
        in_specs=[
            pl.BlockSpec((1, tq, LANES), lambda b_, h_, i: (b_, i, h_)),
            pl.BlockSpec((1, lk, LANES), lambda b_, h_, i: (b_, 0, h_)),
            pl.BlockSpec((1, lk, LANES), lambda b_, h_, i: (b_, 0, h_)),
            pl.BlockSpec((1, 1, tq, 2), lambda b_, h_, i: (b_, h_, i, 0)),
            pl.BlockSpec((1, 1, 2, lk), lambda b_, h_, i: (b_, h_, 0, 0)),
        ],
        out_specs=pl.BlockSpec((1, tq, LANES), lambda b_, h_, i: (b_, i, h_)),
        out_shape=jax.ShapeDtypeStruct((b, lq, FOX_WIDTH), bf16),
        compiler_params=_params("parallel", "parallel", "arbitrary"),
        name="fox_attention",
    )(q, k, v, cq, ck)


def _merge_body(x_ref, oa_ref, u_ref, vn_ref, zp_ref, prev_ref, gate_ref, ws_ref, bs_ref, wpool_ref, pscale_ref,
                wbr_ref, wout_ref, o_ref, *, tn, lc, pos0):
    j = pl.program_id(1)
    d = x_ref.shape[-1]
    t_io = lax.broadcasted_iota(jnp.int32, (lc, lc), 0)
    s_io = lax.broadcasted_iota(jnp.int32, (lc, lc), 1)
    causal = (s_io // SGU_BLOCK) <= (t_io // SGU_BLOCK)
    grp = lax.broadcasted_iota(jnp.int32, (lc, GMLP_WIDTH), 1) // (GMLP_WIDTH // GMLP_GROUPS)
    wmask = [jnp.where(causal, ws_ref[g], 0.0).astype(bf16) for g in range(GMLP_GROUPS)]
    ob = []
    for c in range(tn // lc):
        vc = vn_ref[0, c * lc:(c + 1) * lc, :].astype(bf16)
        s = bs_ref[...]
        for g in range(GMLP_GROUPS):
            sg = jnp.dot(wmask[g], vc, preferred_element_type=f32)
            s = s + jnp.where(grp == g, sg, 0.0)
        ob.append(u_ref[0, c * lc:(c + 1) * lc, :] * s)
    o_b = ob[0] if len(ob) == 1 else jnp.concatenate(ob, axis=0)
    zp = zp_ref[0]
    ext = jnp.concatenate([prev_ref[0, 0], zp], axis=0)
    s2 = ext + pltpu.roll(ext, 1, 0)
    s4 = s2 + pltpu.roll(s2, 2, 0)
    s8 = s4 + pltpu.roll(s4, 4, 0)
    s16 = s8 + pltpu.roll(s8, 8, 0)
    pgrp = lax.broadcasted_iota(jnp.int32, (tn, POOL_WIDTH), 1) // (POOL_WIDTH // POOL_GROUPS)
    pos = pos0 + j * tn + lax.broadcasted_iota(jnp.int32, (tn, POOL_WIDTH), 0)
    win = jnp.where(pgrp == 0, s2[HALO:], jnp.where(pgrp == 1, s4[HALO:], jnp.where(pgrp == 2, s8[HALO:], s16[HALO:])))
    width = jnp.where(pgrp == 0, 2, jnp.where(pgrp == 1, 4, jnp.where(pgrp == 2, 8, 16)))
    cnt = jnp.minimum(pos + 1, width).astype(f32)
    dlt = win / cnt - zp
    o_c = jnp.dot(dlt.astype(bf16), wpool_ref[...], preferred_element_type=f32) * pscale_ref[...]
    ya = jnp.dot(oa_ref[0], wbr_ref[:FOX_WIDTH, :], preferred_element_type=f32)
    yb = jnp.dot(o_b.astype(bf16), wbr_ref[FOX_WIDTH:FOX_WIDTH + GMLP_WIDTH, :], preferred_element_type=f32)
    yc = jnp.dot(o_c.astype(bf16), wbr_ref[FOX_WIDTH + GMLP_WIDTH:, :], preferred_element_type=f32)
    merged = gate_ref[0, :, :d] * ya + gate_ref[0, :, d:2 * d] * yb + gate_ref[0, :, 2 * d:] * yc
    o_ref[0] = x_ref[0] + jnp.dot(merged.astype(bf16), wout_ref[...], preferred_element_type=f32)


def _merge(x, oa, u, vn, zp, prev, gate, w, *, tn, lc, pos0):
    nseq, L, d = x.shape
    seq = lambda width: pl.BlockSpec((1, tn, width), lambda b_, j: (b_, j, 0))
    weights = [w["ws"], w["bs"], w["wpool"], w["pscale"], w["wbr"], w["wout"]]
    return pl.pallas_call(
        functools.partial(_merge_body, tn=tn, lc=lc, pos0=pos0),
        grid=(nseq, L // tn),
        in_specs=[seq(d), seq(FOX_WIDTH), seq(GMLP_WIDTH), seq(GMLP_WIDTH), seq(POOL_WIDTH),
                  pl.BlockSpec((1, 1, HALO, POOL_WIDTH), lambda b_, j: (b_, j, 0, 0)), seq(N_BRANCH * d)]
        + [_const_spec(a.shape) for a in weights],
        out_specs=seq(d),
        out_shape=jax.ShapeDtypeStruct(x.shape, f32),
        compiler_params=_params("parallel", "parallel"),
        name="merge",
    )(x, oa, u, vn, zp, prev, gate, *weights)


def _top_values(block, k):
    tops = []
    for _ in range(k):
        m = jnp.max(block, axis=0, keepdims=True)
        tops.append(m)
        block = jnp.where(block == m, NEG_INF, block)
    return tops


def _route_body(x_ref, g2_ref, wqt_ref, keys_ref, ht_ref, sc_ref, st_ref, top_ref, *, tn):
    h = _rms(x_ref[...], g2_ref[...]).astype(bf16)
    ht = h.T
    ht_ref[...] = ht
    qt = jnp.dot(wqt_ref[...], ht, preferred_element_type=f32).astype(bf16)
    n_sets = 2 * PEER_HEADS
    for r in range(n_sets):
        sc_ref[r * N_KEYS:(r + 1) * N_KEYS, :] = jnp.dot(
            keys_ref[r], qt[r * PEER_HALF:(r + 1) * PEER_HALF, :], preferred_element_type=f32)

    def half_top(r, _):
        base = pl.multiple_of(r * N_KEYS, N_KEYS)
        for tb in range(tn // LANES):
            blk = sc_ref[pl.ds(base, N_KEYS), tb * LANES:(tb + 1) * LANES]
            top_ref[r, :, tb * LANES:(tb + 1) * LANES] = jnp.concatenate(_top_values(blk, PEER_TOPK), axis=0)
        return 0

    lax.fori_loop(0, n_sets, half_top, 0)

    def head_stats(hd, _):
        for tb in range(tn // LANES):
            sl = slice(tb * LANES, (tb + 1) * LANES)
            a = top_ref[2 * hd, :, sl]
            b = top_ref[2 * hd + 1, :, sl]
            rows8 = lax.broadcasted_iota(jnp.int32, (8, LANES), 0)
            parts = [a[0:1] + b]
            for p in range(1, 8):
                parts.append(jnp.where(rows8 < PEER_TOPK // (p + 1), a[p:p + 1] + b[0:8], NEG_INF))
            parts.append(a[8:16] + b[0:1])
            cand = jnp.concatenate(parts, axis=0)
            tau = _top_values(cand, PEER_TOPK)[-1]
            top = a[0:1] + b[0:1]
            z = jnp.sum(jnp.where(cand >= tau, jnp.exp(cand - top), 0.0), axis=0, keepdims=True)
            st_ref[hd, :, sl] = jnp.concatenate(
                [a[0:1], b[0:1], tau, 1.0 / z, jnp.zeros((4, LANES), f32)], axis=0)
        return 0

    lax.fori_loop(0, PEER_HEADS, head_stats, 0)


def _route(x, w, tn):
    n, d = x.shape
    n_sets = 2 * PEER_HEADS
    return pl.pallas_call(
        functools.partial(_route_body, tn=tn),
        grid=(n // tn,),
        in_specs=[pl.BlockSpec((tn, d), lambda i: (i, 0)), _const_spec(w["g2"].shape), _const_spec(w["wqt"].shape),
                  _const_spec(w["keys"].shape)],
        out_specs=[pl.BlockSpec((d, tn), lambda i: (0, i)), pl.BlockSpec((n_sets * N_KEYS, tn), lambda i: (0, i)),
                   pl.BlockSpec((PEER_HEADS, 8, tn), lambda i: (0, 0, i))],
        out_shape=[jax.ShapeDtypeStruct((d, n), bf16), jax.ShapeDtypeStruct((n_sets * N_KEYS, n), f32),
                   jax.ShapeDtypeStruct((PEER_HEADS, 8, n), f32)],
        scratch_shapes=[pltpu.VMEM((n_sets, PEER_TOPK, tn), f32)],
        compiler_params=_params("parallel"),
        name="peer_route",
    )(x, w["g2"], w["wqt"], w["keys"])


def _experts_body(x_ref, ht_ref, sc_ref, st_ref, pu_ref, pvt_ref, o_ref, p_ref, q_ref, a_ref, act_ref, acc_ref,
                  *, tn, ec):
    c = pl.program_id(1)
    n_sub = ec // N_KEYS

    @pl.when(c == 0)
    def _():
        for hd in range(PEER_HEADS):
            s1 = sc_ref[(2 * hd) * N_KEYS:(2 * hd + 1) * N_KEYS, :]
            s2 = sc_ref[(2 * hd + 1) * N_KEYS:(2 * hd + 2) * N_KEYS, :]
            p_ref[hd] = jnp.exp(s1 - st_ref[hd, 0:1, :])
            q_ref[hd] = jnp.exp(s2 - st_ref[hd, 1:2, :]) * st_ref[hd, 3:4, :]
        acc_ref[...] = jnp.zeros_like(acc_ref)

    a_ref[...] = jnp.dot(pu_ref[...], ht_ref[...], preferred_element_type=f32)

    def sub_block(tb, _):
        lanes = pl.ds(pl.multiple_of(tb * LANES, LANES), LANES)
        i0 = pl.multiple_of(c * n_sub, n_sub)
        s1_rows = [sc_ref[pl.ds(2 * hd * N_KEYS + i0, n_sub), lanes] for hd in range(PEER_HEADS)]
        p_rows = [p_ref[hd, pl.ds(i0, n_sub), lanes] for hd in range(PEER_HEADS)]
        for ii in range(n_sub):
            gsum = jnp.zeros((N_KEYS, LANES), f32)
            for hd in range(PEER_HEADS):
                s1row = s1_rows[hd][ii:ii + 1]
                prow = p_rows[hd][ii:ii + 1]
                s2 = sc_ref[(2 * hd + 1) * N_KEYS:(2 * hd + 2) * N_KEYS, lanes]
                sel = (s2 + s1row) >= st_ref[hd, 2:3, lanes]
                gsum = gsum + jnp.where(sel, q_ref[hd, :, lanes] * prow, 0.0)
            a = a_ref[ii * N_KEYS:(ii + 1) * N_KEYS, lanes]
            act_ref[ii * N_KEYS:(ii + 1) * N_KEYS, lanes] = (_gelu(a) * gsum).astype(bf16)
        return 0

    lax.fori_loop(0, tn // LANES, sub_block, 0)
    acc_ref[...] += jnp.dot(pvt_ref[...], act_ref[...], preferred_element_type=f32)

    @pl.when(c == pl.num_programs(1) - 1)
    def _():
        o_ref[...] = x_ref[...] + acc_ref[...].T


def _experts(x, ht, sc, st, w, *, tn, ec):
    n, d = x.shape
    n_exp = w["pu"].shape[0]
    n_sets = 2 * PEER_HEADS
    return pl.pallas_call(
        functools.partial(_experts_body, tn=tn, ec=ec),
        grid=(n // tn, n_exp // ec),
        in_specs=[
            pl.BlockSpec((tn, d), lambda t, c: (t, 0)),
            pl.BlockSpec((d, tn), lambda t, c: (0, t)),
            pl.BlockSpec((n_sets * N_KEYS, tn), lambda t, c: (0, t)),
            pl.BlockSpec((PEER_HEADS, 8, tn), lambda t, c: (0, 0, t)),
            pl.BlockSpec((ec, d), lambda t, c: (c, 0)),
            pl.BlockSpec((d, ec), lambda t, c: (0, c)),
        ],
        out_specs=pl.BlockSpec((tn, d), lambda t, c: (t, 0)),
        out_shape=jax.ShapeDtypeStruct((n, d), f32),
        scratch_shapes=[
            pltpu.VMEM((PEER_HEADS, N_KEYS, tn), f32), pltpu.VMEM((PEER_HEADS, N_KEYS, tn), f32),
            pltpu.VMEM((ec, tn), f32), pltpu.VMEM((ec, tn), bf16), pltpu.VMEM((d, tn), f32),
        ],
        compiler_params=_params("parallel", "arbitrary"),
        name="peer_experts",
    )(x, ht, sc, st, w["pu"], w["pvt"])


def _final_body(x_ref, g_ref, o_ref):
    o_ref[...] = _rms(x_ref[...], g_ref[...])


def _final_norm(x, g, tn):
    n, d = x.shape
    spec = pl.BlockSpec((tn, d), lambda i: (i, 0))
    return pl.pallas_call(
        _final_body, grid=(n // tn,), in_specs=[spec, _const_spec(g.shape)], out_specs=spec,
        out_shape=jax.ShapeDtypeStruct(x.shape, f32), compiler_params=_params("parallel"), name="final_norm",
    )(x, g)


def _layer_weights(l, norm1_g, w_in, b_f, ln_g, ln_b, w_s, b_s, w_pool, pool_scale, w_br, w_out, norm2_g, peer_wq,
                   peer_keys, peer_u, peer_v, lc):
    d = w_in.shape[1]
    wi = w_in[l]
    o_f = 3 * FOX_WIDTH
    o_u = o_f + FOX_HEADS
    o_g = o_u + 2 * GMLP_WIDTH + POOL_WIDTH
    wf = jnp.zeros((d, LANES), f32).at[:, :FOX_HEADS].set(wi[:, o_f:o_u])
    bf = jnp.zeros((1, LANES), f32).at[0, :FOX_HEADS].set(b_f[l])
    gdim = GMLP_WIDTH // GMLP_GROUPS
    pdim = POOL_WIDTH // POOL_GROUPS
    wpool = jnp.zeros((POOL_WIDTH, POOL_WIDTH), f32)
    for g in range(POOL_GROUPS):
        wpool = wpool.at[g * pdim:(g + 1) * pdim, g * pdim:(g + 1) * pdim].set(w_pool[l, g])
    return {
        "g1": norm1_g[l][None], "wqkv": wi[:, :o_f].astype(bf16), "wf": wf.astype(bf16), "bf": bf,
        "wugp": wi[:, o_u:o_g].astype(bf16), "lng": ln_g[l][None], "lnb": ln_b[l][None], "wg": wi[:, o_g:].astype(bf16),
        "ws": w_s[l][:, :lc, :lc], "bs": jnp.repeat(jnp.transpose(b_s[l][:, :lc]), gdim, axis=1),
        "wpool": wpool.astype(bf16), "pscale": pool_scale[l][None],
        "wbr": w_br[l].astype(bf16), "wout": w_out[l].astype(bf16),
        "g2": norm2_g[l][None], "wqt": jnp.transpose(peer_wq[l]).astype(bf16),
        "keys": peer_keys[l].reshape(2 * PEER_HEADS, N_KEYS, PEER_HALF).astype(bf16),
        "pu": peer_u[l].astype(bf16), "pvt": jnp.transpose(peer_v[l]).astype(bf16),
    }


def _pick(n, prefs):
    for t in prefs:
        if n % t == 0:
            return t
    return n


def _layer(x, w, *, k_hist, v_hist, logf_hist, pool_hist, lc, tiles):
    nseq, L, d = x.shape
    n = nseq * L
    q, k, v, kb, vb, logf, u, vn, zp, gate = _inproj(x.reshape(n, d), w, tiles["inproj"])
    r3 = lambda a: a.reshape(nseq, L, a.shape[-1])
    q, kb, vb, logf3, u, vn, zp, gate = map(r3, (q, kb, vb, logf, u, vn, zp, gate))
    past = 0 if k_hist is None else k_hist.shape[1]
    if past:
        kb = jnp.concatenate([k_hist.reshape(nseq, past, FOX_WIDTH).astype(bf16), kb], axis=1)
        vb = jnp.concatenate([v_hist.reshape(nseq, past, FOX_WIDTH).astype(bf16), vb], axis=1)
        lf_all = jnp.concatenate([logf_hist.astype(f32), logf3], axis=1)
    else:
        lf_all = logf3
    lk = past + L
    tk = tiles["tk"]
    lk_pad = -(-lk // tk) * tk
    if lk_pad != lk:
        padk = ((0, 0), (0, lk_pad - lk), (0, 0))
        kb, vb, lf_all = jnp.pad(kb, padk), jnp.pad(vb, padk), jnp.pad(lf_all, padk)
    hp = FOX_HEADS // 2
    c_t = _cumsum_lanes(jnp.transpose(lf_all, (0, 2, 1)))
    ck = c_t.reshape(nseq, hp, 2, lk_pad)
    cq = jnp.transpose(ck[:, :, :, past:past + L], (0, 1, 3, 2))
    o_a = _attention(q, kb, vb, cq, ck, tq=tiles["tq"], tk=tk, q_off=past)
    tn = tiles["merge"]
    nt = L // tn
    first = jnp.zeros((nseq, HALO, POOL_WIDTH), f32)
    if pool_hist is not None:
        first = first.at[:, HALO - POOL_HIST:].set(pool_hist.astype(f32))
    tails = zp.reshape(nseq, nt, tn, POOL_WIDTH)[:, :nt - 1, tn - HALO:, :]
    prev = jnp.concatenate([first[:, None], tails], axis=1)
    x_mid = _merge(x, o_a, u, vn, zp, prev, gate, w, tn=tn, lc=lc, pos0=past)
    ht, sc, st = _route(x_mid.reshape(n, d), w, tiles["route"])
    x_new = _experts(x_mid.reshape(n, d), ht, sc, st, w, tn=tiles["experts"], ec=tiles["ec"])
    new_pool = zp[:, L - POOL_HIST:] if pool_hist is None or L >= POOL_HIST else None
    return x_new.reshape(nseq, L, d), r3(k), r3(v), logf3, new_pool, vn


def kernel(x_prompt, x_sample, cache_fox_k, cache_fox_v, cache_fox_logf, state_pool, norm1_g, w_in, b_f, ln_g, ln_b,
           w_s, b_s, w_pool, pool_scale, w_br, w_out, norm2_g, peer_wq, peer_keys, peer_u, peer_v, final_g):
    depth = w_in.shape[0]
    bp, sp, d = x_prompt.shape
    bd, ls, _ = x_sample.shape
    assert sp % GMLP_CHUNK == 0 and ls <= GMLP_CHUNK and ls >= POOL_HIST and ls % 16 == 0
    params = (norm1_g, w_in, b_f, ln_g, ln_b, w_s, b_s, w_pool, pool_scale, w_br, w_out, norm2_g, peer_wq, peer_keys,
              peer_u, peer_v)
    tiles_p = {"inproj": _pick(bp * sp, (256,)), "tq": _pick(sp, (512, 256, 128)), "tk": _pick(sp, (512, 256, 128)),
               "merge": _pick(sp, (256, 128)), "route": _pick(bp * sp, (512, 256, 128)),
               "experts": _pick(bp * sp, (512, 256, 128)), "ec": EXPERT_CHUNK}
    ns = bd * ls
    tiles_s = {"inproj": ns, "tq": ls, "tk": 128, "merge": ls, "route": ns, "experts": ns, "ec": EXPERT_CHUNK}
    xp, xs = x_prompt, x_sample
    outs_p, outs_s = [], []
    for l in range(depth):
        wp = _layer_weights(l, *params, lc=GMLP_CHUNK)
        xp, k, v, lf, pool, _ = _layer(xp, wp, k_hist=None, v_hist=None, logf_hist=None, pool_hist=None,
                                       lc=GMLP_CHUNK, tiles=tiles_p)
        outs_p.append((k.reshape(bp, sp, FOX_HEADS, HEAD_DIM), v.reshape(bp, sp, FOX_HEADS, HEAD_DIM), lf, pool))
        wsm = dict(wp)
        wsm["ws"] = w_s[l][:, :ls, :ls]
        wsm["bs"] = jnp.repeat(jnp.transpose(b_s[l][:, :ls]), GMLP_WIDTH // GMLP_GROUPS, axis=1)
        xs, k, v, lf, pool, vn = _layer(xs, wsm, k_hist=cache_fox_k[l], v_hist=cache_fox_v[l],
                                        logf_hist=cache_fox_logf[l], pool_hist=state_pool[l], lc=ls, tiles=tiles_s)
        outs_s.append((k.reshape(bd, ls, FOX_HEADS, HEAD_DIM), v.reshape(bd, ls, FOX_HEADS, HEAD_DIM), lf, pool, vn))
    g = final_g[None]
    y_prompt = _final_norm(xp.reshape(bp * sp, d), g, _pick(bp * sp, (512, 256, 128))).reshape(bp, sp, d)
    y_sample = _final_norm(xs.reshape(ns, d), g, ns).reshape(bd, ls, d)
    stack = lambda outs, i: jnp.stack([o[i] for o in outs])
    return (y_prompt, y_sample, stack(outs_p, 0), stack(outs_p, 1), stack(outs_p, 2), stack(outs_p, 3),
            stack(outs_s, 0), stack(outs_s, 1), stack(outs_s, 2), stack(outs_s, 3), stack(outs_s, 4))
```

```python
import functools

import jax
import jax.numpy as jnp
from jax import lax
from jax.experimental import pallas as pl
from jax.experimental.pallas import tpu as pltpu

f32 = jnp.float32
bf16 = jnp.bfloat16

EPS = 1e-6
FOX_HEADS = 8
HEAD_DIM = 64
FOX_WIDTH = FOX_HEADS * HEAD_DIM
GMLP_GROUPS = 4
GMLP_WIDTH = 256
GMLP_CHUNK = 128
SGU_BLOCK = 64
POOL_GROUPS = 4
POOL_WIDTH = 256
POOL_WINDOWS = (2, 4, 8, 16)
POOL_HIST = 15
HALO = 16
N_BRANCH = 3
PEER_HEADS = 8
N_KEYS = 128
PEER_TOPK = 16
PEER_HALF = 128

LANES = 128
SUBLANES = 8
EXPERT_CHUNK = SUBLANES * N_KEYS
TOP_ROWS = 3 * SUBLANES
VMEM_LIMIT = 56 * 1024 * 1024
NEG_INF = float("-inf")
LOG2E = 1.4426950408889634
C_PIECES = 3


def _params(*sem):
    return pltpu.CompilerParams(dimension_semantics=sem, vmem_limit_bytes=VMEM_LIMIT)


def _const_spec(shape):
    nd = len(shape)
    return pl.BlockSpec(shape, lambda *_: (0,) * nd)


def _rms(x, g):
    return x * lax.rsqrt(jnp.mean(x * x, axis=-1, keepdims=True) + EPS) * g


def _gelu(x):
    return 0.5 * x * (1.0 + lax.erf(x * 0.7071067811865476))


def _inproj_body(x_ref, g1_ref, wqkv_ref, wf_ref, bf_ref, wugp_ref, lng_ref, lnb_ref, wg_ref,
                 q_ref, k_ref, v_ref, kb_ref, vb_ref, lf_ref, u_ref, vn_ref, zp_ref, gate_ref):
    h = _rms(x_ref[...], g1_ref[...]).astype(bf16)
    zqkv = jnp.dot(h, wqkv_ref[...], preferred_element_type=f32)
    q_ref[...] = (zqkv[:, :FOX_WIDTH] * (HEAD_DIM ** -0.5 * LOG2E)).astype(bf16)
    k = zqkv[:, FOX_WIDTH:2 * FOX_WIDTH]
    v = zqkv[:, 2 * FOX_WIDTH:]
    k_ref[...] = k
    v_ref[...] = v
    kb_ref[...] = k.astype(bf16)
    vb_ref[...] = v.astype(bf16)
    zf = jnp.dot(h, wf_ref[...], preferred_element_type=f32) + bf_ref[...]
    lf_ref[...] = jax.nn.log_sigmoid(zf)[:, :FOX_HEADS]
    zugp = jnp.dot(h, wugp_ref[...], preferred_element_type=f32)
    u_ref[...] = _gelu(zugp[:, :GMLP_WIDTH])
    gv = _gelu(zugp[:, GMLP_WIDTH:2 * GMLP_WIDTH])
    mu = jnp.mean(gv, axis=-1, keepdims=True)
    gc = gv - mu
    vn_ref[...] = gc * lax.rsqrt(jnp.mean(gc * gc, axis=-1, keepdims=True) + EPS) * lng_ref[...] + lnb_ref[...]
    zp_ref[...] = zugp[:, 2 * GMLP_WIDTH:]
    zg = jnp.dot(h, wg_ref[...], preferred_element_type=f32)
    gate_ref[...] = 1.0 / (1.0 + jnp.exp(-zg))


def _inproj(x, w, tn):
    n, d = x.shape
    row = lambda width: pl.BlockSpec((tn, width), lambda i: (i, 0))
    outs = [
        (FOX_WIDTH, bf16), (FOX_WIDTH, f32), (FOX_WIDTH, f32), (FOX_WIDTH, bf16), (FOX_WIDTH, bf16),
        (FOX_HEADS, f32), (GMLP_WIDTH, f32), (GMLP_WIDTH, f32), (POOL_WIDTH, f32), (N_BRANCH * d, f32),
    ]
    weights = [w["g1"], w["wqkv"], w["wf"], w["bf"], w["wugp"], w["lng"], w["lnb"], w["wg"]]
    return pl.pallas_call(
        _inproj_body,
        grid=(n // tn,),
        in_specs=[row(d)] + [_const_spec(a.shape) for a in weights],
        out_specs=[row(width) for width, _ in outs],
        out_shape=[jax.ShapeDtypeStruct((n, width), dt) for width, dt in outs],
        compiler_params=_params("parallel"),
        name="inproj",
    )(x, *weights)


def _cumsum_body(x_ref, c_ref, hi_ref, mid_ref, lo_ref):
    x = x_ref[0]
    n = x.shape[-1]
    lane = lax.broadcasted_iota(jnp.int32, x.shape, 1)
    shift = 1
    while shift < n:
        x = x + jnp.where(lane >= shift, pltpu.roll(x, shift, 1), 0.0)
        shift *= 2
    c = x * LOG2E
    c_ref[0] = c
    hi = c.astype(bf16)
    r1 = c - hi.astype(f32)
    mid = r1.astype(bf16)
    hi_ref[0] = hi
    mid_ref[0] = mid
    lo_ref[0] = (r1 - mid.astype(f32)).astype(bf16)


def _cumsum_lanes(x):
    b, r, n = x.shape
    spec = pl.BlockSpec((1, r, n), lambda i: (i, 0, 0))
    return pl.pallas_call(
        _cumsum_body, grid=(b,), in_specs=[spec], out_specs=[spec] * 4,
        out_shape=[jax.ShapeDtypeStruct(x.shape, f32)] + [jax.ShapeDtypeStruct(x.shape, bf16)] * 3,
        compiler_params=_params("parallel"), name="cumsum",
    )(x)


def _attn_body(q_ref, k_ref, v_ref, cq_ref, o_ref, *, tq, tk, q_off):
    qi = pl.program_id(2)
    lane = lax.broadcasted_iota(jnp.int32, (tq, LANES), 1)
    row0 = q_off + qi * tq
    n_full = (row0 + 1) // tk
    n_all = (row0 + tq + tk - 1) // tk
    qs = [q_ref[0, :, hh * LANES:(hh + 1) * LANES] for hh in range(2)]
    cqs = [jnp.broadcast_to(cq_ref[0, 0, :, hh:hh + 1], (tq, LANES)) for hh in range(2)]
    n_col = tk // LANES

    def step(t, carry, masked):
        start = pl.multiple_of(t * tk, tk)
        vs = v_ref[0, pl.ds(start, tk), :]
        if masked:
            kpos = start + lax.broadcasted_iota(jnp.int32, (tq, tk), 1)
            qpos = row0 + lax.broadcasted_iota(jnp.int32, (tq, tk), 0)
            allowed = kpos <= qpos
        out = []
        for hh in range(2):
            m, l, acc = carry[hh]
            ks = k_ref[0, pl.ds(start, tk), hh * LANES:(hh + 1) * LANES]
            s = lax.dot_general(qs[hh], ks, (((1,), (1,)), ((), ())), preferred_element_type=f32)
            if masked:
                s = jnp.where(allowed, s, NEG_INF)
            cols = [s[:, j * LANES:(j + 1) * LANES] for j in range(n_col)]
            smax = functools.reduce(jnp.maximum, cols)
            m_new = jnp.maximum(m, jnp.max(smax, axis=1, keepdims=True) + cqs[hh])
            alpha = jnp.exp2(m - m_new)
            r = cqs[hh] - m_new
            ps = [jnp.exp2(col + r) for col in cols]
            l = alpha * l + functools.reduce(jnp.add, ps)
            p = jnp.concatenate([x.astype(bf16) for x in ps], axis=1)
            acc = alpha * acc + jnp.dot(p, vs, preferred_element_type=f32)
            out.append((m_new, l, acc))
        return tuple(out)

    one = (jnp.full((tq, LANES), NEG_INF, f32), jnp.zeros((tq, LANES), f32), jnp.zeros((tq, LANES), f32))
    carry = lax.fori_loop(0, n_full, functools.partial(step, masked=False), (one, one))
    carry = lax.fori_loop(n_full, n_all, functools.partial(step, masked=True), carry)
    accs = [acc / jnp.sum(l, axis=1, keepdims=True) for _, l, acc in carry]
    o_ref[0] = jnp.where(lane < HEAD_DIM, accs[0], accs[1]).astype(o_ref.dtype)


def _attention(q, k, v, cq, *, tq, tk, q_off):
    b, lq, _ = q.shape
    lk = k.shape[1]
    hp = FOX_HEADS // 2
    return pl.pallas_call(
        functools.partial(_attn_body, tq=tq, tk=tk, q_off=q_off),
        grid=(b, hp, lq // tq),
        in_specs=[
            pl.BlockSpec((1, tq, 2 * LANES), lambda b_, h_, i: (b_, i, h_)),
            pl.BlockSpec((1, lk, 2 * LANES), lambda b_, h_, i: (b_, 0, h_)),
            pl.BlockSpec((1, lk, LANES), lambda b_, h_, i: (b_, 0, h_)),
            pl.BlockSpec((1, 1, tq, 2), lambda b_, h_, i: (b_, h_, i, 0)),
        ],
        out_specs=pl.BlockSpec((1, tq, LANES), lambda b_, h_, i: (b_, i, h_)),
        out_shape=jax.ShapeDtypeStruct((b, lq, FOX_WIDTH), bf16),
        compiler_params=_params("parallel", "parallel", "arbitrary"),
        name="fox_attention",
    )(q, k, v, cq)


def _merge_body(x_ref, oa_ref, u_ref, vn_ref, zp_ref, prev_ref, gate_ref, ws_ref, bs_ref, wpool_ref, pscale_ref,
                wbr_ref, wout_ref, o_ref, *, tn, lc, pos0):
    j = pl.program_id(1)
    d = x_ref.shape[-1]
    t_io = lax.broadcasted_iota(jnp.int32, (lc, lc), 0)
    s_io = lax.broadcasted_iota(jnp.int32, (lc, lc), 1)
    causal = (s_io // SGU_BLOCK) <= (t_io // SGU_BLOCK)
    grp = lax.broadcasted_iota(jnp.int32, (lc, GMLP_WIDTH), 1) // (GMLP_WIDTH // GMLP_GROUPS)
    wmask = [jnp.where(causal, ws_ref[g], 0.0).astype(bf16) for g in range(GMLP_GROUPS)]
    ob = []
    for c in range(tn // lc):
        vc = vn_ref[0, c * lc:(c + 1) * lc, :].astype(bf16)
        s = bs_ref[...]
        for g in range(GMLP_GROUPS):
            sg = jnp.dot(wmask[g], vc, preferred_element_type=f32)
            s = s + jnp.where(grp == g, sg, 0.0)
        ob.append(u_ref[0, c * lc:(c + 1) * lc, :] * s)
    o_b = ob[0] if len(ob) == 1 else jnp.concatenate(ob, axis=0)
    zp = zp_ref[0]
    ext = jnp.concatenate([prev_ref[0, 0], zp], axis=0)
    s2 = ext + pltpu.roll(ext, 1, 0)
    s4 = s2 + pltpu.roll(s2, 2, 0)
    s8 = s4 + pltpu.roll(s4, 4, 0)
    s16 = s8 + pltpu.roll(s8, 8, 0)
    pgrp = lax.broadcasted_iota(jnp.int32, (tn, POOL_WIDTH), 1) // (POOL_WIDTH // POOL_GROUPS)
    pos = pos0 + j * tn + lax.broadcasted_iota(jnp.int32, (tn, POOL_WIDTH), 0)
    win = jnp.where(pgrp == 0, s2[HALO:], jnp.where(pgrp == 1, s4[HALO:], jnp.where(pgrp == 2, s8[HALO:], s16[HALO:])))
    width = jnp.where(pgrp == 0, 2, jnp.where(pgrp == 1, 4, jnp.where(pgrp == 2, 8, 16)))
    cnt = jnp.minimum(pos + 1, width).astype(f32)
    dlt = win / cnt - zp
    o_c = jnp.dot(dlt.astype(bf16), wpool_ref[...], preferred_element_type=f32) * pscale_ref[...]
    ya = jnp.dot(oa_ref[0], wbr_ref[:FOX_WIDTH, :], preferred_element_type=f32)
    yb = jnp.dot(o_b.astype(bf16), wbr_ref[FOX_WIDTH:FOX_WIDTH + GMLP_WIDTH, :], preferred_element_type=f32)
    yc = jnp.dot(o_c.astype(bf16), wbr_ref[FOX_WIDTH + GMLP_WIDTH:, :], preferred_element_type=f32)
    merged = gate_ref[0, :, :d] * ya + gate_ref[0, :, d:2 * d] * yb + gate_ref[0, :, 2 * d:] * yc
    o_ref[0] = x_ref[0] + jnp.dot(merged.astype(bf16), wout_ref[...], preferred_element_type=f32)


def _merge(x, oa, u, vn, zp, prev, gate, w, *, tn, lc, pos0):
    nseq, L, d = x.shape
    seq = lambda width: pl.BlockSpec((1, tn, width), lambda b_, j: (b_, j, 0))
    weights = [w["ws"], w["bs"], w["wpool"], w["pscale"], w["wbr"], w["wout"]]
    return pl.pallas_call(
        functools.partial(_merge_body, tn=tn, lc=lc, pos0=pos0),
        grid=(nseq, L // tn),
        in_specs=[seq(d), seq(FOX_WIDTH), seq(GMLP_WIDTH), seq(GMLP_WIDTH), seq(POOL_WIDTH),
                  pl.BlockSpec((1, 1, HALO, POOL_WIDTH), lambda b_, j: (b_, j, 0, 0)), seq(N_BRANCH * d)]
        + [_const_spec(a.shape) for a in weights],
        out_specs=seq(d),
        out_shape=jax.ShapeDtypeStruct(x.shape, f32),
        compiler_params=_params("parallel", "parallel"),
        name="merge",
    )(x, oa, u, vn, zp, prev, gate, *weights)


def _top_values(block, k):
    tops = []
    for _ in range(k):
        m = jnp.max(block, axis=0, keepdims=True)
        tops.append(m)
        block = jnp.where(block == m, NEG_INF, block)
    return tops


def _route_body(x_ref, g2_ref, wqt_ref, keys_ref, ht_ref, sc_ref, st_ref, top_ref, *, tn):
    h = _rms(x_ref[...], g2_ref[...]).astype(bf16)
    ht = h.T
    ht_ref[...] = ht
    qt = jnp.dot(wqt_ref[...], ht, preferred_element_type=f32).astype(bf16)
    n_sets = 2 * PEER_HEADS
    for r in range(n_sets):
        sc_ref[r * N_KEYS:(r + 1) * N_KEYS, :] = jnp.dot(
            keys_ref[r], qt[r * PEER_HALF:(r + 1) * PEER_HALF, :], preferred_element_type=f32)

    kk = PEER_TOPK + 1

    def half_top(r, _):
        base = pl.multiple_of(r * N_KEYS, N_KEYS)
        for tb in range(tn // LANES):
            blk = sc_ref[pl.ds(base, N_KEYS), tb * LANES:(tb + 1) * LANES]
            tops = _top_values(blk, kk)
            top_ref[r, :, tb * LANES:(tb + 1) * LANES] = jnp.concatenate(
                tops + [tops[-1]] * (TOP_ROWS - kk), axis=0)
        return 0

    lax.fori_loop(0, n_sets, half_top, 0)

    def head_stats(hd, _):
        for tb in range(tn // LANES):
            sl = slice(tb * LANES, (tb + 1) * LANES)
            a = top_ref[2 * hd, :, sl]
            b = top_ref[2 * hd + 1, :, sl]
            rows8 = lax.broadcasted_iota(jnp.int32, (SUBLANES, LANES), 0)
            never = jnp.full((SUBLANES, LANES), NEG_INF, f32)
            blocks = [(a[0:1], b[0:SUBLANES]), (a[0:1], b[SUBLANES:2 * SUBLANES])]
            for p in range(1, SUBLANES):
                blocks.append((a[p:p + 1], jnp.where(rows8 < kk // (p + 1), b[0:SUBLANES], never)))
            blocks.append((a[SUBLANES:2 * SUBLANES], b[0:1]))
            blocks.append((jnp.where(rows8 == 0, a[PEER_TOPK:PEER_TOPK + 1], a[0:1]),
                           jnp.where(rows8 == 0, b[0:1], jnp.where(rows8 == 1, b[PEER_TOPK:PEER_TOPK + 1], never))))
            cand = jnp.concatenate([ab + bb for ab, bb in blocks], axis=0)
            tops = _top_values(cand, kk)
            tau = 0.5 * (tops[-2] + tops[-1])
            z = jnp.zeros((1, LANES), f32)
            for ab, bb in blocks:
                kept = bb >= tau - ab
                z = z + jnp.sum(jnp.where(kept, jnp.exp((ab - a[0:1]) + (bb - b[0:1])), 0.0), axis=0, keepdims=True)
            st_ref[hd, :, sl] = jnp.concatenate(
                [a[0:1], b[0:1], tau, 0.5 / z, jnp.zeros((SUBLANES - 4, LANES), f32)], axis=0)
        return 0

    lax.fori_loop(0, PEER_HEADS, head_stats, 0)


def _route(x, w, tn):
    n, d = x.shape
    n_sets = 2 * PEER_HEADS
    return pl.pallas_call(
        functools.partial(_route_body, tn=tn),
        grid=(n // tn,),
        in_specs=[pl.BlockSpec((tn, d), lambda i: (i, 0)), _const_spec(w["g2"].shape), _const_spec(w["wqt"].shape),
                  _const_spec(w["keys"].shape)],
        out_specs=[pl.BlockSpec((d, tn), lambda i: (0, i)), pl.BlockSpec((n_sets * N_KEYS, tn), lambda i: (0, i)),
                   pl.BlockSpec((PEER_HEADS, 8, tn), lambda i: (0, 0, i))],
        out_shape=[jax.ShapeDtypeStruct((d, n), bf16), jax.ShapeDtypeStruct((n_sets * N_KEYS, n), f32),
                   jax.ShapeDtypeStruct((PEER_HEADS, 8, n), f32)],
        scratch_shapes=[pltpu.VMEM((n_sets, TOP_ROWS, tn), f32)],
        compiler_params=_params("parallel"),
        name="peer_route",
    )(x, w["g2"], w["wqt"], w["keys"])


def _n_halves(tn):
    return 2 if tn % (2 * LANES) == 0 else 1


def _experts_body(x_ref, ht_ref, sc_ref, st_ref, pu_ref, pvp_ref, pvc_ref, o_ref, q_ref, act0_ref, act1_ref, acc_ref,
                  *, tn, ec):
    c = pl.program_id(1)
    n_sub = ec // N_KEYS
    n_half = _n_halves(tn)
    half = tn // n_half
    acts = (act0_ref, act1_ref)

    @pl.when(c == 0)
    def _():
        for hd in range(PEER_HEADS):
            s2 = sc_ref[(2 * hd + 1) * N_KEYS:(2 * hd + 2) * N_KEYS, :]
            q_ref[hd] = jnp.exp(s2 - st_ref[hd, 1:2, :]) * st_ref[hd, 3:4, :]
        acc_ref[...] = jnp.zeros_like(acc_ref)
        acts[n_half - 1][...] = jnp.zeros_like(acts[n_half - 1])

    i0 = pl.multiple_of(c * n_sub, n_sub)

    def hidden(hf):
        return jnp.dot(pu_ref[...], ht_ref[:, hf * half:(hf + 1) * half], preferred_element_type=f32)

    def values(pv_ref, hf):
        acc_ref[:, hf * half:(hf + 1) * half] += jnp.dot(pv_ref[...], acts[hf][...], preferred_element_type=f32)

    def weigh(hf, a):
        for tb in range(half // LANES):
            lanes = slice(hf * half + tb * LANES, hf * half + (tb + 1) * LANES)
            thr_rows, p_rows = [], []
            for hd in range(PEER_HEADS):
                s1 = sc_ref[pl.ds(2 * hd * N_KEYS + i0, n_sub), lanes]
                thr_rows.append(st_ref[hd, 2:3, lanes] - s1)
                p_rows.append(jnp.exp(s1 - st_ref[hd, 0:1, lanes]))
            for ii in range(n_sub):
                gsum = jnp.zeros((N_KEYS, LANES), f32)
                for hd in range(PEER_HEADS):
                    s2 = sc_ref[(2 * hd + 1) * N_KEYS:(2 * hd + 2) * N_KEYS, lanes]
                    kept = s2 >= thr_rows[hd][ii:ii + 1]
                    gsum = gsum + jnp.where(kept, q_ref[hd, :, lanes], 0.0) * p_rows[hd][ii:ii + 1]
                aa = a[ii * N_KEYS:(ii + 1) * N_KEYS, tb * LANES:(tb + 1) * LANES]
                acts[hf][ii * N_KEYS:(ii + 1) * N_KEYS, tb * LANES:(tb + 1) * LANES] = (
                    (aa + aa * lax.erf(aa * 0.7071067811865476)) * gsum).astype(bf16)

    a0 = hidden(0)
    values(pvp_ref, n_half - 1)
    weigh(0, a0)
    if n_half == 2:
        a1 = hidden(1)
        values(pvc_ref, 0)
        weigh(1, a1)

    @pl.when(c == pl.num_programs(1) - 1)
    def _():
        values(pvc_ref, n_half - 1)
        o_ref[...] = x_ref[...] + acc_ref[...].T


def _experts(x, ht, sc, st, w, *, tn, ec):
    n, d = x.shape
    n_exp = w["pu"].shape[0]
    n_sets = 2 * PEER_HEADS
    return pl.pallas_call(
        functools.partial(_experts_body, tn=tn, ec=ec),
        grid=(n // tn, n_exp // ec),
        in_specs=[
            pl.BlockSpec((tn, d), lambda t, c: (t, 0)),
            pl.BlockSpec((d, tn), lambda t, c: (0, t)),
            pl.BlockSpec((n_sets * N_KEYS, tn), lambda t, c: (0, t)),
            pl.BlockSpec((PEER_HEADS, SUBLANES, tn), lambda t, c: (0, 0, t)),
            pl.BlockSpec((ec, d), lambda t, c: (c, 0)),
            pl.BlockSpec((d, ec), lambda t, c: (0, jnp.maximum(c - 1, 0))),
            pl.BlockSpec((d, ec), lambda t, c: (0, c)),
        ],
        out_specs=pl.BlockSpec((tn, d), lambda t, c: (t, 0)),
        out_shape=jax.ShapeDtypeStruct((n, d), f32),
        scratch_shapes=[
            pltpu.VMEM((PEER_HEADS, N_KEYS, tn), f32),
            pltpu.VMEM((ec, tn // _n_halves(tn)), bf16), pltpu.VMEM((ec, tn // _n_halves(tn)), bf16),
            pltpu.VMEM((d, tn), f32),
        ],
        compiler_params=_params("parallel", "arbitrary"),
        name="peer_experts",
    )(x, ht, sc, st, w["pu"], w["pvt"], w["pvt"])


def _final_body(x_ref, g_ref, o_ref):
    o_ref[...] = _rms(x_ref[...], g_ref[...])


def _final_norm(x, g, tn):
    n, d = x.shape
    spec = pl.BlockSpec((tn, d), lambda i: (i, 0))
    return pl.pallas_call(
        _final_body, grid=(n // tn,), in_specs=[spec, _const_spec(g.shape)], out_specs=spec,
        out_shape=jax.ShapeDtypeStruct(x.shape, f32), compiler_params=_params("parallel"), name="final_norm",
    )(x, g)


def _layer_weights(l, norm1_g, w_in, b_f, ln_g, ln_b, w_s, b_s, w_pool, pool_scale, w_br, w_out, norm2_g, peer_wq,
                   peer_keys, peer_u, peer_v, lc):
    d = w_in.shape[1]
    wi = w_in[l]
    o_f = 3 * FOX_WIDTH
    o_u = o_f + FOX_HEADS
    o_g = o_u + 2 * GMLP_WIDTH + POOL_WIDTH
    wf = jnp.zeros((d, LANES), f32).at[:, :FOX_HEADS].set(wi[:, o_f:o_u])
    bf = jnp.zeros((1, LANES), f32).at[0, :FOX_HEADS].set(b_f[l])
    gdim = GMLP_WIDTH // GMLP_GROUPS
    pdim = POOL_WIDTH // POOL_GROUPS
    wpool = jnp.zeros((POOL_WIDTH, POOL_WIDTH), f32)
    for g in range(POOL_GROUPS):
        wpool = wpool.at[g * pdim:(g + 1) * pdim, g * pdim:(g + 1) * pdim].set(w_pool[l, g])
    return {
        "g1": norm1_g[l][None], "wqkv": wi[:, :o_f].astype(bf16), "wf": wf.astype(bf16), "bf": bf,
        "wugp": wi[:, o_u:o_g].astype(bf16), "lng": ln_g[l][None], "lnb": ln_b[l][None], "wg": wi[:, o_g:].astype(bf16),
        "ws": w_s[l][:, :lc, :lc], "bs": jnp.repeat(jnp.transpose(b_s[l][:, :lc]), gdim, axis=1),
        "wpool": wpool.astype(bf16), "pscale": pool_scale[l][None],
        "wbr": w_br[l].astype(bf16), "wout": w_out[l].astype(bf16),
        "g2": norm2_g[l][None], "wqt": jnp.transpose(peer_wq[l]).astype(bf16),
        "keys": peer_keys[l].reshape(2 * PEER_HEADS, N_KEYS, PEER_HALF).astype(bf16),
        "pu": peer_u[l].astype(bf16), "pvt": jnp.transpose(peer_v[l]).astype(bf16),
    }


def _pick(n, prefs):
    for t in prefs:
        if n % t == 0:
            return t
    return n


def _layer(x, w, *, k_hist, v_hist, logf_hist, pool_hist, lc, tiles):
    nseq, L, d = x.shape
    n = nseq * L
    q, k, v, kb, vb, logf, u, vn, zp, gate = _inproj(x.reshape(n, d), w, tiles["inproj"])
    r3 = lambda a: a.reshape(nseq, L, a.shape[-1])
    q, kb, vb, logf3, u, vn, zp, gate = map(r3, (q, kb, vb, logf, u, vn, zp, gate))
    past = 0 if k_hist is None else k_hist.shape[1]
    if past:
        kb = jnp.concatenate([k_hist.reshape(nseq, past, FOX_WIDTH).astype(bf16), kb], axis=1)
        vb = jnp.concatenate([v_hist.reshape(nseq, past, FOX_WIDTH).astype(bf16), vb], axis=1)
        lf_all = jnp.concatenate([logf_hist.astype(f32), logf3], axis=1)
    else:
        lf_all = logf3
    lk = past + L
    tk = tiles["tk"]
    lk_pad = -(-lk // tk) * tk
    if lk_pad != lk:
        padk = ((0, 0), (0, lk_pad - lk), (0, 0))
        kb, vb, lf_all = jnp.pad(kb, padk), jnp.pad(vb, padk), jnp.pad(lf_all, padk)
    hp = FOX_HEADS // 2
    c_t, c_hi, c_mid, c_lo = _cumsum_lanes(jnp.transpose(lf_all, (0, 2, 1)))
    cq = jnp.transpose(c_t.reshape(nseq, hp, 2, lk_pad)[:, :, :, past:past + L], (0, 1, 3, 2))
    fill = LANES - HEAD_DIM - C_PIECES
    pieces = jnp.transpose(jnp.stack([c_hi, c_mid, c_lo], axis=-1), (0, 2, 1, 3))
    k_aug = jnp.concatenate([kb.reshape(nseq, lk_pad, FOX_HEADS, HEAD_DIM), pieces,
                             jnp.zeros((nseq, lk_pad, FOX_HEADS, fill), bf16)], axis=-1)
    q_aug = jnp.concatenate([q.reshape(nseq, L, FOX_HEADS, HEAD_DIM), jnp.full((nseq, L, FOX_HEADS, C_PIECES), -1, bf16),
                             jnp.zeros((nseq, L, FOX_HEADS, fill), bf16)], axis=-1)
    o_a = _attention(q_aug.reshape(nseq, L, FOX_HEADS * LANES), k_aug.reshape(nseq, lk_pad, FOX_HEADS * LANES), vb, cq,
                     tq=tiles["tq"], tk=tk, q_off=past)
    tn = tiles["merge"]
    nt = L // tn
    first = jnp.zeros((nseq, HALO, POOL_WIDTH), f32)
    if pool_hist is not None:
        first = first.at[:, HALO - POOL_HIST:].set(pool_hist.astype(f32))
    tails = zp.reshape(nseq, nt, tn, POOL_WIDTH)[:, :nt - 1, tn - HALO:, :]
    prev = jnp.concatenate([first[:, None], tails], axis=1)
    x_mid = _merge(x, o_a, u, vn, zp, prev, gate, w, tn=tn, lc=lc, pos0=past)
    ht, sc, st = _route(x_mid.reshape(n, d), w, tiles["route"])
    x_new = _experts(x_mid.reshape(n, d), ht, sc, st, w, tn=tiles["experts"], ec=tiles["ec"])
    new_pool = zp[:, L - POOL_HIST:]
    return x_new.reshape(nseq, L, d), r3(k), r3(v), logf3, new_pool, vn


def kernel(x_prompt, x_sample, cache_fox_k, cache_fox_v, cache_fox_logf, state_pool, norm1_g, w_in, b_f, ln_g, ln_b,
           w_s, b_s, w_pool, pool_scale, w_br, w_out, norm2_g, peer_wq, peer_keys, peer_u, peer_v, final_g):
    depth = w_in.shape[0]
    bp, sp, d = x_prompt.shape
    bd, ls, _ = x_sample.shape
    assert sp % GMLP_CHUNK == 0 and ls <= GMLP_CHUNK and ls >= POOL_HIST and ls % 16 == 0
    params = (norm1_g, w_in, b_f, ln_g, ln_b, w_s, b_s, w_pool, pool_scale, w_br, w_out, norm2_g, peer_wq, peer_keys,
              peer_u, peer_v)
    tiles_p = {"inproj": _pick(bp * sp, (256,)), "tq": _pick(sp, (512, 256, 128)), "tk": _pick(sp, (512, 256, 128)),
               "merge": _pick(sp, (256, 128)), "route": _pick(bp * sp, (512, 256, 128)),
               "experts": _pick(bp * sp, (512, 256, 128)), "ec": EXPERT_CHUNK}
    ns = bd * ls
    tiles_s = {"inproj": ns, "tq": ls, "tk": 128, "merge": ls, "route": ns, "experts": ns, "ec": EXPERT_CHUNK}
    xp, xs = x_prompt, x_sample
    outs_p, outs_s = [], []
    for l in range(depth):
        wp = _layer_weights(l, *params, lc=GMLP_CHUNK)
        xp, k, v, lf, pool, _ = _layer(xp, wp, k_hist=None, v_hist=None, logf_hist=None, pool_hist=None,
                                       lc=GMLP_CHUNK, tiles=tiles_p)
        outs_p.append((k.reshape(bp, sp, FOX_HEADS, HEAD_DIM), v.reshape(bp, sp, FOX_HEADS, HEAD_DIM), lf, pool))
        wsm = dict(wp)
        wsm["ws"] = w_s[l][:, :ls, :ls]
        wsm["bs"] = jnp.repeat(jnp.transpose(b_s[l][:, :ls]), GMLP_WIDTH // GMLP_GROUPS, axis=1)
        xs, k, v, lf, pool, vn = _layer(xs, wsm, k_hist=cache_fox_k[l], v_hist=cache_fox_v[l],
                                        logf_hist=cache_fox_logf[l], pool_hist=state_pool[l], lc=ls, tiles=tiles_s)
        outs_s.append((k.reshape(bd, ls, FOX_HEADS, HEAD_DIM), v.reshape(bd, ls, FOX_HEADS, HEAD_DIM), lf, pool, vn))
    g = final_g[None]
    y_prompt = _final_norm(xp.reshape(bp * sp, d), g, _pick(bp * sp, (512, 256, 128))).reshape(bp, sp, d)
    y_sample = _final_norm(xs.reshape(ns, d), g, ns).reshape(bd, ls, d)
    stack = lambda outs, i: jnp.stack([o[i] for o in outs])
    return (y_prompt, y_sample, stack(outs_p, 0), stack(outs_p, 1), stack(outs_p, 2), stack(outs_p, 3),
            stack(outs_s, 0), stack(outs_s, 1), stack(outs_s, 2), stack(outs_s, 3), stack(outs_s, 4))
```

```python
import functools

import jax
import jax.numpy as jnp
from jax import lax
from jax.experimental import pallas as pl
from jax.experimental.pallas import tpu as pltpu

f32 = jnp.float32
bf16 = jnp.bfloat16

EPS = 1e-6
FOX_HEADS = 8
HEAD_DIM = 64
FOX_WIDTH = FOX_HEADS * HEAD_DIM
GMLP_GROUPS = 4
GMLP_WIDTH = 256
GMLP_CHUNK = 128
SGU_BLOCK = 64
POOL_GROUPS = 4
POOL_WIDTH = 256
POOL_WINDOWS = (2, 4, 8, 16)
POOL_HIST = 15
HALO = 16
N_BRANCH = 3
PEER_HEADS = 8
N_KEYS = 128
PEER_TOPK = 16
PEER_HALF = 128

LANES = 128
SUBLANES = 8
EXPERT_CHUNK = SUBLANES * N_KEYS
TOP_ROWS = 3 * SUBLANES
VMEM_LIMIT = 56 * 1024 * 1024
NEG_INF = float("-inf")
LOG2E = 1.4426950408889634
C_PIECES = 3
ATTN_HEADS_PER_STEP = 4
ATTN_WIDTH = FOX_HEADS * LANES


def _params(*sem):
    return pltpu.CompilerParams(dimension_semantics=sem, vmem_limit_bytes=VMEM_LIMIT)


def _const_spec(shape):
    nd = len(shape)
    return pl.BlockSpec(shape, lambda *_: (0,) * nd)


def _rms(x, g):
    return x * lax.rsqrt(jnp.mean(x * x, axis=-1, keepdims=True) + EPS) * g


def _gelu(x):
    return 0.5 * x * (1.0 + lax.erf(x * 0.7071067811865476))


def _inproj_body(x_ref, g1_ref, wqk_ref, qc_ref, wkv_ref, wf_ref, bf_ref, wugp_ref, lng_ref, lnb_ref, wg_ref,
                 q_ref, k_ref, v_ref, kb_ref, vb_ref, lf_ref, u_ref, vn_ref, zp_ref, gate_ref):
    h = _rms(x_ref[...], g1_ref[...]).astype(bf16)
    zqk = jnp.dot(h, wqk_ref[...], preferred_element_type=f32)
    q_ref[...] = (zqk[:, :ATTN_WIDTH] * (HEAD_DIM ** -0.5 * LOG2E) + qc_ref[...]).astype(bf16)
    kb_ref[...] = zqk[:, ATTN_WIDTH:].astype(bf16)
    zkv = jnp.dot(h, wkv_ref[...], preferred_element_type=f32)
    v = zkv[:, FOX_WIDTH:]
    k_ref[...] = zkv[:, :FOX_WIDTH]
    v_ref[...] = v
    vb_ref[...] = v.astype(bf16)
    zf = jnp.dot(h, wf_ref[...], preferred_element_type=f32) + bf_ref[...]
    lf_ref[...] = jax.nn.log_sigmoid(zf)[:, :FOX_HEADS]
    zugp = jnp.dot(h, wugp_ref[...], preferred_element_type=f32)
    u_ref[...] = _gelu(zugp[:, :GMLP_WIDTH])
    gv = _gelu(zugp[:, GMLP_WIDTH:2 * GMLP_WIDTH])
    mu = jnp.mean(gv, axis=-1, keepdims=True)
    gc = gv - mu
    vn_ref[...] = gc * lax.rsqrt(jnp.mean(gc * gc, axis=-1, keepdims=True) + EPS) * lng_ref[...] + lnb_ref[...]
    zp_ref[...] = zugp[:, 2 * GMLP_WIDTH:]
    zg = jnp.dot(h, wg_ref[...], preferred_element_type=f32)
    gate_ref[...] = 1.0 / (1.0 + jnp.exp(-zg))


def _inproj(x, w, tn):
    n, d = x.shape
    row = lambda width: pl.BlockSpec((tn, width), lambda i: (i, 0))
    outs = [
        (ATTN_WIDTH, bf16), (FOX_WIDTH, f32), (FOX_WIDTH, f32), (ATTN_WIDTH, bf16), (FOX_WIDTH, bf16),
        (FOX_HEADS, f32), (GMLP_WIDTH, f32), (GMLP_WIDTH, f32), (POOL_WIDTH, f32), (N_BRANCH * d, f32),
    ]
    weights = [w["g1"], w["wqk"], w["qc"], w["wkv"], w["wf"], w["bf"], w["wugp"], w["lng"], w["lnb"], w["wg"]]
    return pl.pallas_call(
        _inproj_body,
        grid=(n // tn,),
        in_specs=[row(d)] + [_const_spec(a.shape) for a in weights],
        out_specs=[row(width) for width, _ in outs],
        out_shape=[jax.ShapeDtypeStruct((n, width), dt) for width, dt in outs],
        compiler_params=_params("parallel"),
        name="inproj",
    )(x, *weights)


def _cumsum_body(x_ref, c_ref, hi_ref, mid_ref, lo_ref):
    x = x_ref[0]
    n = x.shape[-1]
    lane = lax.broadcasted_iota(jnp.int32, x.shape, 1)
    shift = 1
    while shift < n:
        x = x + jnp.where(lane >= shift, pltpu.roll(x, shift, 1), 0.0)
        shift *= 2
    c = x * LOG2E
    c_ref[0] = c
    hi = c.astype(bf16)
    r1 = c - hi.astype(f32)
    mid = r1.astype(bf16)
    hi_ref[0] = hi
    mid_ref[0] = mid
    lo_ref[0] = (r1 - mid.astype(f32)).astype(bf16)


def _cumsum_lanes(x):
    b, r, n = x.shape
    spec = pl.BlockSpec((1, r, n), lambda i: (i, 0, 0))
    return pl.pallas_call(
        _cumsum_body, grid=(b,), in_specs=[spec], out_specs=[spec] * 4,
        out_shape=[jax.ShapeDtypeStruct(x.shape, f32)] + [jax.ShapeDtypeStruct(x.shape, bf16)] * 3,
        compiler_params=_params("parallel"), name="cumsum",
    )(x)


def _attn_body(q_ref, k_ref, v_ref, cq_ref, o_ref, *, tq, tk, q_off):
    qi = pl.program_id(2)
    lane = lax.broadcasted_iota(jnp.int32, (tq, LANES), 1)
    row0 = q_off + qi * tq
    n_full = (row0 + 1) // tk
    n_all = (row0 + tq + tk - 1) // tk
    nh = ATTN_HEADS_PER_STEP
    qs = [q_ref[0, :, hh * LANES:(hh + 1) * LANES] for hh in range(nh)]
    cqs = [jnp.broadcast_to(cq_ref[0, 0, :, hh:hh + 1], (tq, LANES)) for hh in range(nh)]
    n_col = tk // LANES

    def step(t, carry, masked):
        start = pl.multiple_of(t * tk, tk)
        if masked:
            kpos = start + lax.broadcasted_iota(jnp.int32, (tq, tk), 1)
            qpos = row0 + lax.broadcasted_iota(jnp.int32, (tq, tk), 0)
            allowed = kpos <= qpos
        out = []
        for hh in range(nh):
            m, l, acc = carry[hh]
            ks = k_ref[0, pl.ds(start, tk), hh * LANES:(hh + 1) * LANES]
            vs = v_ref[0, pl.ds(start, tk), (hh // 2) * LANES:(hh // 2 + 1) * LANES]
            s = lax.dot_general(qs[hh], ks, (((1,), (1,)), ((), ())), preferred_element_type=f32)
            if masked:
                s = jnp.where(allowed, s, NEG_INF)
            cols = [s[:, j * LANES:(j + 1) * LANES] for j in range(n_col)]
            smax = functools.reduce(jnp.maximum, cols)
            m_new = jnp.maximum(m, jnp.max(smax, axis=1, keepdims=True) + cqs[hh])
            alpha = jnp.exp2(m - m_new)
            r = cqs[hh] - m_new
            ps = [jnp.exp2(col + r) for col in cols]
            l = alpha * l + functools.reduce(jnp.add, ps)
            p = jnp.concatenate([x.astype(bf16) for x in ps], axis=1)
            acc = alpha * acc + jnp.dot(p, vs, preferred_element_type=f32)
            out.append((m_new, l, acc))
        return tuple(out)

    one = (jnp.full((tq, LANES), NEG_INF, f32), jnp.zeros((tq, LANES), f32), jnp.zeros((tq, LANES), f32))
    carry = lax.fori_loop(0, n_full, functools.partial(step, masked=False), (one,) * nh)
    carry = lax.fori_loop(n_full, n_all, functools.partial(step, masked=True), carry)
    accs = [acc / jnp.sum(l, axis=1, keepdims=True) for _, l, acc in carry]
    for pr in range(nh // 2):
        o_ref[0, :, pr * LANES:(pr + 1) * LANES] = jnp.where(
            lane < HEAD_DIM, accs[2 * pr], accs[2 * pr + 1]).astype(o_ref.dtype)


def _attention(q, k, v, cq, *, tq, tk, q_off):
    b, lq, _ = q.shape
    lk = k.shape[1]
    nh = ATTN_HEADS_PER_STEP
    return pl.pallas_call(
        functools.partial(_attn_body, tq=tq, tk=tk, q_off=q_off),
        grid=(b, FOX_HEADS // nh, lq // tq),
        in_specs=[
            pl.BlockSpec((1, tq, nh * LANES), lambda b_, h_, i: (b_, i, h_)),
            pl.BlockSpec((1, lk, nh * LANES), lambda b_, h_, i: (b_, 0, h_)),
            pl.BlockSpec((1, lk, nh * HEAD_DIM), lambda b_, h_, i: (b_, 0, h_)),
            pl.BlockSpec((1, 1, tq, nh), lambda b_, h_, i: (b_, h_, i, 0)),
        ],
        out_specs=pl.BlockSpec((1, tq, nh * HEAD_DIM), lambda b_, h_, i: (b_, i, h_)),
        out_shape=jax.ShapeDtypeStruct((b, lq, FOX_WIDTH), bf16),
        compiler_params=_params("parallel", "parallel", "arbitrary"),
        name="fox_attention",
    )(q, k, v, cq)


def _merge_body(x_ref, oa_ref, u_ref, vn_ref, zp_ref, prev_ref, gate_ref, ws_ref, bs_ref, wpool_ref, pscale_ref,
                wbr_ref, wout_ref, o_ref, *, tn, lc, pos0):
    j = pl.program_id(1)
    d = x_ref.shape[-1]
    t_io = lax.broadcasted_iota(jnp.int32, (lc, lc), 0)
    s_io = lax.broadcasted_iota(jnp.int32, (lc, lc), 1)
    causal = (s_io // SGU_BLOCK) <= (t_io // SGU_BLOCK)
    grp = lax.broadcasted_iota(jnp.int32, (lc, GMLP_WIDTH), 1) // (GMLP_WIDTH // GMLP_GROUPS)
    wmask = [jnp.where(causal, ws_ref[g], 0.0).astype(bf16) for g in range(GMLP_GROUPS)]
    ob = []
    for c in range(tn // lc):
        vc = vn_ref[0, c * lc:(c + 1) * lc, :].astype(bf16)
        s = bs_ref[...]
        for g in range(GMLP_GROUPS):
            sg = jnp.dot(wmask[g], vc, preferred_element_type=f32)
            s = s + jnp.where(grp == g, sg, 0.0)
        ob.append(u_ref[0, c * lc:(c + 1) * lc, :] * s)
    o_b = ob[0] if len(ob) == 1 else jnp.concatenate(ob, axis=0)
    zp = zp_ref[0]
    ext = jnp.concatenate([prev_ref[0, 0], zp], axis=0)
    s2 = ext + pltpu.roll(ext, 1, 0)
    s4 = s2 + pltpu.roll(s2, 2, 0)
    s8 = s4 + pltpu.roll(s4, 4, 0)
    s16 = s8 + pltpu.roll(s8, 8, 0)
    pgrp = lax.broadcasted_iota(jnp.int32, (tn, POOL_WIDTH), 1) // (POOL_WIDTH // POOL_GROUPS)
    pos = pos0 + j * tn + lax.broadcasted_iota(jnp.int32, (tn, POOL_WIDTH), 0)
    win = jnp.where(pgrp == 0, s2[HALO:], jnp.where(pgrp == 1, s4[HALO:], jnp.where(pgrp == 2, s8[HALO:], s16[HALO:])))
    width = jnp.where(pgrp == 0, 2, jnp.where(pgrp == 1, 4, jnp.where(pgrp == 2, 8, 16)))
    cnt = jnp.minimum(pos + 1, width).astype(f32)
    dlt = win / cnt - zp
    o_c = jnp.dot(dlt.astype(bf16), wpool_ref[...], preferred_element_type=f32) * pscale_ref[...]
    ya = jnp.dot(oa_ref[0], wbr_ref[:FOX_WIDTH, :], preferred_element_type=f32)
    yb = jnp.dot(o_b.astype(bf16), wbr_ref[FOX_WIDTH:FOX_WIDTH + GMLP_WIDTH, :], preferred_element_type=f32)
    yc = jnp.dot(o_c.astype(bf16), wbr_ref[FOX_WIDTH + GMLP_WIDTH:, :], preferred_element_type=f32)
    merged = gate_ref[0, :, :d] * ya + gate_ref[0, :, d:2 * d] * yb + gate_ref[0, :, 2 * d:] * yc
    o_ref[0] = x_ref[0] + jnp.dot(merged.astype(bf16), wout_ref[...], preferred_element_type=f32)


def _merge(x, oa, u, vn, zp, prev, gate, w, *, tn, lc, pos0):
    nseq, L, d = x.shape
    seq = lambda width: pl.BlockSpec((1, tn, width), lambda b_, j: (b_, j, 0))
    weights = [w["ws"], w["bs"], w["wpool"], w["pscale"], w["wbr"], w["wout"]]
    return pl.pallas_call(
        functools.partial(_merge_body, tn=tn, lc=lc, pos0=pos0),
        grid=(nseq, L // tn),
        in_specs=[seq(d), seq(FOX_WIDTH), seq(GMLP_WIDTH), seq(GMLP_WIDTH), seq(POOL_WIDTH),
                  pl.BlockSpec((1, 1, HALO, POOL_WIDTH), lambda b_, j: (b_, j, 0, 0)), seq(N_BRANCH * d)]
        + [_const_spec(a.shape) for a in weights],
        out_specs=seq(d),
        out_shape=jax.ShapeDtypeStruct(x.shape, f32),
        compiler_params=_params("parallel", "parallel"),
        name="merge",
    )(x, oa, u, vn, zp, prev, gate, *weights)


def _top_values(block, k):
    tops = []
    for _ in range(k):
        m = jnp.max(block, axis=0, keepdims=True)
        tops.append(m)
        block = jnp.where(block == m, NEG_INF, block)
    return tops


def _route_body(x_ref, g2_ref, wqt_ref, keys_ref, ht_ref, sc_ref, st_ref, top_ref, *, tn):
    h = _rms(x_ref[...], g2_ref[...]).astype(bf16)
    ht = h.T
    ht_ref[...] = ht
    qt = jnp.dot(wqt_ref[...], ht, preferred_element_type=f32).astype(bf16)
    n_sets = 2 * PEER_HEADS
    for r in range(n_sets):
        sc_ref[r * N_KEYS:(r + 1) * N_KEYS, :] = jnp.dot(
            keys_ref[r], qt[r * PEER_HALF:(r + 1) * PEER_HALF, :], preferred_element_type=f32)

    kk = PEER_TOPK + 1

    def half_top(r, _):
        base = pl.multiple_of(r * N_KEYS, N_KEYS)
        for tb in range(tn // LANES):
            blk = sc_ref[pl.ds(base, N_KEYS), tb * LANES:(tb + 1) * LANES]
            tops = _top_values(blk, kk)
            top_ref[r, :, tb * LANES:(tb + 1) * LANES] = jnp.concatenate(
                tops + [tops[-1]] * (TOP_ROWS - kk), axis=0)
        return 0

    lax.fori_loop(0, n_sets, half_top, 0)

    def head_stats(hd, _):
        for tb in range(tn // LANES):
            sl = slice(tb * LANES, (tb + 1) * LANES)
            a = top_ref[2 * hd, :, sl]
            b = top_ref[2 * hd + 1, :, sl]
            rows8 = lax.broadcasted_iota(jnp.int32, (SUBLANES, LANES), 0)
            never = jnp.full((SUBLANES, LANES), NEG_INF, f32)
            blocks = [(a[0:1], b[0:SUBLANES]), (a[0:1], b[SUBLANES:2 * SUBLANES])]
            for p in range(1, SUBLANES):
                blocks.append((a[p:p + 1], jnp.where(rows8 < kk // (p + 1), b[0:SUBLANES], never)))
            blocks.append((a[SUBLANES:2 * SUBLANES], b[0:1]))
            blocks.append((jnp.where(rows8 == 0, a[PEER_TOPK:PEER_TOPK + 1], a[0:1]),
                           jnp.where(rows8 == 0, b[0:1], jnp.where(rows8 == 1, b[PEER_TOPK:PEER_TOPK + 1], never))))
            cand = jnp.concatenate([ab + bb for ab, bb in blocks], axis=0)
            tops = _top_values(cand, kk)
            tau = 0.5 * (tops[-2] + tops[-1])
            z = jnp.zeros((1, LANES), f32)
            for ab, bb in blocks:
                kept = bb >= tau - ab
                z = z + jnp.sum(jnp.where(kept, jnp.exp((ab - a[0:1]) + (bb - b[0:1])), 0.0), axis=0, keepdims=True)
            st_ref[hd, :, sl] = jnp.concatenate(
                [a[0:1], b[0:1], tau, 0.5 / z, jnp.zeros((SUBLANES - 4, LANES), f32)], axis=0)
        return 0

    lax.fori_loop(0, PEER_HEADS, head_stats, 0)


def _route(x, w, tn):
    n, d = x.shape
    n_sets = 2 * PEER_HEADS
    return pl.pallas_call(
        functools.partial(_route_body, tn=tn),
        grid=(n // tn,),
        in_specs=[pl.BlockSpec((tn, d), lambda i: (i, 0)), _const_spec(w["g2"].shape), _const_spec(w["wqt"].shape),
                  _const_spec(w["keys"].shape)],
        out_specs=[pl.BlockSpec((d, tn), lambda i: (0, i)), pl.BlockSpec((n_sets * N_KEYS, tn), lambda i: (0, i)),
                   pl.BlockSpec((PEER_HEADS, SUBLANES, tn), lambda i: (0, 0, i))],
        out_shape=[jax.ShapeDtypeStruct((d, n), bf16), jax.ShapeDtypeStruct((n_sets * N_KEYS, n), f32),
                   jax.ShapeDtypeStruct((PEER_HEADS, SUBLANES, n), f32)],
        scratch_shapes=[pltpu.VMEM((n_sets, TOP_ROWS, tn), f32)],
        compiler_params=_params("parallel"),
        name="peer_route",
    )(x, w["g2"], w["wqt"], w["keys"])


def _n_halves(tn):
    return 2 if tn % (2 * LANES) == 0 else 1


def _experts_body(x_ref, ht_ref, sc_ref, st_ref, pu_ref, pvp_ref, pvc_ref, o_ref, q_ref, act0_ref, act1_ref, acc_ref,
                  *, tn, ec):
    c = pl.program_id(1)
    n_sub = ec // N_KEYS
    n_half = _n_halves(tn)
    half = tn // n_half
    acts = (act0_ref, act1_ref)

    @pl.when(c == 0)
    def _():
        for hd in range(PEER_HEADS):
            s2 = sc_ref[(2 * hd + 1) * N_KEYS:(2 * hd + 2) * N_KEYS, :]
            q_ref[hd] = jnp.exp(s2 - st_ref[hd, 1:2, :]) * st_ref[hd, 3:4, :]
        acc_ref[...] = jnp.zeros_like(acc_ref)
        acts[n_half - 1][...] = jnp.zeros_like(acts[n_half - 1])

    i0 = pl.multiple_of(c * n_sub, n_sub)

    def hidden(hf):
        return jnp.dot(pu_ref[...], ht_ref[:, hf * half:(hf + 1) * half], preferred_element_type=f32)

    def values(pv_ref, hf):
        acc_ref[:, hf * half:(hf + 1) * half] += jnp.dot(pv_ref[...], acts[hf][...], preferred_element_type=f32)

    def weigh(hf, a):
        for tb in range(half // LANES):
            lanes = slice(hf * half + tb * LANES, hf * half + (tb + 1) * LANES)
            thr_rows, p_rows = [], []
            for hd in range(PEER_HEADS):
                s1 = sc_ref[pl.ds(2 * hd * N_KEYS + i0, n_sub), lanes]
                thr_rows.append(st_ref[hd, 2:3, lanes] - s1)
                p_rows.append(jnp.exp(s1 - st_ref[hd, 0:1, lanes]))
            for ii in range(n_sub):
                gsum = jnp.zeros((N_KEYS, LANES), f32)
                for hd in range(PEER_HEADS):
                    s2 = sc_ref[(2 * hd + 1) * N_KEYS:(2 * hd + 2) * N_KEYS, lanes]
                    kept = s2 >= thr_rows[hd][ii:ii + 1]
                    gsum = gsum + jnp.where(kept, q_ref[hd, :, lanes], 0.0) * p_rows[hd][ii:ii + 1]
                aa = a[ii * N_KEYS:(ii + 1) * N_KEYS, tb * LANES:(tb + 1) * LANES]
                acts[hf][ii * N_KEYS:(ii + 1) * N_KEYS, tb * LANES:(tb + 1) * LANES] = (
                    (aa + aa * lax.erf(aa * 0.7071067811865476)) * gsum).astype(bf16)

    a0 = hidden(0)
    values(pvp_ref, n_half - 1)
    weigh(0, a0)
    if n_half == 2:
        a1 = hidden(1)
        values(pvc_ref, 0)
        weigh(1, a1)

    @pl.when(c == pl.num_programs(1) - 1)
    def _():
        values(pvc_ref, n_half - 1)
        o_ref[...] = x_ref[...] + acc_ref[...].T


def _experts(x, ht, sc, st, w, *, tn, ec):
    n, d = x.shape
    n_exp = w["pu"].shape[0]
    n_sets = 2 * PEER_HEADS
    return pl.pallas_call(
        functools.partial(_experts_body, tn=tn, ec=ec),
        grid=(n // tn, n_exp // ec),
        in_specs=[
            pl.BlockSpec((tn, d), lambda t, c: (t, 0)),
            pl.BlockSpec((d, tn), lambda t, c: (0, t)),
            pl.BlockSpec((n_sets * N_KEYS, tn), lambda t, c: (0, t)),
            pl.BlockSpec((PEER_HEADS, SUBLANES, tn), lambda t, c: (0, 0, t)),
            pl.BlockSpec((ec, d), lambda t, c: (c, 0)),
            pl.BlockSpec((d, ec), lambda t, c: (0, jnp.maximum(c - 1, 0))),
            pl.BlockSpec((d, ec), lambda t, c: (0, c)),
        ],
        out_specs=pl.BlockSpec((tn, d), lambda t, c: (t, 0)),
        out_shape=jax.ShapeDtypeStruct((n, d), f32),
        scratch_shapes=[
            pltpu.VMEM((PEER_HEADS, N_KEYS, tn), f32),
            pltpu.VMEM((ec, tn // _n_halves(tn)), bf16), pltpu.VMEM((ec, tn // _n_halves(tn)), bf16),
            pltpu.VMEM((d, tn), f32),
        ],
        compiler_params=_params("parallel", "arbitrary"),
        name="peer_experts",
    )(x, ht, sc, st, w["pu"], w["pvt"], w["pvt"])


def _final_body(x_ref, g_ref, o_ref):
    o_ref[...] = _rms(x_ref[...], g_ref[...])


def _final_norm(x, g, tn):
    n, d = x.shape
    spec = pl.BlockSpec((tn, d), lambda i: (i, 0))
    return pl.pallas_call(
        _final_body, grid=(n // tn,), in_specs=[spec, _const_spec(g.shape)], out_specs=spec,
        out_shape=jax.ShapeDtypeStruct(x.shape, f32), compiler_params=_params("parallel"), name="final_norm",
    )(x, g)


def _layer_weights(l, norm1_g, w_in, b_f, ln_g, ln_b, w_s, b_s, w_pool, pool_scale, w_br, w_out, norm2_g, peer_wq,
                   peer_keys, peer_u, peer_v, lc):
    d = w_in.shape[1]
    wi = w_in[l]
    o_f = 3 * FOX_WIDTH
    o_u = o_f + FOX_HEADS
    o_g = o_u + 2 * GMLP_WIDTH + POOL_WIDTH
    wf = jnp.zeros((d, LANES), f32).at[:, :FOX_HEADS].set(wi[:, o_f:o_u])
    bf = jnp.zeros((1, LANES), f32).at[0, :FOX_HEADS].set(b_f[l])
    gdim = GMLP_WIDTH // GMLP_GROUPS
    pdim = POOL_WIDTH // POOL_GROUPS
    wpool = jnp.zeros((POOL_WIDTH, POOL_WIDTH), f32)
    for g in range(POOL_GROUPS):
        wpool = wpool.at[g * pdim:(g + 1) * pdim, g * pdim:(g + 1) * pdim].set(w_pool[l, g])
    spare = LANES - HEAD_DIM

    def head_lanes(cols):
        return jnp.pad(cols.reshape(d, FOX_HEADS, HEAD_DIM), ((0, 0), (0, 0), (0, spare))).reshape(d, ATTN_WIDTH)

    qc = jnp.zeros((FOX_HEADS, LANES), f32).at[:, HEAD_DIM:HEAD_DIM + C_PIECES].set(-1.0).reshape(1, ATTN_WIDTH)
    wqk = jnp.concatenate([head_lanes(wi[:, :FOX_WIDTH]), head_lanes(wi[:, FOX_WIDTH:2 * FOX_WIDTH])], axis=1)
    return {
        "g1": norm1_g[l][None], "wqk": wqk.astype(bf16), "qc": qc, "wkv": wi[:, FOX_WIDTH:o_f].astype(bf16),
        "wf": wf.astype(bf16), "bf": bf,
        "wugp": wi[:, o_u:o_g].astype(bf16), "lng": ln_g[l][None], "lnb": ln_b[l][None], "wg": wi[:, o_g:].astype(bf16),
        "ws": w_s[l][:, :lc, :lc], "bs": jnp.repeat(jnp.transpose(b_s[l][:, :lc]), gdim, axis=1),
        "wpool": wpool.astype(bf16), "pscale": pool_scale[l][None],
        "wbr": w_br[l].astype(bf16), "wout": w_out[l].astype(bf16),
        "g2": norm2_g[l][None], "wqt": jnp.transpose(peer_wq[l]).astype(bf16),
        "keys": peer_keys[l].reshape(2 * PEER_HEADS, N_KEYS, PEER_HALF).astype(bf16),
        "pu": peer_u[l].astype(bf16), "pvt": jnp.transpose(peer_v[l]).astype(bf16),
    }


def _pick(n, prefs):
    for t in prefs:
        if n % t == 0:
            return t
    return n


def _layer(x, w, *, k_hist, v_hist, logf_hist, pool_hist, lc, tiles):
    nseq, L, d = x.shape
    n = nseq * L
    q, k, v, kb, vb, logf, u, vn, zp, gate = _inproj(x.reshape(n, d), w, tiles["inproj"])
    r3 = lambda a: a.reshape(nseq, L, a.shape[-1])
    q, kb, vb, logf3, u, vn, zp, gate = map(r3, (q, kb, vb, logf, u, vn, zp, gate))
    past = 0 if k_hist is None else k_hist.shape[1]
    spare = LANES - HEAD_DIM
    if past:
        k_hist = jnp.pad(k_hist.astype(bf16), ((0, 0), (0, 0), (0, 0), (0, spare)))
        kb = jnp.concatenate([k_hist.reshape(nseq, past, ATTN_WIDTH), kb], axis=1)
        vb = jnp.concatenate([v_hist.reshape(nseq, past, FOX_WIDTH).astype(bf16), vb], axis=1)
        lf_all = jnp.concatenate([logf_hist.astype(f32), logf3], axis=1)
    else:
        lf_all = logf3
    lk = past + L
    tk = tiles["tk"]
    lk_pad = -(-lk // tk) * tk
    if lk_pad != lk:
        padk = ((0, 0), (0, lk_pad - lk), (0, 0))
        kb, vb, lf_all = jnp.pad(kb, padk), jnp.pad(vb, padk), jnp.pad(lf_all, padk)
    nh = ATTN_HEADS_PER_STEP
    c_t, c_hi, c_mid, c_lo = _cumsum_lanes(jnp.transpose(lf_all, (0, 2, 1)))
    cq = jnp.transpose(c_t.reshape(nseq, FOX_HEADS // nh, nh, lk_pad)[:, :, :, past:past + L], (0, 1, 3, 2))
    pieces = jnp.transpose(jnp.stack([c_hi, c_mid, c_lo], axis=-1), (0, 2, 1, 3))
    pieces = jnp.pad(pieces, ((0, 0), (0, 0), (0, 0), (HEAD_DIM, spare - C_PIECES)))
    k_aug = kb + pieces.reshape(nseq, lk_pad, ATTN_WIDTH)
    o_a = _attention(q, k_aug, vb, cq, tq=tiles["tq"], tk=tk, q_off=past)
    tn = tiles["merge"]
    nt = L // tn
    first = jnp.zeros((nseq, HALO, POOL_WIDTH), f32)
    if pool_hist is not None:
        first = first.at[:, HALO - POOL_HIST:].set(pool_hist.astype(f32))
    tails = zp.reshape(nseq, nt, tn, POOL_WIDTH)[:, :nt - 1, tn - HALO:, :]
    prev = jnp.concatenate([first[:, None], tails], axis=1)
    x_mid = _merge(x, o_a, u, vn, zp, prev, gate, w, tn=tn, lc=lc, pos0=past)
    ht, sc, st = _route(x_mid.reshape(n, d), w, tiles["route"])
    x_new = _experts(x_mid.reshape(n, d), ht, sc, st, w, tn=tiles["experts"], ec=tiles["ec"])
    new_pool = zp[:, L - POOL_HIST:]
    return x_new.reshape(nseq, L, d), r3(k), r3(v), logf3, new_pool, vn


def kernel(x_prompt, x_sample, cache_fox_k, cache_fox_v, cache_fox_logf, state_pool, norm1_g, w_in, b_f, ln_g, ln_b,
           w_s, b_s, w_pool, pool_scale, w_br, w_out, norm2_g, peer_wq, peer_keys, peer_u, peer_v, final_g):
    depth = w_in.shape[0]
    bp, sp, d = x_prompt.shape
    bd, ls, _ = x_sample.shape
    assert sp % GMLP_CHUNK == 0 and ls <= GMLP_CHUNK and ls >= POOL_HIST and ls % 16 == 0
    params = (norm1_g, w_in, b_f, ln_g, ln_b, w_s, b_s, w_pool, pool_scale, w_br, w_out, norm2_g, peer_wq, peer_keys,
              peer_u, peer_v)
    tiles_p = {"inproj": _pick(bp * sp, (256,)), "tq": _pick(sp, (512, 256, 128)), "tk": _pick(sp, (512, 256, 128)),
               "merge": _pick(sp, (256, 128)), "route": _pick(bp * sp, (512, 256, 128)),
               "experts": _pick(bp * sp, (512, 256, 128)), "ec": EXPERT_CHUNK}
    ns = bd * ls
    tiles_s = {"inproj": ns, "tq": ls, "tk": 128, "merge": ls, "route": ns, "experts": ns, "ec": EXPERT_CHUNK}
    xp, xs = x_prompt, x_sample
    outs_p, outs_s = [], []
    for l in range(depth):
        wp = _layer_weights(l, *params, lc=GMLP_CHUNK)
        xp, k, v, lf, pool, _ = _layer(xp, wp, k_hist=None, v_hist=None, logf_hist=None, pool_hist=None,
                                       lc=GMLP_CHUNK, tiles=tiles_p)
        outs_p.append((k.reshape(bp, sp, FOX_HEADS, HEAD_DIM), v.reshape(bp, sp, FOX_HEADS, HEAD_DIM), lf, pool))
        wsm = dict(wp)
        wsm["ws"] = w_s[l][:, :ls, :ls]
        wsm["bs"] = jnp.repeat(jnp.transpose(b_s[l][:, :ls]), GMLP_WIDTH // GMLP_GROUPS, axis=1)
        xs, k, v, lf, pool, vn = _layer(xs, wsm, k_hist=cache_fox_k[l], v_hist=cache_fox_v[l],
                                        logf_hist=cache_fox_logf[l], pool_hist=state_pool[l], lc=ls, tiles=tiles_s)
        outs_s.append((k.reshape(bd, ls, FOX_HEADS, HEAD_DIM), v.reshape(bd, ls, FOX_HEADS, HEAD_DIM), lf, pool, vn))
    g = final_g[None]
    y_prompt = _final_norm(xp.reshape(bp * sp, d), g, _pick(bp * sp, (512, 256, 128))).reshape(bp, sp, d)
    y_sample = _final_norm(xs.reshape(ns, d), g, ns).reshape(bd, ls, d)
    stack = lambda outs, i: jnp.stack([o[i] for o in outs])
    return (y_prompt, y_sample, stack(outs_p, 0), stack(outs_p, 1), stack(outs_p, 2), stack(outs_p, 3),
            stack(outs_s, 0), stack(outs_s, 1), stack(outs_s, 2), stack(outs_s, 3), stack(outs_s, 4))
```

```python
import functools

import jax
import jax.numpy as jnp
from jax import lax
from jax.experimental import pallas as pl
from jax.experimental.pallas import tpu as pltpu

f32 = jnp.float32
bf16 = jnp.bfloat16

EPS = 1e-6
FOX_HEADS = 8
HEAD_DIM = 64
FOX_WIDTH = FOX_HEADS * HEAD_DIM
GMLP_GROUPS = 4
GMLP_WIDTH = 256
GMLP_CHUNK = 128
SGU_BLOCK = 64
POOL_GROUPS = 4
POOL_WIDTH = 256
POOL_WINDOWS = (2, 4, 8, 16)
POOL_HIST = 15
HALO = 16
N_BRANCH = 3
PEER_HEADS = 8
N_KEYS = 128
PEER_TOPK = 16
PEER_HALF = 128

LANES = 128
SUBLANES = 8
EXPERT_CHUNK = SUBLANES * N_KEYS
TOP_ROWS = 3 * SUBLANES
VMEM_LIMIT = 56 * 1024 * 1024
NEG_INF = float("-inf")
LOG2E = 1.4426950408889634
C_PIECES = 3
ATTN_HEADS_PER_STEP = 4
ATTN_WIDTH = FOX_HEADS * LANES


def _params(*sem):
    return pltpu.CompilerParams(dimension_semantics=sem, vmem_limit_bytes=VMEM_LIMIT)


def _const_spec(shape):
    nd = len(shape)
    return pl.BlockSpec(shape, lambda *_: (0,) * nd)


def _layer_spec(a, l):
    nd = a.ndim - 1
    return pl.BlockSpec((None,) + a.shape[1:], lambda *_: (l,) + (0,) * nd)


def _rms(x, g):
    return x * lax.rsqrt(jnp.mean(x * x, axis=-1, keepdims=True) + EPS) * g


def _gelu(x):
    return 0.5 * x * (1.0 + lax.erf(x * 0.7071067811865476))


def _inproj_body(x_ref, g1_ref, wqk_ref, qc_ref, wkv_ref, wf_ref, bf_ref, wugp_ref, lng_ref, lnb_ref, wg_ref,
                 q_ref, k_ref, v_ref, kb_ref, vb_ref, lf_ref, u_ref, vn_ref, zp_ref, gate_ref):
    h = _rms(x_ref[...], g1_ref[...]).astype(bf16)
    zqk = jnp.dot(h, wqk_ref[...], preferred_element_type=f32)
    q_ref[...] = (zqk[:, :ATTN_WIDTH] * (HEAD_DIM ** -0.5 * LOG2E) + qc_ref[...]).astype(bf16)
    kb_ref[...] = zqk[:, ATTN_WIDTH:].astype(bf16)
    zkv = jnp.dot(h, wkv_ref[...], preferred_element_type=f32)
    v = zkv[:, FOX_WIDTH:]
    k_ref[...] = zkv[:, :FOX_WIDTH]
    v_ref[...] = v
    vb_ref[...] = v.astype(bf16)
    zf = jnp.dot(h, wf_ref[...], preferred_element_type=f32) + bf_ref[...]
    lf_ref[...] = jax.nn.log_sigmoid(zf)[:, :FOX_HEADS]
    zugp = jnp.dot(h, wugp_ref[...], preferred_element_type=f32)
    u_ref[...] = _gelu(zugp[:, :GMLP_WIDTH])
    gv = _gelu(zugp[:, GMLP_WIDTH:2 * GMLP_WIDTH])
    mu = jnp.mean(gv, axis=-1, keepdims=True)
    gc = gv - mu
    vn_ref[...] = gc * lax.rsqrt(jnp.mean(gc * gc, axis=-1, keepdims=True) + EPS) * lng_ref[...] + lnb_ref[...]
    zp_ref[...] = zugp[:, 2 * GMLP_WIDTH:]
    zg = jnp.dot(h, wg_ref[...], preferred_element_type=f32)
    gate_ref[...] = 1.0 / (1.0 + jnp.exp(-zg))


def _inproj(x, w, l, tn):
    n, d = x.shape
    row = lambda width: pl.BlockSpec((tn, width), lambda i: (i, 0))
    outs = [
        (ATTN_WIDTH, bf16), (FOX_WIDTH, f32), (FOX_WIDTH, f32), (ATTN_WIDTH, bf16), (FOX_WIDTH, bf16),
        (FOX_HEADS, f32), (GMLP_WIDTH, f32), (GMLP_WIDTH, f32), (POOL_WIDTH, f32), (N_BRANCH * d, f32),
    ]
    weights = [w["g1"], w["wqk"], w["qc"], w["wkv"], w["wf"], w["bf"], w["wugp"], w["lng"], w["lnb"], w["wg"]]
    return pl.pallas_call(
        _inproj_body,
        grid=(n // tn,),
        in_specs=[row(d)] + [_layer_spec(a, l) for a in weights],
        out_specs=[row(width) for width, _ in outs],
        out_shape=[jax.ShapeDtypeStruct((n, width), dt) for width, dt in outs],
        compiler_params=_params("parallel"),
        name="inproj",
    )(x, *weights)


def _cumsum_body(x_ref, c_ref, hi_ref, mid_ref, lo_ref):
    x = x_ref[0]
    n = x.shape[-1]
    lane = lax.broadcasted_iota(jnp.int32, x.shape, 1)
    shift = 1
    while shift < n:
        x = x + jnp.where(lane >= shift, pltpu.roll(x, shift, 1), 0.0)
        shift *= 2
    c = x * LOG2E
    c_ref[0] = c
    hi = c.astype(bf16)
    r1 = c - hi.astype(f32)
    mid = r1.astype(bf16)
    hi_ref[0] = hi
    mid_ref[0] = mid
    lo_ref[0] = (r1 - mid.astype(f32)).astype(bf16)


def _cumsum_lanes(x):
    b, r, n = x.shape
    spec = pl.BlockSpec((1, r, n), lambda i: (i, 0, 0))
    return pl.pallas_call(
        _cumsum_body, grid=(b,), in_specs=[spec], out_specs=[spec] * 4,
        out_shape=[jax.ShapeDtypeStruct(x.shape, f32)] + [jax.ShapeDtypeStruct(x.shape, bf16)] * 3,
        compiler_params=_params("parallel"), name="cumsum",
    )(x)


def _attn_body(q_ref, k_ref, v_ref, cq_ref, o_ref, *, tq, tk, q_off):
    qi = pl.program_id(2)
    lane = lax.broadcasted_iota(jnp.int32, (tq, LANES), 1)
    row0 = q_off + qi * tq
    n_full = (row0 + 1) // tk
    n_all = (row0 + tq + tk - 1) // tk
    nh = ATTN_HEADS_PER_STEP
    qs = [q_ref[0, :, hh * LANES:(hh + 1) * LANES] for hh in range(nh)]
    cqs = [jnp.broadcast_to(cq_ref[0, 0, :, hh:hh + 1], (tq, LANES)) for hh in range(nh)]
    n_col = tk // LANES

    def step(t, carry, masked):
        start = pl.multiple_of(t * tk, tk)
        if masked:
            kpos = start + lax.broadcasted_iota(jnp.int32, (tq, tk), 1)
            qpos = row0 + lax.broadcasted_iota(jnp.int32, (tq, tk), 0)
            allowed = kpos <= qpos
        out = []
        for hh in range(nh):
            m, l, acc = carry[hh]
            ks = k_ref[0, pl.ds(start, tk), hh * LANES:(hh + 1) * LANES]
            vs = v_ref[0, pl.ds(start, tk), (hh // 2) * LANES:(hh // 2 + 1) * LANES]
            s = lax.dot_general(qs[hh], ks, (((1,), (1,)), ((), ())), preferred_element_type=f32)
            if masked:
                s = jnp.where(allowed, s, NEG_INF)
            cols = [s[:, j * LANES:(j + 1) * LANES] for j in range(n_col)]
            smax = functools.reduce(jnp.maximum, cols)
            m_new = jnp.maximum(m, jnp.max(smax, axis=1, keepdims=True) + cqs[hh])
            alpha = jnp.exp2(m - m_new)
            r = cqs[hh] - m_new
            ps = [jnp.exp2(col + r) for col in cols]
            l = alpha * l + functools.reduce(jnp.add, ps)
            p = jnp.concatenate([x.astype(bf16) for x in ps], axis=1)
            acc = alpha * acc + jnp.dot(p, vs, preferred_element_type=f32)
            out.append((m_new, l, acc))
        return tuple(out)

    one = (jnp.full((tq, LANES), NEG_INF, f32), jnp.zeros((tq, LANES), f32), jnp.zeros((tq, LANES), f32))
    carry = lax.fori_loop(0, n_full, functools.partial(step, masked=False), (one,) * nh)
    carry = lax.fori_loop(n_full, n_all, functools.partial(step, masked=True), carry)
    accs = [acc / jnp.sum(l, axis=1, keepdims=True) for _, l, acc in carry]
    for pr in range(nh // 2):
        o_ref[0, :, pr * LANES:(pr + 1) * LANES] = jnp.where(
            lane < HEAD_DIM, accs[2 * pr], accs[2 * pr + 1]).astype(o_ref.dtype)


def _attention(q, k, v, cq, *, tq, tk, q_off):
    b, lq, _ = q.shape
    lk = k.shape[1]
    nh = ATTN_HEADS_PER_STEP
    return pl.pallas_call(
        functools.partial(_attn_body, tq=tq, tk=tk, q_off=q_off),
        grid=(b, FOX_HEADS // nh, lq // tq),
        in_specs=[
            pl.BlockSpec((1, tq, nh * LANES), lambda b_, h_, i: (b_, i, h_)),
            pl.BlockSpec((1, lk, nh * LANES), lambda b_, h_, i: (b_, 0, h_)),
            pl.BlockSpec((1, lk, nh * HEAD_DIM), lambda b_, h_, i: (b_, 0, h_)),
            pl.BlockSpec((1, 1, tq, nh), lambda b_, h_, i: (b_, h_, i, 0)),
        ],
        out_specs=pl.BlockSpec((1, tq, nh * HEAD_DIM), lambda b_, h_, i: (b_, i, h_)),
        out_shape=jax.ShapeDtypeStruct((b, lq, FOX_WIDTH), bf16),
        compiler_params=_params("parallel", "parallel", "arbitrary"),
        name="fox_attention",
    )(q, k, v, cq)


def _merge_body(x_ref, oa_ref, u_ref, vn_ref, zp_ref, prev_ref, gate_ref, ws_ref, bs_ref, wpool_ref, pscale_ref,
                wbr_ref, wout_ref, o_ref, *, tn, lc, pos0):
    j = pl.program_id(1)
    d = x_ref.shape[-1]
    t_io = lax.broadcasted_iota(jnp.int32, (lc, lc), 0)
    s_io = lax.broadcasted_iota(jnp.int32, (lc, lc), 1)
    causal = (s_io // SGU_BLOCK) <= (t_io // SGU_BLOCK)
    grp = lax.broadcasted_iota(jnp.int32, (lc, GMLP_WIDTH), 1) // (GMLP_WIDTH // GMLP_GROUPS)
    wmask = [jnp.where(causal, ws_ref[g], 0.0).astype(bf16) for g in range(GMLP_GROUPS)]
    ob = []
    for c in range(tn // lc):
        vc = vn_ref[0, c * lc:(c + 1) * lc, :].astype(bf16)
        s = bs_ref[...]
        for g in range(GMLP_GROUPS):
            sg = jnp.dot(wmask[g], vc, preferred_element_type=f32)
            s = s + jnp.where(grp == g, sg, 0.0)
        ob.append(u_ref[0, c * lc:(c + 1) * lc, :] * s)
    o_b = ob[0] if len(ob) == 1 else jnp.concatenate(ob, axis=0)
    zp = zp_ref[0]
    ext = jnp.concatenate([prev_ref[0, 0], zp], axis=0)
    s2 = ext + pltpu.roll(ext, 1, 0)
    s4 = s2 + pltpu.roll(s2, 2, 0)
    s8 = s4 + pltpu.roll(s4, 4, 0)
    s16 = s8 + pltpu.roll(s8, 8, 0)
    pgrp = lax.broadcasted_iota(jnp.int32, (tn, POOL_WIDTH), 1) // (POOL_WIDTH // POOL_GROUPS)
    pos = pos0 + j * tn + lax.broadcasted_iota(jnp.int32, (tn, POOL_WIDTH), 0)
    win = jnp.where(pgrp == 0, s2[HALO:], jnp.where(pgrp == 1, s4[HALO:], jnp.where(pgrp == 2, s8[HALO:], s16[HALO:])))
    width = jnp.where(pgrp == 0, 2, jnp.where(pgrp == 1, 4, jnp.where(pgrp == 2, 8, 16)))
    cnt = jnp.minimum(pos + 1, width).astype(f32)
    dlt = win / cnt - zp
    o_c = jnp.dot(dlt.astype(bf16), wpool_ref[...], preferred_element_type=f32) * pscale_ref[...]
    ya = jnp.dot(oa_ref[0], wbr_ref[:FOX_WIDTH, :], preferred_element_type=f32)
    yb = jnp.dot(o_b.astype(bf16), wbr_ref[FOX_WIDTH:FOX_WIDTH + GMLP_WIDTH, :], preferred_element_type=f32)
    yc = jnp.dot(o_c.astype(bf16), wbr_ref[FOX_WIDTH + GMLP_WIDTH:, :], preferred_element_type=f32)
    merged = gate_ref[0, :, :d] * ya + gate_ref[0, :, d:2 * d] * yb + gate_ref[0, :, 2 * d:] * yc
    o_ref[0] = x_ref[0] + jnp.dot(merged.astype(bf16), wout_ref[...], preferred_element_type=f32)


def _merge(x, oa, u, vn, zp, prev, gate, w, l, *, tn, lc, pos0):
    nseq, L, d = x.shape
    seq = lambda width: pl.BlockSpec((1, tn, width), lambda b_, j: (b_, j, 0))
    weights = [w["ws"], w["bs"], w["wpool"], w["pscale"], w["wbr"], w["wout"]]
    return pl.pallas_call(
        functools.partial(_merge_body, tn=tn, lc=lc, pos0=pos0),
        grid=(nseq, L // tn),
        in_specs=[seq(d), seq(FOX_WIDTH), seq(GMLP_WIDTH), seq(GMLP_WIDTH), seq(POOL_WIDTH),
                  pl.BlockSpec((1, 1, HALO, POOL_WIDTH), lambda b_, j: (b_, j, 0, 0)), seq(N_BRANCH * d)]
        + [_layer_spec(a, l) for a in weights],
        out_specs=seq(d),
        out_shape=jax.ShapeDtypeStruct(x.shape, f32),
        compiler_params=_params("parallel", "parallel"),
        name="merge",
    )(x, oa, u, vn, zp, prev, gate, *weights)


def _top_values(block, k):
    tops = []
    for _ in range(k):
        m = jnp.max(block, axis=0, keepdims=True)
        tops.append(m)
        block = jnp.where(block == m, NEG_INF, block)
    return tops


def _route_body(x_ref, g2_ref, wqt_ref, keys_ref, ht_ref, sc_ref, st_ref, top_ref, *, tn):
    h = _rms(x_ref[...], g2_ref[...]).astype(bf16)
    ht = h.T
    ht_ref[...] = ht
    qt = jnp.dot(wqt_ref[...], ht, preferred_element_type=f32).astype(bf16)
    n_sets = 2 * PEER_HEADS
    for r in range(n_sets):
        sc_ref[r * N_KEYS:(r + 1) * N_KEYS, :] = jnp.dot(
            keys_ref[r], qt[r * PEER_HALF:(r + 1) * PEER_HALF, :], preferred_element_type=f32)

    kk = PEER_TOPK + 1

    def half_top(r, _):
        base = pl.multiple_of(r * N_KEYS, N_KEYS)
        for tb in range(tn // LANES):
            blk = sc_ref[pl.ds(base, N_KEYS), tb * LANES:(tb + 1) * LANES]
            tops = _top_values(blk, kk)
            top_ref[r, :, tb * LANES:(tb + 1) * LANES] = jnp.concatenate(
                tops + [tops[-1]] * (TOP_ROWS - kk), axis=0)
        return 0

    lax.fori_loop(0, n_sets, half_top, 0)

    def head_stats(hd, _):
        for tb in range(tn // LANES):
            sl = slice(tb * LANES, (tb + 1) * LANES)
            a = top_ref[2 * hd, :, sl]
            b = top_ref[2 * hd + 1, :, sl]
            rows8 = lax.broadcasted_iota(jnp.int32, (SUBLANES, LANES), 0)
            never = jnp.full((SUBLANES, LANES), NEG_INF, f32)
            blocks = [(a[0:1], b[0:SUBLANES]), (a[0:1], b[SUBLANES:2 * SUBLANES])]
            for p in range(1, SUBLANES):
                blocks.append((a[p:p + 1], jnp.where(rows8 < kk // (p + 1), b[0:SUBLANES], never)))
            blocks.append((a[SUBLANES:2 * SUBLANES], b[0:1]))
            blocks.append((jnp.where(rows8 == 0, a[PEER_TOPK:PEER_TOPK + 1], a[0:1]),
                           jnp.where(rows8 == 0, b[0:1], jnp.where(rows8 == 1, b[PEER_TOPK:PEER_TOPK + 1], never))))
            cand = jnp.concatenate([ab + bb for ab, bb in blocks], axis=0)
            tops = _top_values(cand, kk)
            tau = 0.5 * (tops[-2] + tops[-1])
            z = jnp.zeros((1, LANES), f32)
            for ab, bb in blocks:
                kept = bb >= tau - ab
                z = z + jnp.sum(jnp.where(kept, jnp.exp((ab - a[0:1]) + (bb - b[0:1])), 0.0), axis=0, keepdims=True)
            st_ref[hd, :, sl] = jnp.concatenate(
                [a[0:1], b[0:1], tau, 0.5 / z, jnp.zeros((SUBLANES - 4, LANES), f32)], axis=0)
        return 0

    lax.fori_loop(0, PEER_HEADS, head_stats, 0)


def _route(x, w, l, tn):
    n, d = x.shape
    n_sets = 2 * PEER_HEADS
    return pl.pallas_call(
        functools.partial(_route_body, tn=tn),
        grid=(n // tn,),
        in_specs=[pl.BlockSpec((tn, d), lambda i: (i, 0)), _layer_spec(w["g2"], l), _layer_spec(w["wqt"], l),
                  _layer_spec(w["keys"], l)],
        out_specs=[pl.BlockSpec((d, tn), lambda i: (0, i)), pl.BlockSpec((n_sets * N_KEYS, tn), lambda i: (0, i)),
                   pl.BlockSpec((PEER_HEADS, SUBLANES, tn), lambda i: (0, 0, i))],
        out_shape=[jax.ShapeDtypeStruct((d, n), bf16), jax.ShapeDtypeStruct((n_sets * N_KEYS, n), f32),
                   jax.ShapeDtypeStruct((PEER_HEADS, SUBLANES, n), f32)],
        scratch_shapes=[pltpu.VMEM((n_sets, TOP_ROWS, tn), f32)],
        compiler_params=_params("parallel"),
        name="peer_route",
    )(x, w["g2"], w["wqt"], w["keys"])


def _n_halves(tn):
    return 2 if tn % (2 * LANES) == 0 else 1


def _experts_body(x_ref, ht_ref, sc_ref, st_ref, pu_ref, pvp_ref, pvc_ref, o_ref, q_ref, act0_ref, act1_ref, acc_ref,
                  *, tn, ec):
    c = pl.program_id(1)
    n_sub = ec // N_KEYS
    n_half = _n_halves(tn)
    half = tn // n_half
    acts = (act0_ref, act1_ref)

    @pl.when(c == 0)
    def _():
        for hd in range(PEER_HEADS):
            s2 = sc_ref[(2 * hd + 1) * N_KEYS:(2 * hd + 2) * N_KEYS, :]
            q_ref[hd] = (jnp.exp(s2 - st_ref[hd, 1:2, :]) * st_ref[hd, 3:4, :]).astype(bf16)
        acc_ref[...] = jnp.zeros_like(acc_ref)
        acts[n_half - 1][...] = jnp.zeros_like(acts[n_half - 1])

    i0 = pl.multiple_of(c * n_sub, n_sub)

    def hidden(hf):
        return jnp.dot(pu_ref[...], ht_ref[:, hf * half:(hf + 1) * half], preferred_element_type=f32)

    def values(pv_ref, hf):
        acc_ref[:, hf * half:(hf + 1) * half] += jnp.dot(pv_ref[...], acts[hf][...], preferred_element_type=f32)

    def weigh(hf, a):
        for tb in range(half // LANES):
            lanes = slice(hf * half + tb * LANES, hf * half + (tb + 1) * LANES)
            thr_rows, p_rows = [], []
            for hd in range(PEER_HEADS):
                s1 = sc_ref[pl.ds(2 * hd * N_KEYS + i0, n_sub), lanes]
                thr_rows.append(st_ref[hd, 2:3, lanes] - s1)
                p_rows.append(jnp.exp(s1 - st_ref[hd, 0:1, lanes]))
            for ii in range(n_sub):
                gsum = jnp.zeros((N_KEYS, LANES), bf16)
                for hd in range(PEER_HEADS):
                    s2 = sc_ref[(2 * hd + 1) * N_KEYS:(2 * hd + 2) * N_KEYS, lanes]
                    kept = s2 >= thr_rows[hd][ii:ii + 1]
                    prow = jnp.broadcast_to(p_rows[hd][ii:ii + 1], (N_KEYS, LANES)).astype(bf16)
                    gsum = gsum + jnp.where(kept, q_ref[hd, :, lanes], jnp.zeros((), bf16)) * prow
                aa = a[ii * N_KEYS:(ii + 1) * N_KEYS, tb * LANES:(tb + 1) * LANES]
                acts[hf][ii * N_KEYS:(ii + 1) * N_KEYS, tb * LANES:(tb + 1) * LANES] = (
                    (aa + aa * lax.erf(aa * 0.7071067811865476)) * gsum).astype(bf16)

    a0 = hidden(0)
    values(pvp_ref, n_half - 1)
    weigh(0, a0)
    if n_half == 2:
        a1 = hidden(1)
        values(pvc_ref, 0)
        weigh(1, a1)

    @pl.when(c == pl.num_programs(1) - 1)
    def _():
        values(pvc_ref, n_half - 1)
        o_ref[...] = x_ref[...] + acc_ref[...].T


def _experts(x, ht, sc, st, w, l, *, tn, ec):
    n, d = x.shape
    n_exp = w["pu"].shape[1]
    n_sets = 2 * PEER_HEADS
    return pl.pallas_call(
        functools.partial(_experts_body, tn=tn, ec=ec),
        grid=(n // tn, n_exp // ec),
        in_specs=[
            pl.BlockSpec((tn, d), lambda t, c: (t, 0)),
            pl.BlockSpec((d, tn), lambda t, c: (0, t)),
            pl.BlockSpec((n_sets * N_KEYS, tn), lambda t, c: (0, t)),
            pl.BlockSpec((PEER_HEADS, SUBLANES, tn), lambda t, c: (0, 0, t)),
            pl.BlockSpec((None, ec, d), lambda t, c: (l, c, 0)),
            pl.BlockSpec((None, d, ec), lambda t, c: (l, 0, jnp.maximum(c - 1, 0))),
            pl.BlockSpec((None, d, ec), lambda t, c: (l, 0, c)),
        ],
        out_specs=pl.BlockSpec((tn, d), lambda t, c: (t, 0)),
        out_shape=jax.ShapeDtypeStruct((n, d), f32),
        scratch_shapes=[
            pltpu.VMEM((PEER_HEADS, N_KEYS, tn), bf16),
            pltpu.VMEM((ec, tn // _n_halves(tn)), bf16), pltpu.VMEM((ec, tn // _n_halves(tn)), bf16),
            pltpu.VMEM((d, tn), f32),
        ],
        compiler_params=_params("parallel", "arbitrary"),
        name="peer_experts",
    )(x, ht, sc, st, w["pu"], w["pvt"], w["pvt"])


def _final_body(x_ref, g_ref, o_ref):
    o_ref[...] = _rms(x_ref[...], g_ref[...])


def _final_norm(x, g, tn):
    n, d = x.shape
    spec = pl.BlockSpec((tn, d), lambda i: (i, 0))
    return pl.pallas_call(
        _final_body, grid=(n // tn,), in_specs=[spec, _const_spec(g.shape)], out_specs=spec,
        out_shape=jax.ShapeDtypeStruct(x.shape, f32), compiler_params=_params("parallel"), name="final_norm",
    )(x, g)


def _gating_weights(w_s, b_s, lc):
    gdim = GMLP_WIDTH // GMLP_GROUPS
    return w_s[:, :, :lc, :lc], jnp.repeat(jnp.transpose(b_s[:, :, :lc], (0, 2, 1)), gdim, axis=2)


def _all_weights(norm1_g, w_in, b_f, ln_g, ln_b, w_s, b_s, w_pool, pool_scale, w_br, w_out, norm2_g, peer_wq,
                 peer_keys, peer_u, peer_v):
    depth, d = w_in.shape[:2]
    o_f = 3 * FOX_WIDTH
    o_u = o_f + FOX_HEADS
    o_g = o_u + 2 * GMLP_WIDTH + POOL_WIDTH
    wf = jnp.zeros((depth, d, LANES), f32).at[:, :, :FOX_HEADS].set(w_in[:, :, o_f:o_u])
    bf = jnp.zeros((depth, 1, LANES), f32).at[:, 0, :FOX_HEADS].set(b_f)
    pdim = POOL_WIDTH // POOL_GROUPS
    wpool = jnp.zeros((depth, POOL_WIDTH, POOL_WIDTH), f32)
    for g in range(POOL_GROUPS):
        wpool = wpool.at[:, g * pdim:(g + 1) * pdim, g * pdim:(g + 1) * pdim].set(w_pool[:, g])
    spare = LANES - HEAD_DIM

    def head_lanes(cols):
        cols = cols.reshape(depth, d, FOX_HEADS, HEAD_DIM)
        return jnp.pad(cols, ((0, 0), (0, 0), (0, 0), (0, spare))).reshape(depth, d, ATTN_WIDTH)

    qc = jnp.zeros((FOX_HEADS, LANES), f32).at[:, HEAD_DIM:HEAD_DIM + C_PIECES].set(-1.0).reshape(1, 1, ATTN_WIDTH)
    wqk = jnp.concatenate([head_lanes(w_in[:, :, :FOX_WIDTH]), head_lanes(w_in[:, :, FOX_WIDTH:2 * FOX_WIDTH])], axis=2)
    ws, bs = _gating_weights(w_s, b_s, GMLP_CHUNK)
    return {
        "g1": norm1_g[:, None], "wqk": wqk.astype(bf16), "qc": jnp.broadcast_to(qc, (depth, 1, ATTN_WIDTH)),
        "wkv": w_in[:, :, FOX_WIDTH:o_f].astype(bf16), "wf": wf.astype(bf16), "bf": bf,
        "wugp": w_in[:, :, o_u:o_g].astype(bf16), "lng": ln_g[:, None], "lnb": ln_b[:, None],
        "wg": w_in[:, :, o_g:].astype(bf16),
        "ws": ws, "bs": bs, "wpool": wpool.astype(bf16), "pscale": pool_scale[:, None],
        "wbr": w_br.astype(bf16), "wout": w_out.astype(bf16),
        "g2": norm2_g[:, None], "wqt": jnp.transpose(peer_wq, (0, 2, 1)).astype(bf16),
        "keys": peer_keys.reshape(depth, 2 * PEER_HEADS, N_KEYS, PEER_HALF).astype(bf16),
        "pu": peer_u.astype(bf16), "pvt": jnp.transpose(peer_v, (0, 2, 1)).astype(bf16),
    }


def _pick(n, prefs):
    for t in prefs:
        if n % t == 0:
            return t
    return n


def _layer(x, w, l, *, k_hist, v_hist, logf_hist, pool_hist, lc, tiles):
    nseq, L, d = x.shape
    n = nseq * L
    q, k, v, kb, vb, logf, u, vn, zp, gate = _inproj(x.reshape(n, d), w, l, tiles["inproj"])
    r3 = lambda a: a.reshape(nseq, L, a.shape[-1])
    q, kb, vb, logf3, u, vn, zp, gate = map(r3, (q, kb, vb, logf, u, vn, zp, gate))
    past = 0 if k_hist is None else k_hist.shape[1]
    spare = LANES - HEAD_DIM
    if past:
        k_hist = jnp.pad(k_hist.astype(bf16), ((0, 0), (0, 0), (0, 0), (0, spare)))
        kb = jnp.concatenate([k_hist.reshape(nseq, past, ATTN_WIDTH), kb], axis=1)
        vb = jnp.concatenate([v_hist.reshape(nseq, past, FOX_WIDTH).astype(bf16), vb], axis=1)
        lf_all = jnp.concatenate([logf_hist.astype(f32), logf3], axis=1)
    else:
        lf_all = logf3
    lk = past + L
    tk = tiles["tk"]
    lk_pad = -(-lk // tk) * tk
    if lk_pad != lk:
        padk = ((0, 0), (0, lk_pad - lk), (0, 0))
        kb, vb, lf_all = jnp.pad(kb, padk), jnp.pad(vb, padk), jnp.pad(lf_all, padk)
    nh = ATTN_HEADS_PER_STEP
    c_t, c_hi, c_mid, c_lo = _cumsum_lanes(jnp.transpose(lf_all, (0, 2, 1)))
    cq = jnp.transpose(c_t.reshape(nseq, FOX_HEADS // nh, nh, lk_pad)[:, :, :, past:past + L], (0, 1, 3, 2))
    pieces = jnp.transpose(jnp.stack([c_hi, c_mid, c_lo], axis=-1), (0, 2, 1, 3))
    pieces = jnp.pad(pieces, ((0, 0), (0, 0), (0, 0), (HEAD_DIM, spare - C_PIECES)))
    k_aug = kb + pieces.reshape(nseq, lk_pad, ATTN_WIDTH)
    o_a = _attention(q, k_aug, vb, cq, tq=tiles["tq"], tk=tk, q_off=past)
    tn = tiles["merge"]
    nt = L // tn
    first = jnp.zeros((nseq, HALO, POOL_WIDTH), f32)
    if pool_hist is not None:
        first = first.at[:, HALO - POOL_HIST:].set(pool_hist.astype(f32))
    tails = zp.reshape(nseq, nt, tn, POOL_WIDTH)[:, :nt - 1, tn - HALO:, :]
    prev = jnp.concatenate([first[:, None], tails], axis=1)
    x_mid = _merge(x, o_a, u, vn, zp, prev, gate, w, l, tn=tn, lc=lc, pos0=past)
    ht, sc, st = _route(x_mid.reshape(n, d), w, l, tiles["route"])
    x_new = _experts(x_mid.reshape(n, d), ht, sc, st, w, l, tn=tiles["experts"], ec=tiles["ec"])
    new_pool = zp[:, L - POOL_HIST:]
    return x_new.reshape(nseq, L, d), r3(k), r3(v), logf3, new_pool, vn


def kernel(x_prompt, x_sample, cache_fox_k, cache_fox_v, cache_fox_logf, state_pool, norm1_g, w_in, b_f, ln_g, ln_b,
           w_s, b_s, w_pool, pool_scale, w_br, w_out, norm2_g, peer_wq, peer_keys, peer_u, peer_v, final_g):
    depth = w_in.shape[0]
    bp, sp, d = x_prompt.shape
    bd, ls, _ = x_sample.shape
    assert sp % GMLP_CHUNK == 0 and ls <= GMLP_CHUNK and ls >= POOL_HIST and ls % 16 == 0
    params = (norm1_g, w_in, b_f, ln_g, ln_b, w_s, b_s, w_pool, pool_scale, w_br, w_out, norm2_g, peer_wq, peer_keys,
              peer_u, peer_v)
    tiles_p = {"inproj": _pick(bp * sp, (256,)), "tq": _pick(sp, (512, 256, 128)), "tk": _pick(sp, (512, 256, 128)),
               "merge": _pick(sp, (256, 128)), "route": _pick(bp * sp, (512, 256, 128)),
               "experts": _pick(bp * sp, (512, 256, 128)), "ec": EXPERT_CHUNK}
    ns = bd * ls
    tiles_s = {"inproj": ns, "tq": ls, "tk": 128, "merge": ls, "route": ns, "experts": ns, "ec": EXPERT_CHUNK}
    xp, xs = x_prompt, x_sample
    outs_p, outs_s = [], []
    wp = _all_weights(*params)
    wsm = dict(wp)
    wsm["ws"], wsm["bs"] = _gating_weights(w_s, b_s, ls)
    for l in range(depth):
        xp, k, v, lf, pool, _ = _layer(xp, wp, l, k_hist=None, v_hist=None, logf_hist=None, pool_hist=None,
                                       lc=GMLP_CHUNK, tiles=tiles_p)
        outs_p.append((k.reshape(bp, sp, FOX_HEADS, HEAD_DIM), v.reshape(bp, sp, FOX_HEADS, HEAD_DIM), lf, pool))
        xs, k, v, lf, pool, vn = _layer(xs, wsm, l, k_hist=cache_fox_k[l], v_hist=cache_fox_v[l],
                                        logf_hist=cache_fox_logf[l], pool_hist=state_pool[l], lc=ls, tiles=tiles_s)
        outs_s.append((k.reshape(bd, ls, FOX_HEADS, HEAD_DIM), v.reshape(bd, ls, FOX_HEADS, HEAD_DIM), lf, pool, vn))
    g = final_g[None]
    y_prompt = _final_norm(xp.reshape(bp * sp, d), g, _pick(bp * sp, (512, 256, 128))).reshape(bp, sp, d)
    y_sample = _final_norm(xs.reshape(ns, d), g, ns).reshape(bd, ls, d)
    stack = lambda outs, i: jnp.stack([o[i] for o in outs])
    return (y_prompt, y_sample, stack(outs_p, 0), stack(outs_p, 1), stack(outs_p, 2), stack(outs_p, 3),
            stack(outs_s, 0), stack(outs_s, 1), stack(outs_s, 2), stack(outs_s, 3), stack(outs_s, 4))
```

```python
import functools

import jax
import jax.numpy as jnp
from jax import lax
from jax.experimental import pallas as pl
from jax.experimental.pallas import tpu as pltpu

f32 = jnp.float32
bf16 = jnp.bfloat16

EPS = 1e-6
FOX_HEADS = 8
HEAD_DIM = 64
FOX_WIDTH = FOX_HEADS * HEAD_DIM
GMLP_GROUPS = 4
GMLP_WIDTH = 256
GMLP_CHUNK = 128
SGU_BLOCK = 64
POOL_GROUPS = 4
POOL_WIDTH = 256
POOL_WINDOWS = (2, 4, 8, 16)
POOL_HIST = 15
HALO = 16
N_BRANCH = 3
PEER_HEADS = 8
N_KEYS = 128
PEER_TOPK = 16
PEER_HALF = 128

LANES = 128
SUBLANES = 8
EXPERT_CHUNK = SUBLANES * N_KEYS
TOP_ROWS = 3 * SUBLANES
VMEM_LIMIT = 56 * 1024 * 1024
NEG_INF = float("-inf")
LOG2E = 1.4426950408889634
C_PIECES = 3
ATTN_HEADS_PER_STEP = 4
ATTN_WIDTH = FOX_HEADS * LANES


def _params(*sem):
    return pltpu.CompilerParams(dimension_semantics=sem, vmem_limit_bytes=VMEM_LIMIT)


def _const_spec(shape):
    nd = len(shape)
    return pl.BlockSpec(shape, lambda *_: (0,) * nd)


def _layer_spec(a, l):
    nd = a.ndim - 1
    return pl.BlockSpec((None,) + a.shape[1:], lambda *_: (l,) + (0,) * nd)


def _rms(x, g):
    return x * lax.rsqrt(jnp.mean(x * x, axis=-1, keepdims=True) + EPS) * g


def _gelu(x):
    return 0.5 * x * (1.0 + lax.erf(x * 0.7071067811865476))


def _inproj_body(x_ref, g1_ref, wqk_ref, qc_ref, wkv_ref, wf_ref, bf_ref, wugp_ref, lng_ref, lnb_ref, wg_ref,
                 q_ref, k_ref, v_ref, kb_ref, vb_ref, lf_ref, u_ref, vn_ref, zp_ref, gate_ref):
    h = _rms(x_ref[...], g1_ref[...]).astype(bf16)
    zqk = jnp.dot(h, wqk_ref[...], preferred_element_type=f32)
    q_ref[...] = (zqk[:, :ATTN_WIDTH] * (HEAD_DIM ** -0.5 * LOG2E) + qc_ref[...]).astype(bf16)
    kb_ref[...] = zqk[:, ATTN_WIDTH:].astype(bf16)
    zkv = jnp.dot(h, wkv_ref[...], preferred_element_type=f32)
    v = zkv[:, FOX_WIDTH:]
    k_ref[...] = zkv[:, :FOX_WIDTH]
    v_ref[...] = v
    vb_ref[...] = v.astype(bf16)
    zf = jnp.dot(h, wf_ref[...], preferred_element_type=f32) + bf_ref[...]
    lf_ref[...] = jax.nn.log_sigmoid(zf)[:, :FOX_HEADS]
    zugp = jnp.dot(h, wugp_ref[...], preferred_element_type=f32)
    u_ref[...] = _gelu(zugp[:, :GMLP_WIDTH])
    gv = _gelu(zugp[:, GMLP_WIDTH:2 * GMLP_WIDTH])
    mu = jnp.mean(gv, axis=-1, keepdims=True)
    gc = gv - mu
    vn_ref[...] = gc * lax.rsqrt(jnp.mean(gc * gc, axis=-1, keepdims=True) + EPS) * lng_ref[...] + lnb_ref[...]
    zp_ref[...] = zugp[:, 2 * GMLP_WIDTH:]
    zg = jnp.dot(h, wg_ref[...], preferred_element_type=f32)
    gate_ref[...] = 1.0 / (1.0 + jnp.exp(-zg))


def _inproj(x, w, l, tn):
    n, d = x.shape
    row = lambda width: pl.BlockSpec((tn, width), lambda i: (i, 0))
    outs = [
        (ATTN_WIDTH, bf16), (FOX_WIDTH, f32), (FOX_WIDTH, f32), (ATTN_WIDTH, bf16), (FOX_WIDTH, bf16),
        (FOX_HEADS, f32), (GMLP_WIDTH, f32), (GMLP_WIDTH, f32), (POOL_WIDTH, f32), (N_BRANCH * d, f32),
    ]
    weights = [w["g1"], w["wqk"], w["qc"], w["wkv"], w["wf"], w["bf"], w["wugp"], w["lng"], w["lnb"], w["wg"]]
    return pl.pallas_call(
        _inproj_body,
        grid=(n // tn,),
        in_specs=[row(d)] + [_layer_spec(a, l) for a in weights],
        out_specs=[row(width) for width, _ in outs],
        out_shape=[jax.ShapeDtypeStruct((n, width), dt) for width, dt in outs],
        compiler_params=_params("parallel"),
        name="inproj",
    )(x, *weights)


def _cumsum_body(x_ref, c_ref, hi_ref, mid_ref, lo_ref):
    x = x_ref[0]
    n = x.shape[-1]
    lane = lax.broadcasted_iota(jnp.int32, x.shape, 1)
    shift = 1
    while shift < n:
        x = x + jnp.where(lane >= shift, pltpu.roll(x, shift, 1), 0.0)
        shift *= 2
    c = x * LOG2E
    c_ref[0] = c
    hi = c.astype(bf16)
    r1 = c - hi.astype(f32)
    mid = r1.astype(bf16)
    hi_ref[0] = hi
    mid_ref[0] = mid
    lo_ref[0] = (r1 - mid.astype(f32)).astype(bf16)


def _cumsum_lanes(x):
    b, r, n = x.shape
    spec = pl.BlockSpec((1, r, n), lambda i: (i, 0, 0))
    return pl.pallas_call(
        _cumsum_body, grid=(b,), in_specs=[spec], out_specs=[spec] * 4,
        out_shape=[jax.ShapeDtypeStruct(x.shape, f32)] + [jax.ShapeDtypeStruct(x.shape, bf16)] * 3,
        compiler_params=_params("parallel"), name="cumsum",
    )(x)


def _attn_body(q_ref, k_ref, v_ref, cq_ref, o_ref, *, tq, tk, q_off):
    qi = pl.program_id(2)
    lane = lax.broadcasted_iota(jnp.int32, (tq, LANES), 1)
    row0 = q_off + qi * tq
    n_full = (row0 + 1) // tk
    n_all = (row0 + tq + tk - 1) // tk
    nh = ATTN_HEADS_PER_STEP
    qs = [q_ref[0, :, hh * LANES:(hh + 1) * LANES] for hh in range(nh)]
    cqs = [jnp.broadcast_to(cq_ref[0, 0, :, hh:hh + 1], (tq, LANES)) for hh in range(nh)]
    n_col = tk // LANES

    def step(t, carry, masked):
        start = pl.multiple_of(t * tk, tk)
        if masked:
            kpos = start + lax.broadcasted_iota(jnp.int32, (tq, tk), 1)
            qpos = row0 + lax.broadcasted_iota(jnp.int32, (tq, tk), 0)
            allowed = kpos <= qpos
        out = []
        for hh in range(nh):
            m, l, acc = carry[hh]
            ks = k_ref[0, pl.ds(start, tk), hh * LANES:(hh + 1) * LANES]
            vs = v_ref[0, pl.ds(start, tk), (hh // 2) * LANES:(hh // 2 + 1) * LANES]
            s = lax.dot_general(qs[hh], ks, (((1,), (1,)), ((), ())), preferred_element_type=f32)
            if masked:
                s = jnp.where(allowed, s, NEG_INF)
            cols = [s[:, j * LANES:(j + 1) * LANES] for j in range(n_col)]
            smax = functools.reduce(jnp.maximum, cols)
            m_new = jnp.maximum(m, jnp.max(smax, axis=1, keepdims=True) + cqs[hh])
            alpha = jnp.exp2(m - m_new)
            r = cqs[hh] - m_new
            ps = [jnp.exp2(col + r) for col in cols]
            l = alpha * l + functools.reduce(jnp.add, ps)
            p = jnp.concatenate([x.astype(bf16) for x in ps], axis=1)
            acc = alpha * acc + jnp.dot(p, vs, preferred_element_type=f32)
            out.append((m_new, l, acc))
        return tuple(out)

    one = (jnp.full((tq, LANES), NEG_INF, f32), jnp.zeros((tq, LANES), f32), jnp.zeros((tq, LANES), f32))
    carry = lax.fori_loop(0, n_full, functools.partial(step, masked=False), (one,) * nh)
    carry = lax.fori_loop(n_full, n_all, functools.partial(step, masked=True), carry)
    accs = [acc / jnp.sum(l, axis=1, keepdims=True) for _, l, acc in carry]
    for pr in range(nh // 2):
        o_ref[0, :, pr * LANES:(pr + 1) * LANES] = jnp.where(
            lane < HEAD_DIM, accs[2 * pr], accs[2 * pr + 1]).astype(o_ref.dtype)


def _attention(q, k, v, cq, *, tq, tk, q_off):
    b, lq, _ = q.shape
    lk = k.shape[1]
    nh = ATTN_HEADS_PER_STEP
    return pl.pallas_call(
        functools.partial(_attn_body, tq=tq, tk=tk, q_off=q_off),
        grid=(b, FOX_HEADS // nh, lq // tq),
        in_specs=[
            pl.BlockSpec((1, tq, nh * LANES), lambda b_, h_, i: (b_, i, h_)),
            pl.BlockSpec((1, lk, nh * LANES), lambda b_, h_, i: (b_, 0, h_)),
            pl.BlockSpec((1, lk, nh * HEAD_DIM), lambda b_, h_, i: (b_, 0, h_)),
            pl.BlockSpec((1, 1, tq, nh), lambda b_, h_, i: (b_, h_, i, 0)),
        ],
        out_specs=pl.BlockSpec((1, tq, nh * HEAD_DIM), lambda b_, h_, i: (b_, i, h_)),
        out_shape=jax.ShapeDtypeStruct((b, lq, FOX_WIDTH), bf16),
        compiler_params=_params("parallel", "parallel", "arbitrary"),
        name="fox_attention",
    )(q, k, v, cq)


def _merge_body(x_ref, oa_ref, u_ref, vn_ref, zp_ref, prev_ref, gate_ref, ws_ref, bs_ref, wpool_ref, pscale_ref,
                wbr_ref, wout_ref, o_ref, *, tn, lc, pos0):
    j = pl.program_id(1)
    d = x_ref.shape[-1]
    t_io = lax.broadcasted_iota(jnp.int32, (lc, lc), 0)
    s_io = lax.broadcasted_iota(jnp.int32, (lc, lc), 1)
    causal = (s_io // SGU_BLOCK) <= (t_io // SGU_BLOCK)
    grp = lax.broadcasted_iota(jnp.int32, (lc, GMLP_WIDTH), 1) // (GMLP_WIDTH // GMLP_GROUPS)
    wmask = [jnp.where(causal, ws_ref[g], 0.0).astype(bf16) for g in range(GMLP_GROUPS)]
    ob = []
    for c in range(tn // lc):
        vc = vn_ref[0, c * lc:(c + 1) * lc, :].astype(bf16)
        s = bs_ref[...]
        for g in range(GMLP_GROUPS):
            sg = jnp.dot(wmask[g], vc, preferred_element_type=f32)
            s = s + jnp.where(grp == g, sg, 0.0)
        ob.append(u_ref[0, c * lc:(c + 1) * lc, :] * s)
    o_b = ob[0] if len(ob) == 1 else jnp.concatenate(ob, axis=0)
    zp = zp_ref[0]
    ext = jnp.concatenate([prev_ref[0, 0], zp], axis=0)
    s2 = ext + pltpu.roll(ext, 1, 0)
    s4 = s2 + pltpu.roll(s2, 2, 0)
    s8 = s4 + pltpu.roll(s4, 4, 0)
    s16 = s8 + pltpu.roll(s8, 8, 0)
    pgrp = lax.broadcasted_iota(jnp.int32, (tn, POOL_WIDTH), 1) // (POOL_WIDTH // POOL_GROUPS)
    pos = pos0 + j * tn + lax.broadcasted_iota(jnp.int32, (tn, POOL_WIDTH), 0)
    win = jnp.where(pgrp == 0, s2[HALO:], jnp.where(pgrp == 1, s4[HALO:], jnp.where(pgrp == 2, s8[HALO:], s16[HALO:])))
    width = jnp.where(pgrp == 0, 2, jnp.where(pgrp == 1, 4, jnp.where(pgrp == 2, 8, 16)))
    cnt = jnp.minimum(pos + 1, width).astype(f32)
    dlt = win / cnt - zp
    o_c = jnp.dot(dlt.astype(bf16), wpool_ref[...], preferred_element_type=f32) * pscale_ref[...]
    ya = jnp.dot(oa_ref[0], wbr_ref[:FOX_WIDTH, :], preferred_element_type=f32)
    yb = jnp.dot(o_b.astype(bf16), wbr_ref[FOX_WIDTH:FOX_WIDTH + GMLP_WIDTH, :], preferred_element_type=f32)
    yc = jnp.dot(o_c.astype(bf16), wbr_ref[FOX_WIDTH + GMLP_WIDTH:, :], preferred_element_type=f32)
    merged = gate_ref[0, :, :d] * ya + gate_ref[0, :, d:2 * d] * yb + gate_ref[0, :, 2 * d:] * yc
    o_ref[0] = x_ref[0] + jnp.dot(merged.astype(bf16), wout_ref[...], preferred_element_type=f32)


def _merge(x, oa, u, vn, zp, prev, gate, w, l, *, tn, lc, pos0):
    nseq, L, d = x.shape
    seq = lambda width: pl.BlockSpec((1, tn, width), lambda b_, j: (b_, j, 0))
    weights = [w["ws"], w["bs"], w["wpool"], w["pscale"], w["wbr"], w["wout"]]
    return pl.pallas_call(
        functools.partial(_merge_body, tn=tn, lc=lc, pos0=pos0),
        grid=(nseq, L // tn),
        in_specs=[seq(d), seq(FOX_WIDTH), seq(GMLP_WIDTH), seq(GMLP_WIDTH), seq(POOL_WIDTH),
                  pl.BlockSpec((1, 1, HALO, POOL_WIDTH), lambda b_, j: (b_, j, 0, 0)), seq(N_BRANCH * d)]
        + [_layer_spec(a, l) for a in weights],
        out_specs=seq(d),
        out_shape=jax.ShapeDtypeStruct(x.shape, f32),
        compiler_params=_params("parallel", "parallel"),
        name="merge",
    )(x, oa, u, vn, zp, prev, gate, *weights)


def _top_values(block, k):
    tops = []
    for _ in range(k):
        m = jnp.max(block, axis=0, keepdims=True)
        tops.append(m)
        block = jnp.where(block == m, NEG_INF, block)
    return tops


def _route_body(x_ref, g2_ref, wqt_ref, keys_ref, ht_ref, sc_ref, st_ref, top_ref, *, tn):
    h = _rms(x_ref[...], g2_ref[...]).astype(bf16)
    ht = h.T
    ht_ref[...] = ht
    qt = jnp.dot(wqt_ref[...], ht, preferred_element_type=f32).astype(bf16)
    n_sets = 2 * PEER_HEADS
    for r in range(n_sets):
        sc_ref[r * N_KEYS:(r + 1) * N_KEYS, :] = jnp.dot(
            keys_ref[r], qt[r * PEER_HALF:(r + 1) * PEER_HALF, :], preferred_element_type=f32)

    kk = PEER_TOPK + 1

    def half_top(r, _):
        base = pl.multiple_of(r * N_KEYS, N_KEYS)
        for tb in range(tn // LANES):
            blk = sc_ref[pl.ds(base, N_KEYS), tb * LANES:(tb + 1) * LANES]
            tops = _top_values(blk, kk)
            top_ref[r, :, tb * LANES:(tb + 1) * LANES] = jnp.concatenate(
                tops + [tops[-1]] * (TOP_ROWS - kk), axis=0)
        return 0

    lax.fori_loop(0, n_sets, half_top, 0)

    def head_stats(hd, _):
        for tb in range(tn // LANES):
            sl = slice(tb * LANES, (tb + 1) * LANES)
            a = top_ref[2 * hd, :, sl]
            b = top_ref[2 * hd + 1, :, sl]
            rows8 = lax.broadcasted_iota(jnp.int32, (SUBLANES, LANES), 0)
            never = jnp.full((SUBLANES, LANES), NEG_INF, f32)
            blocks = [(a[0:1], b[0:SUBLANES]), (a[0:1], b[SUBLANES:2 * SUBLANES])]
            for p in range(1, SUBLANES):
                blocks.append((a[p:p + 1], jnp.where(rows8 < kk // (p + 1), b[0:SUBLANES], never)))
            blocks.append((a[SUBLANES:2 * SUBLANES], b[0:1]))
            blocks.append((jnp.where(rows8 == 0, a[PEER_TOPK:PEER_TOPK + 1], a[0:1]),
                           jnp.where(rows8 == 0, b[0:1], jnp.where(rows8 == 1, b[PEER_TOPK:PEER_TOPK + 1], never))))
            cand = jnp.concatenate([ab + bb for ab, bb in blocks], axis=0)
            tops = _top_values(cand, kk)
            tau = 0.5 * (tops[-2] + tops[-1])
            z = jnp.zeros((1, LANES), f32)
            for ab, bb in blocks:
                kept = bb >= tau - ab
                z = z + jnp.sum(jnp.where(kept, jnp.exp((ab - a[0:1]) + (bb - b[0:1])), 0.0), axis=0, keepdims=True)
            st_ref[hd, :, sl] = jnp.concatenate(
                [a[0:1], b[0:1], tau, 0.5 / z, jnp.zeros((SUBLANES - 4, LANES), f32)], axis=0)
        return 0

    lax.fori_loop(0, PEER_HEADS, head_stats, 0)


def _route(x, w, l, tn):
    n, d = x.shape
    n_sets = 2 * PEER_HEADS
    return pl.pallas_call(
        functools.partial(_route_body, tn=tn),
        grid=(n // tn,),
        in_specs=[pl.BlockSpec((tn, d), lambda i: (i, 0)), _layer_spec(w["g2"], l), _layer_spec(w["wqt"], l),
                  _layer_spec(w["keys"], l)],
        out_specs=[pl.BlockSpec((d, tn), lambda i: (0, i)), pl.BlockSpec((n_sets * N_KEYS, tn), lambda i: (0, i)),
                   pl.BlockSpec((PEER_HEADS, SUBLANES, tn), lambda i: (0, 0, i))],
        out_shape=[jax.ShapeDtypeStruct((d, n), bf16), jax.ShapeDtypeStruct((n_sets * N_KEYS, n), f32),
                   jax.ShapeDtypeStruct((PEER_HEADS, SUBLANES, n), f32)],
        scratch_shapes=[pltpu.VMEM((n_sets, TOP_ROWS, tn), f32)],
        compiler_params=_params("parallel"),
        name="peer_route",
    )(x, w["g2"], w["wqt"], w["keys"])


def _n_halves(tn):
    return 2 if tn % (2 * LANES) == 0 else 1


def _experts_body(x_ref, ht_ref, sc_ref, st_ref, pu_ref, pvc_ref, o_ref, q_ref, act0_ref, act1_ref, acc_ref, pvp_ref,
                  *, tn, ec):
    c = pl.program_id(1)
    n_sub = ec // N_KEYS
    n_half = _n_halves(tn)
    half = tn // n_half
    acts = (act0_ref, act1_ref)

    @pl.when(c == 0)
    def _():
        for hd in range(PEER_HEADS):
            s2 = sc_ref[(2 * hd + 1) * N_KEYS:(2 * hd + 2) * N_KEYS, :]
            q_ref[hd] = (jnp.exp(s2 - st_ref[hd, 1:2, :]) * st_ref[hd, 3:4, :]).astype(bf16)
        acc_ref[...] = jnp.zeros_like(acc_ref)
        acts[n_half - 1][...] = jnp.zeros_like(acts[n_half - 1])
        pvp_ref[...] = jnp.zeros_like(pvp_ref)

    i0 = pl.multiple_of(c * n_sub, n_sub)

    def hidden(hf):
        return jnp.dot(pu_ref[...], ht_ref[:, hf * half:(hf + 1) * half], preferred_element_type=f32)

    def values(pv_ref, hf):
        acc_ref[:, hf * half:(hf + 1) * half] += jnp.dot(pv_ref[...], acts[hf][...], preferred_element_type=f32)

    def weigh(hf, a):
        for tb in range(half // LANES):
            lanes = slice(hf * half + tb * LANES, hf * half + (tb + 1) * LANES)
            thr_rows, p_rows = [], []
            for hd in range(PEER_HEADS):
                s1 = sc_ref[pl.ds(2 * hd * N_KEYS + i0, n_sub), lanes]
                thr_rows.append(st_ref[hd, 2:3, lanes] - s1)
                p_rows.append(jnp.exp(s1 - st_ref[hd, 0:1, lanes]))
            for ii in range(n_sub):
                gsum = jnp.zeros((N_KEYS, LANES), bf16)
                for hd in range(PEER_HEADS):
                    s2 = sc_ref[(2 * hd + 1) * N_KEYS:(2 * hd + 2) * N_KEYS, lanes]
                    kept = s2 >= thr_rows[hd][ii:ii + 1]
                    prow = jnp.broadcast_to(p_rows[hd][ii:ii + 1], (N_KEYS, LANES)).astype(bf16)
                    gsum = gsum + jnp.where(kept, q_ref[hd, :, lanes], jnp.zeros((), bf16)) * prow
                aa = a[ii * N_KEYS:(ii + 1) * N_KEYS, tb * LANES:(tb + 1) * LANES]
                acts[hf][ii * N_KEYS:(ii + 1) * N_KEYS, tb * LANES:(tb + 1) * LANES] = (
                    (aa + aa * lax.erf(aa * 0.7071067811865476)) * gsum).astype(bf16)

    a0 = hidden(0)
    values(pvp_ref, n_half - 1)
    weigh(0, a0)
    if n_half == 2:
        a1 = hidden(1)
        values(pvc_ref, 0)
        weigh(1, a1)
    pvp_ref[...] = pvc_ref[...]

    @pl.when(c == pl.num_programs(1) - 1)
    def _():
        values(pvc_ref, n_half - 1)
        o_ref[...] = x_ref[...] + acc_ref[...].T


def _experts(x, ht, sc, st, w, l, *, tn, ec):
    n, d = x.shape
    n_exp = w["pu"].shape[1]
    n_sets = 2 * PEER_HEADS
    return pl.pallas_call(
        functools.partial(_experts_body, tn=tn, ec=ec),
        grid=(n // tn, n_exp // ec),
        in_specs=[
            pl.BlockSpec((tn, d), lambda t, c: (t, 0)),
            pl.BlockSpec((d, tn), lambda t, c: (0, t)),
            pl.BlockSpec((n_sets * N_KEYS, tn), lambda t, c: (0, t)),
            pl.BlockSpec((PEER_HEADS, SUBLANES, tn), lambda t, c: (0, 0, t)),
            pl.BlockSpec((None, ec, d), lambda t, c: (l, c, 0)),
            pl.BlockSpec((None, d, ec), lambda t, c: (l, 0, c)),
        ],
        out_specs=pl.BlockSpec((tn, d), lambda t, c: (t, 0)),
        out_shape=jax.ShapeDtypeStruct((n, d), f32),
        scratch_shapes=[
            pltpu.VMEM((PEER_HEADS, N_KEYS, tn), bf16),
            pltpu.VMEM((ec, tn // _n_halves(tn)), bf16), pltpu.VMEM((ec, tn // _n_halves(tn)), bf16),
            pltpu.VMEM((d, tn), f32), pltpu.VMEM((d, ec), bf16),
        ],
        compiler_params=_params("parallel", "arbitrary"),
        name="peer_experts",
    )(x, ht, sc, st, w["pu"], w["pvt"])


def _final_body(x_ref, g_ref, o_ref):
    o_ref[...] = _rms(x_ref[...], g_ref[...])


def _final_norm(x, g, tn):
    n, d = x.shape
    spec = pl.BlockSpec((tn, d), lambda i: (i, 0))
    return pl.pallas_call(
        _final_body, grid=(n // tn,), in_specs=[spec, _const_spec(g.shape)], out_specs=spec,
        out_shape=jax.ShapeDtypeStruct(x.shape, f32), compiler_params=_params("parallel"), name="final_norm",
    )(x, g)


def _gating_weights(w_s, b_s, lc):
    gdim = GMLP_WIDTH // GMLP_GROUPS
    return w_s[:, :, :lc, :lc], jnp.repeat(jnp.transpose(b_s[:, :, :lc], (0, 2, 1)), gdim, axis=2)


def _all_weights(norm1_g, w_in, b_f, ln_g, ln_b, w_s, b_s, w_pool, pool_scale, w_br, w_out, norm2_g, peer_wq,
                 peer_keys, peer_u, peer_v):
    depth, d = w_in.shape[:2]
    o_f = 3 * FOX_WIDTH
    o_u = o_f + FOX_HEADS
    o_g = o_u + 2 * GMLP_WIDTH + POOL_WIDTH
    wf = jnp.zeros((depth, d, LANES), f32).at[:, :, :FOX_HEADS].set(w_in[:, :, o_f:o_u])
    bf = jnp.zeros((depth, 1, LANES), f32).at[:, 0, :FOX_HEADS].set(b_f)
    pdim = POOL_WIDTH // POOL_GROUPS
    wpool = jnp.zeros((depth, POOL_WIDTH, POOL_WIDTH), f32)
    for g in range(POOL_GROUPS):
        wpool = wpool.at[:, g * pdim:(g + 1) * pdim, g * pdim:(g + 1) * pdim].set(w_pool[:, g])
    spare = LANES - HEAD_DIM

    def head_lanes(cols):
        cols = cols.reshape(depth, d, FOX_HEADS, HEAD_DIM)
        return jnp.pad(cols, ((0, 0), (0, 0), (0, 0), (0, spare))).reshape(depth, d, ATTN_WIDTH)

    qc = jnp.zeros((FOX_HEADS, LANES), f32).at[:, HEAD_DIM:HEAD_DIM + C_PIECES].set(-1.0).reshape(1, 1, ATTN_WIDTH)
    wqk = jnp.concatenate([head_lanes(w_in[:, :, :FOX_WIDTH]), head_lanes(w_in[:, :, FOX_WIDTH:2 * FOX_WIDTH])], axis=2)
    ws, bs = _gating_weights(w_s, b_s, GMLP_CHUNK)
    return {
        "g1": norm1_g[:, None], "wqk": wqk.astype(bf16), "qc": jnp.broadcast_to(qc, (depth, 1, ATTN_WIDTH)),
        "wkv": w_in[:, :, FOX_WIDTH:o_f].astype(bf16), "wf": wf.astype(bf16), "bf": bf,
        "wugp": w_in[:, :, o_u:o_g].astype(bf16), "lng": ln_g[:, None], "lnb": ln_b[:, None],
        "wg": w_in[:, :, o_g:].astype(bf16),
        "ws": ws, "bs": bs, "wpool": wpool.astype(bf16), "pscale": pool_scale[:, None],
        "wbr": w_br.astype(bf16), "wout": w_out.astype(bf16),
        "g2": norm2_g[:, None], "wqt": jnp.transpose(peer_wq, (0, 2, 1)).astype(bf16),
        "keys": peer_keys.reshape(depth, 2 * PEER_HEADS, N_KEYS, PEER_HALF).astype(bf16),
        "pu": peer_u.astype(bf16), "pvt": jnp.transpose(peer_v, (0, 2, 1)).astype(bf16),
    }


def _pick(n, prefs):
    for t in prefs:
        if n % t == 0:
            return t
    return n


def _layer(x, w, l, *, k_hist, v_hist, logf_hist, pool_hist, lc, tiles):
    nseq, L, d = x.shape
    n = nseq * L
    q, k, v, kb, vb, logf, u, vn, zp, gate = _inproj(x.reshape(n, d), w, l, tiles["inproj"])
    r3 = lambda a: a.reshape(nseq, L, a.shape[-1])
    q, kb, vb, logf3, u, vn, zp, gate = map(r3, (q, kb, vb, logf, u, vn, zp, gate))
    past = 0 if k_hist is None else k_hist.shape[1]
    spare = LANES - HEAD_DIM
    if past:
        k_hist = jnp.pad(k_hist.astype(bf16), ((0, 0), (0, 0), (0, 0), (0, spare)))
        kb = jnp.concatenate([k_hist.reshape(nseq, past, ATTN_WIDTH), kb], axis=1)
        vb = jnp.concatenate([v_hist.reshape(nseq, past, FOX_WIDTH).astype(bf16), vb], axis=1)
        lf_all = jnp.concatenate([logf_hist.astype(f32), logf3], axis=1)
    else:
        lf_all = logf3
    lk = past + L
    tk = tiles["tk"]
    lk_pad = -(-lk // tk) * tk
    if lk_pad != lk:
        padk = ((0, 0), (0, lk_pad - lk), (0, 0))
        kb, vb, lf_all = jnp.pad(kb, padk), jnp.pad(vb, padk), jnp.pad(lf_all, padk)
    nh = ATTN_HEADS_PER_STEP
    c_t, c_hi, c_mid, c_lo = _cumsum_lanes(jnp.transpose(lf_all, (0, 2, 1)))
    cq = jnp.transpose(c_t.reshape(nseq, FOX_HEADS // nh, nh, lk_pad)[:, :, :, past:past + L], (0, 1, 3, 2))
    pieces = jnp.transpose(jnp.stack([c_hi, c_mid, c_lo], axis=-1), (0, 2, 1, 3))
    pieces = jnp.pad(pieces, ((0, 0), (0, 0), (0, 0), (HEAD_DIM, spare - C_PIECES)))
    k_aug = kb + pieces.reshape(nseq, lk_pad, ATTN_WIDTH)
    o_a = _attention(q, k_aug, vb, cq, tq=tiles["tq"], tk=tk, q_off=past)
    tn = tiles["merge"]
    nt = L // tn
    first = jnp.zeros((nseq, HALO, POOL_WIDTH), f32)
    if pool_hist is not None:
        first = first.at[:, HALO - POOL_HIST:].set(pool_hist.astype(f32))
    tails = zp.reshape(nseq, nt, tn, POOL_WIDTH)[:, :nt - 1, tn - HALO:, :]
    prev = jnp.concatenate([first[:, None], tails], axis=1)
    x_mid = _merge(x, o_a, u, vn, zp, prev, gate, w, l, tn=tn, lc=lc, pos0=past)
    ht, sc, st = _route(x_mid.reshape(n, d), w, l, tiles["route"])
    x_new = _experts(x_mid.reshape(n, d), ht, sc, st, w, l, tn=tiles["experts"], ec=tiles["ec"])
    new_pool = zp[:, L - POOL_HIST:]
    return x_new.reshape(nseq, L, d), r3(k), r3(v), logf3, new_pool, vn


def kernel(x_prompt, x_sample, cache_fox_k, cache_fox_v, cache_fox_logf, state_pool, norm1_g, w_in, b_f, ln_g, ln_b,
           w_s, b_s, w_pool, pool_scale, w_br, w_out, norm2_g, peer_wq, peer_keys, peer_u, peer_v, final_g):
    depth = w_in.shape[0]
    bp, sp, d = x_prompt.shape
    bd, ls, _ = x_sample.shape
    assert sp % GMLP_CHUNK == 0 and ls <= GMLP_CHUNK and ls >= POOL_HIST and ls % 16 == 0
    params = (norm1_g, w_in, b_f, ln_g, ln_b, w_s, b_s, w_pool, pool_scale, w_br, w_out, norm2_g, peer_wq, peer_keys,
              peer_u, peer_v)
    tiles_p = {"inproj": _pick(bp * sp, (256,)), "tq": _pick(sp, (512, 256, 128)), "tk": _pick(sp, (512, 256, 128)),
               "merge": _pick(sp, (256, 128)), "route": _pick(bp * sp, (512, 256, 128)),
               "experts": _pick(bp * sp, (512, 256, 128)), "ec": EXPERT_CHUNK}
    ns = bd * ls
    tiles_s = {"inproj": ns, "tq": ls, "tk": 128, "merge": ls, "route": ns, "experts": ns, "ec": EXPERT_CHUNK}
    xp, xs = x_prompt, x_sample
    outs_p, outs_s = [], []
    wp = _all_weights(*params)
    wsm = dict(wp)
    wsm["ws"], wsm["bs"] = _gating_weights(w_s, b_s, ls)
    for l in range(depth):
        xp, k, v, lf, pool, _ = _layer(xp, wp, l, k_hist=None, v_hist=None, logf_hist=None, pool_hist=None,
                                       lc=GMLP_CHUNK, tiles=tiles_p)
        outs_p.append((k.reshape(bp, sp, FOX_HEADS, HEAD_DIM), v.reshape(bp, sp, FOX_HEADS, HEAD_DIM), lf, pool))
        xs, k, v, lf, pool, vn = _layer(xs, wsm, l, k_hist=cache_fox_k[l], v_hist=cache_fox_v[l],
                                        logf_hist=cache_fox_logf[l], pool_hist=state_pool[l], lc=ls, tiles=tiles_s)
        outs_s.append((k.reshape(bd, ls, FOX_HEADS, HEAD_DIM), v.reshape(bd, ls, FOX_HEADS, HEAD_DIM), lf, pool, vn))
    g = final_g[None]
    y_prompt = _final_norm(xp.reshape(bp * sp, d), g, _pick(bp * sp, (512, 256, 128))).reshape(bp, sp, d)
    y_sample = _final_norm(xs.reshape(ns, d), g, ns).reshape(bd, ls, d)
    stack = lambda outs, i: jnp.stack([o[i] for o in outs])
    return (y_prompt, y_sample, stack(outs_p, 0), stack(outs_p, 1), stack(outs_p, 2), stack(outs_p, 3),
            stack(outs_s, 0), stack(outs_s, 1), stack(outs_s, 2), stack(outs_s, 3), stack(outs_s, 4))
```

```python
import functools

import jax
import jax.numpy as jnp
from jax import lax
from jax.experimental import pallas as pl
from jax.experimental.pallas import tpu as pltpu

f32 = jnp.float32
bf16 = jnp.bfloat16

EPS = 1e-6
FOX_HEADS = 8
HEAD_DIM = 64
FOX_WIDTH = FOX_HEADS * HEAD_DIM
GMLP_GROUPS = 4
GMLP_WIDTH = 256
GMLP_CHUNK = 128
SGU_BLOCK = 64
POOL_GROUPS = 4
POOL_WIDTH = 256
POOL_WINDOWS = (2, 4, 8, 16)
POOL_HIST = 15
HALO = 16
N_BRANCH = 3
PEER_HEADS = 8
N_KEYS = 128
PEER_TOPK = 16
PEER_HALF = 128

LANES = 128
SUBLANES = 8
EXPERT_CHUNK = SUBLANES * N_KEYS
TOP_ROWS = 3 * SUBLANES
VMEM_LIMIT = 56 * 1024 * 1024
NEG_INF = float("-inf")
LOG2E = 1.4426950408889634
C_PIECES = 3
ATTN_HEADS_PER_STEP = 4
ATTN_WIDTH = FOX_HEADS * LANES


def _params(*sem):
    return pltpu.CompilerParams(dimension_semantics=sem, vmem_limit_bytes=VMEM_LIMIT)


def _const_spec(shape):
    nd = len(shape)
    return pl.BlockSpec(shape, lambda *_: (0,) * nd)


def _layer_spec(a, l):
    nd = a.ndim - 1
    return pl.BlockSpec((None,) + a.shape[1:], lambda *_: (l,) + (0,) * nd)


def _rms(x, g):
    return x * lax.rsqrt(jnp.mean(x * x, axis=-1, keepdims=True) + EPS) * g


def _gelu(x):
    return 0.5 * x * (1.0 + lax.erf(x * 0.7071067811865476))


def _inproj_body(x_ref, g1_ref, wqk_ref, qc_ref, wkv_ref, wf_ref, bf_ref, wugp_ref, lng_ref, lnb_ref, wg_ref,
                 q_ref, k_ref, v_ref, kb_ref, vb_ref, lf_ref, u_ref, vn_ref, zp_ref, gate_ref):
    h = _rms(x_ref[...], g1_ref[...]).astype(bf16)
    zqk = jnp.dot(h, wqk_ref[...], preferred_element_type=f32)
    q_ref[...] = (zqk[:, :ATTN_WIDTH] * (HEAD_DIM ** -0.5 * LOG2E) + qc_ref[...]).astype(bf16)
    kb_ref[...] = zqk[:, ATTN_WIDTH:].astype(bf16)
    zkv = jnp.dot(h, wkv_ref[...], preferred_element_type=f32)
    v = zkv[:, FOX_WIDTH:]
    k_ref[...] = zkv[:, :FOX_WIDTH]
    v_ref[...] = v
    vb_ref[...] = v.astype(bf16)
    zf = jnp.dot(h, wf_ref[...], preferred_element_type=f32) + bf_ref[...]
    lf_ref[...] = jax.nn.log_sigmoid(zf)[:, :FOX_HEADS]
    zugp = jnp.dot(h, wugp_ref[...], preferred_element_type=f32)
    u_ref[...] = _gelu(zugp[:, :GMLP_WIDTH])
    gv = _gelu(zugp[:, GMLP_WIDTH:2 * GMLP_WIDTH])
    mu = jnp.mean(gv, axis=-1, keepdims=True)
    gc = gv - mu
    vn_ref[...] = gc * lax.rsqrt(jnp.mean(gc * gc, axis=-1, keepdims=True) + EPS) * lng_ref[...] + lnb_ref[...]
    zp_ref[...] = zugp[:, 2 * GMLP_WIDTH:]
    zg = jnp.dot(h, wg_ref[...], preferred_element_type=f32)
    gate_ref[...] = 1.0 / (1.0 + jnp.exp(-zg))


def _inproj(x, w, l, tn):
    n, d = x.shape
    row = lambda width: pl.BlockSpec((tn, width), lambda i: (i, 0))
    outs = [
        (ATTN_WIDTH, bf16), (FOX_WIDTH, f32), (FOX_WIDTH, f32), (ATTN_WIDTH, bf16), (FOX_WIDTH, bf16),
        (FOX_HEADS, f32), (GMLP_WIDTH, f32), (GMLP_WIDTH, f32), (POOL_WIDTH, f32), (N_BRANCH * d, f32),
    ]
    weights = [w["g1"], w["wqk"], w["qc"], w["wkv"], w["wf"], w["bf"], w["wugp"], w["lng"], w["lnb"], w["wg"]]
    return pl.pallas_call(
        _inproj_body,
        grid=(n // tn,),
        in_specs=[row(d)] + [_layer_spec(a, l) for a in weights],
        out_specs=[row(width) for width, _ in outs],
        out_shape=[jax.ShapeDtypeStruct((n, width), dt) for width, dt in outs],
        compiler_params=_params("parallel"),
        name="inproj",
    )(x, *weights)


def _cumsum_body(x_ref, c_ref, hi_ref, mid_ref, lo_ref):
    x = x_ref[0]
    n = x.shape[-1]
    lane = lax.broadcasted_iota(jnp.int32, x.shape, 1)
    shift = 1
    while shift < n:
        x = x + jnp.where(lane >= shift, pltpu.roll(x, shift, 1), 0.0)
        shift *= 2
    c = x * LOG2E
    c_ref[0] = c
    hi = c.astype(bf16)
    r1 = c - hi.astype(f32)
    mid = r1.astype(bf16)
    hi_ref[0] = hi
    mid_ref[0] = mid
    lo_ref[0] = (r1 - mid.astype(f32)).astype(bf16)


def _cumsum_lanes(x):
    b, r, n = x.shape
    spec = pl.BlockSpec((1, r, n), lambda i: (i, 0, 0))
    return pl.pallas_call(
        _cumsum_body, grid=(b,), in_specs=[spec], out_specs=[spec] * 4,
        out_shape=[jax.ShapeDtypeStruct(x.shape, f32)] + [jax.ShapeDtypeStruct(x.shape, bf16)] * 3,
        compiler_params=_params("parallel"), name="cumsum",
    )(x)


def _attn_body(q_ref, k_ref, pc_ref, place_ref, v_ref, cq_ref, o_ref, ka_ref, *, tq, tk, q_off):
    qi = pl.program_id(2)
    lane = lax.broadcasted_iota(jnp.int32, (tq, LANES), 1)
    row0 = q_off + qi * tq
    n_full = (row0 + 1) // tk
    n_all = (row0 + tq + tk - 1) // tk
    nh = ATTN_HEADS_PER_STEP
    lk = k_ref.shape[1]

    @pl.when(qi == 0)
    def _():
        def place(t, _):
            rows = pl.ds(pl.multiple_of(t * tk, tk), tk)
            moved = jnp.dot(pc_ref[0, rows, :], place_ref[...], preferred_element_type=f32)
            ka_ref[rows, :] = (k_ref[0, rows, :].astype(f32) + moved).astype(bf16)
            return 0

        lax.fori_loop(0, lk // tk, place, 0)

    qs = [q_ref[0, :, hh * LANES:(hh + 1) * LANES] for hh in range(nh)]
    cqs = [jnp.broadcast_to(cq_ref[0, 0, :, hh:hh + 1], (tq, LANES)) for hh in range(nh)]
    n_col = tk // LANES
    lane_k = lax.broadcasted_iota(jnp.int32, (tk, LANES), 1)

    def step(t, carry, masked):
        start = pl.multiple_of(t * tk, tk)
        if masked:
            kpos = start + lax.broadcasted_iota(jnp.int32, (tq, tk), 1)
            qpos = row0 + lax.broadcasted_iota(jnp.int32, (tq, tk), 0)
            allowed = kpos <= qpos
        out = []
        for hh in range(nh):
            m, acc = carry[hh]
            ks = ka_ref[pl.ds(start, tk), hh * LANES:(hh + 1) * LANES]
            vs = v_ref[0, pl.ds(start, tk), (hh // 2) * LANES:(hh // 2 + 1) * LANES]
            own = (lane_k < HEAD_DIM) if hh % 2 == 0 else (lane_k >= HEAD_DIM)
            vs = jnp.where(own, vs, jnp.ones((), bf16))
            s = lax.dot_general(qs[hh], ks, (((1,), (1,)), ((), ())), preferred_element_type=f32)
            if masked:
                s = jnp.where(allowed, s, NEG_INF)
            cols = [s[:, j * LANES:(j + 1) * LANES] for j in range(n_col)]
            smax = functools.reduce(jnp.maximum, cols)
            m_new = jnp.maximum(m, jnp.max(smax, axis=1, keepdims=True) + cqs[hh])
            alpha = jnp.exp2(m - m_new)
            r = cqs[hh] - m_new
            p = jnp.concatenate([jnp.exp2(col + r).astype(bf16) for col in cols], axis=1)
            acc = alpha * acc + jnp.dot(p, vs, preferred_element_type=f32)
            out.append((m_new, acc))
        return tuple(out)

    one = (jnp.full((tq, LANES), NEG_INF, f32), jnp.zeros((tq, LANES), f32))
    carry = lax.fori_loop(0, n_full, functools.partial(step, masked=False), (one,) * nh)
    carry = lax.fori_loop(n_full, n_all, functools.partial(step, masked=True), carry)
    outs = []
    for hh, (_, acc) in enumerate(carry):
        sums = (lane >= HEAD_DIM) if hh % 2 == 0 else (lane < HEAD_DIM)
        outs.append(acc / jnp.max(jnp.where(sums, acc, 0.0), axis=1, keepdims=True))
    for pr in range(nh // 2):
        o_ref[0, :, pr * LANES:(pr + 1) * LANES] = jnp.where(
            lane < HEAD_DIM, outs[2 * pr], outs[2 * pr + 1]).astype(o_ref.dtype)


def _attention(q, k, pieces, place, v, cq, *, tq, tk, q_off):
    b, lq, _ = q.shape
    lk = k.shape[1]
    nh = ATTN_HEADS_PER_STEP
    return pl.pallas_call(
        functools.partial(_attn_body, tq=tq, tk=tk, q_off=q_off),
        grid=(b, FOX_HEADS // nh, lq // tq),
        in_specs=[
            pl.BlockSpec((1, tq, nh * LANES), lambda b_, h_, i: (b_, i, h_)),
            pl.BlockSpec((1, lk, nh * LANES), lambda b_, h_, i: (b_, 0, h_)),
            pl.BlockSpec((1, lk, LANES), lambda b_, h_, i: (b_, 0, h_)),
            _const_spec(place.shape),
            pl.BlockSpec((1, lk, nh * HEAD_DIM), lambda b_, h_, i: (b_, 0, h_)),
            pl.BlockSpec((1, 1, tq, nh), lambda b_, h_, i: (b_, h_, i, 0)),
        ],
        out_specs=pl.BlockSpec((1, tq, nh * HEAD_DIM), lambda b_, h_, i: (b_, i, h_)),
        out_shape=jax.ShapeDtypeStruct((b, lq, FOX_WIDTH), bf16),
        scratch_shapes=[pltpu.VMEM((lk, nh * LANES), bf16)],
        compiler_params=_params("parallel", "parallel", "arbitrary"),
        name="fox_attention",
    )(q, k, pieces, place, v, cq)


def _merge_body(x_ref, oa_ref, u_ref, vn_ref, zp_ref, prev_ref, gate_ref, ws_ref, bs_ref, wpool_ref, pscale_ref,
                wbr_ref, wout_ref, o_ref, *, tn, lc, pos0):
    j = pl.program_id(1)
    d = x_ref.shape[-1]
    t_io = lax.broadcasted_iota(jnp.int32, (lc, lc), 0)
    s_io = lax.broadcasted_iota(jnp.int32, (lc, lc), 1)
    causal = (s_io // SGU_BLOCK) <= (t_io // SGU_BLOCK)
    grp = lax.broadcasted_iota(jnp.int32, (lc, GMLP_WIDTH), 1) // (GMLP_WIDTH // GMLP_GROUPS)
    wmask = [jnp.where(causal, ws_ref[g], 0.0).astype(bf16) for g in range(GMLP_GROUPS)]
    ob = []
    for c in range(tn // lc):
        vc = vn_ref[0, c * lc:(c + 1) * lc, :].astype(bf16)
        s = bs_ref[...]
        for g in range(GMLP_GROUPS):
            sg = jnp.dot(wmask[g], vc, preferred_element_type=f32)
            s = s + jnp.where(grp == g, sg, 0.0)
        ob.append(u_ref[0, c * lc:(c + 1) * lc, :] * s)
    o_b = ob[0] if len(ob) == 1 else jnp.concatenate(ob, axis=0)
    zp = zp_ref[0]
    ext = jnp.concatenate([prev_ref[0, 0], zp], axis=0)
    s2 = ext + pltpu.roll(ext, 1, 0)
    s4 = s2 + pltpu.roll(s2, 2, 0)
    s8 = s4 + pltpu.roll(s4, 4, 0)
    s16 = s8 + pltpu.roll(s8, 8, 0)
    pgrp = lax.broadcasted_iota(jnp.int32, (tn, POOL_WIDTH), 1) // (POOL_WIDTH // POOL_GROUPS)
    pos = pos0 + j * tn + lax.broadcasted_iota(jnp.int32, (tn, POOL_WIDTH), 0)
    win = jnp.where(pgrp == 0, s2[HALO:], jnp.where(pgrp == 1, s4[HALO:], jnp.where(pgrp == 2, s8[HALO:], s16[HALO:])))
    width = jnp.where(pgrp == 0, 2, jnp.where(pgrp == 1, 4, jnp.where(pgrp == 2, 8, 16)))
    cnt = jnp.minimum(pos + 1, width).astype(f32)
    dlt = win / cnt - zp
    o_c = jnp.dot(dlt.astype(bf16), wpool_ref[...], preferred_element_type=f32) * pscale_ref[...]
    ya = jnp.dot(oa_ref[0], wbr_ref[:FOX_WIDTH, :], preferred_element_type=f32)
    yb = jnp.dot(o_b.astype(bf16), wbr_ref[FOX_WIDTH:FOX_WIDTH + GMLP_WIDTH, :], preferred_element_type=f32)
    yc = jnp.dot(o_c.astype(bf16), wbr_ref[FOX_WIDTH + GMLP_WIDTH:, :], preferred_element_type=f32)
    merged = gate_ref[0, :, :d] * ya + gate_ref[0, :, d:2 * d] * yb + gate_ref[0, :, 2 * d:] * yc
    o_ref[0] = x_ref[0] + jnp.dot(merged.astype(bf16), wout_ref[...], preferred_element_type=f32)


def _merge(x, oa, u, vn, zp, prev, gate, w, l, *, tn, lc, pos0):
    nseq, L, d = x.shape
    seq = lambda width: pl.BlockSpec((1, tn, width), lambda b_, j: (b_, j, 0))
    weights = [w["ws"], w["bs"], w["wpool"], w["pscale"], w["wbr"], w["wout"]]
    return pl.pallas_call(
        functools.partial(_merge_body, tn=tn, lc=lc, pos0=pos0),
        grid=(nseq, L // tn),
        in_specs=[seq(d), seq(FOX_WIDTH), seq(GMLP_WIDTH), seq(GMLP_WIDTH), seq(POOL_WIDTH),
                  pl.BlockSpec((1, 1, HALO, POOL_WIDTH), lambda b_, j: (b_, j, 0, 0)), seq(N_BRANCH * d)]
        + [_layer_spec(a, l) for a in weights],
        out_specs=seq(d),
        out_shape=jax.ShapeDtypeStruct(x.shape, f32),
        compiler_params=_params("parallel", "parallel"),
        name="merge",
    )(x, oa, u, vn, zp, prev, gate, *weights)


def _top_values(block, k):
    tops = []
    for _ in range(k):
        m = jnp.max(block, axis=0, keepdims=True)
        tops.append(m)
        block = jnp.where(block == m, NEG_INF, block)
    return tops


def _route_body(x_ref, g2_ref, wqt_ref, keys_ref, ht_ref, sc_ref, st_ref, top_ref, *, tn):
    h = _rms(x_ref[...], g2_ref[...]).astype(bf16)
    ht = h.T
    ht_ref[...] = ht
    qt = jnp.dot(wqt_ref[...], ht, preferred_element_type=f32).astype(bf16)
    n_sets = 2 * PEER_HEADS
    for r in range(n_sets):
        sc_ref[r * N_KEYS:(r + 1) * N_KEYS, :] = jnp.dot(
            keys_ref[r], qt[r * PEER_HALF:(r + 1) * PEER_HALF, :], preferred_element_type=f32)

    kk = PEER_TOPK + 1

    def half_top(r, _):
        base = pl.multiple_of(r * N_KEYS, N_KEYS)
        for tb in range(tn // LANES):
            blk = sc_ref[pl.ds(base, N_KEYS), tb * LANES:(tb + 1) * LANES]
            tops = _top_values(blk, kk)
            top_ref[r, :, tb * LANES:(tb + 1) * LANES] = jnp.concatenate(
                tops + [tops[-1]] * (TOP_ROWS - kk), axis=0)
        return 0

    lax.fori_loop(0, n_sets, half_top, 0)

    def head_stats(hd, _):
        for tb in range(tn // LANES):
            sl = slice(tb * LANES, (tb + 1) * LANES)
            a = top_ref[2 * hd, :, sl]
            b = top_ref[2 * hd + 1, :, sl]
            rows8 = lax.broadcasted_iota(jnp.int32, (SUBLANES, LANES), 0)
            never = jnp.full((SUBLANES, LANES), NEG_INF, f32)
            blocks = [(a[0:1], b[0:SUBLANES]), (a[0:1], b[SUBLANES:2 * SUBLANES])]
            for p in range(1, SUBLANES):
                blocks.append((a[p:p + 1], jnp.where(rows8 < kk // (p + 1), b[0:SUBLANES], never)))
            blocks.append((a[SUBLANES:2 * SUBLANES], b[0:1]))
            blocks.append((jnp.where(rows8 == 0, a[PEER_TOPK:PEER_TOPK + 1], a[0:1]),
                           jnp.where(rows8 == 0, b[0:1], jnp.where(rows8 == 1, b[PEER_TOPK:PEER_TOPK + 1], never))))
            cand = jnp.concatenate([ab + bb for ab, bb in blocks], axis=0)
            tops = _top_values(cand, kk)
            tau = 0.5 * (tops[-2] + tops[-1])
            z = jnp.zeros((1, LANES), f32)
            for ab, bb in blocks:
                kept = bb >= tau - ab
                z = z + jnp.sum(jnp.where(kept, jnp.exp((ab - a[0:1]) + (bb - b[0:1])), 0.0), axis=0, keepdims=True)
            st_ref[hd, :, sl] = jnp.concatenate(
                [a[0:1], b[0:1], tau, 0.5 / z, jnp.zeros((SUBLANES - 4, LANES), f32)], axis=0)
        return 0

    lax.fori_loop(0, PEER_HEADS, head_stats, 0)


def _route(x, w, l, tn):
    n, d = x.shape
    n_sets = 2 * PEER_HEADS
    return pl.pallas_call(
        functools.partial(_route_body, tn=tn),
        grid=(n // tn,),
        in_specs=[pl.BlockSpec((tn, d), lambda i: (i, 0)), _layer_spec(w["g2"], l), _layer_spec(w["wqt"], l),
                  _layer_spec(w["keys"], l)],
        out_specs=[pl.BlockSpec((d, tn), lambda i: (0, i)), pl.BlockSpec((n_sets * N_KEYS, tn), lambda i: (0, i)),
                   pl.BlockSpec((PEER_HEADS, SUBLANES, tn), lambda i: (0, 0, i))],
        out_shape=[jax.ShapeDtypeStruct((d, n), bf16), jax.ShapeDtypeStruct((n_sets * N_KEYS, n), f32),
                   jax.ShapeDtypeStruct((PEER_HEADS, SUBLANES, n), f32)],
        scratch_shapes=[pltpu.VMEM((n_sets, TOP_ROWS, tn), f32)],
        compiler_params=_params("parallel"),
        name="peer_route",
    )(x, w["g2"], w["wqt"], w["keys"])


def _n_halves(tn):
    return 2 if tn % (2 * LANES) == 0 else 1


def _experts_body(x_ref, ht_ref, sc_ref, st_ref, pu_ref, pvc_ref, o_ref, q_ref, act0_ref, act1_ref, acc_ref, pvp_ref,
                  *, tn, ec):
    c = pl.program_id(1)
    n_sub = ec // N_KEYS
    n_half = _n_halves(tn)
    half = tn // n_half
    acts = (act0_ref, act1_ref)

    @pl.when(c == 0)
    def _():
        for hd in range(PEER_HEADS):
            s2 = sc_ref[(2 * hd + 1) * N_KEYS:(2 * hd + 2) * N_KEYS, :]
            q_ref[hd] = jnp.exp(s2 - st_ref[hd, 1:2, :]) * st_ref[hd, 3:4, :]
        acc_ref[...] = jnp.zeros_like(acc_ref)
        acts[n_half - 1][...] = jnp.zeros_like(acts[n_half - 1])
        pvp_ref[...] = jnp.zeros_like(pvp_ref)

    i0 = pl.multiple_of(c * n_sub, n_sub)

    def hidden(hf):
        return jnp.dot(pu_ref[...], ht_ref[:, hf * half:(hf + 1) * half], preferred_element_type=f32)

    def values(pv_ref, hf):
        acc_ref[:, hf * half:(hf + 1) * half] += jnp.dot(pv_ref[...], acts[hf][...], preferred_element_type=f32)

    def weigh(hf, a):
        for tb in range(half // LANES):
            lanes = slice(hf * half + tb * LANES, hf * half + (tb + 1) * LANES)
            thr_rows, p_rows = [], []
            for hd in range(PEER_HEADS):
                s1 = sc_ref[pl.ds(2 * hd * N_KEYS + i0, n_sub), lanes]
                thr_rows.append(st_ref[hd, 2:3, lanes] - s1)
                p_rows.append(jnp.exp(s1 - st_ref[hd, 0:1, lanes]))
            for ii in range(n_sub):
                gsum = jnp.zeros((N_KEYS, LANES), f32)
                for hd in range(PEER_HEADS):
                    s2 = sc_ref[(2 * hd + 1) * N_KEYS:(2 * hd + 2) * N_KEYS, lanes]
                    kept = s2 >= thr_rows[hd][ii:ii + 1]
                    gsum = gsum + jnp.where(kept, q_ref[hd, :, lanes], 0.0) * p_rows[hd][ii:ii + 1]
                aa = a[ii * N_KEYS:(ii + 1) * N_KEYS, tb * LANES:(tb + 1) * LANES]
                acts[hf][ii * N_KEYS:(ii + 1) * N_KEYS, tb * LANES:(tb + 1) * LANES] = (
                    (aa + aa * lax.erf(aa * 0.7071067811865476)) * gsum).astype(bf16)

    a0 = hidden(0)
    values(pvp_ref, n_half - 1)
    weigh(0, a0)
    if n_half == 2:
        a1 = hidden(1)
        values(pvc_ref, 0)
        weigh(1, a1)
    pvp_ref[...] = pvc_ref[...]

    @pl.when(c == pl.num_programs(1) - 1)
    def _():
        values(pvc_ref, n_half - 1)
        o_ref[...] = x_ref[...] + acc_ref[...].T


def _experts(x, ht, sc, st, w, l, *, tn, ec):
    n, d = x.shape
    n_exp = w["pu"].shape[1]
    n_sets = 2 * PEER_HEADS
    return pl.pallas_call(
        functools.partial(_experts_body, tn=tn, ec=ec),
        grid=(n // tn, n_exp // ec),
        in_specs=[
            pl.BlockSpec((tn, d), lambda t, c: (t, 0)),
            pl.BlockSpec((d, tn), lambda t, c: (0, t)),
            pl.BlockSpec((n_sets * N_KEYS, tn), lambda t, c: (0, t)),
            pl.BlockSpec((PEER_HEADS, SUBLANES, tn), lambda t, c: (0, 0, t)),
            pl.BlockSpec((None, ec, d), lambda t, c: (l, c, 0)),
            pl.BlockSpec((None, d, ec), lambda t, c: (l, 0, c)),
        ],
        out_specs=pl.BlockSpec((tn, d), lambda t, c: (t, 0)),
        out_shape=jax.ShapeDtypeStruct((n, d), f32),
        scratch_shapes=[
            pltpu.VMEM((PEER_HEADS, N_KEYS, tn), f32),
            pltpu.VMEM((ec, tn // _n_halves(tn)), bf16), pltpu.VMEM((ec, tn // _n_halves(tn)), bf16),
            pltpu.VMEM((d, tn), f32), pltpu.VMEM((d, ec), bf16),
        ],
        compiler_params=_params("parallel", "arbitrary"),
        name="peer_experts",
    )(x, ht, sc, st, w["pu"], w["pvt"])


def _final_body(x_ref, g_ref, o_ref):
    o_ref[...] = _rms(x_ref[...], g_ref[...])


def _final_norm(x, g, tn):
    n, d = x.shape
    spec = pl.BlockSpec((tn, d), lambda i: (i, 0))
    return pl.pallas_call(
        _final_body, grid=(n // tn,), in_specs=[spec, _const_spec(g.shape)], out_specs=spec,
        out_shape=jax.ShapeDtypeStruct(x.shape, f32), compiler_params=_params("parallel"), name="final_norm",
    )(x, g)


def _gating_weights(w_s, b_s, lc):
    gdim = GMLP_WIDTH // GMLP_GROUPS
    return w_s[:, :, :lc, :lc], jnp.repeat(jnp.transpose(b_s[:, :, :lc], (0, 2, 1)), gdim, axis=2)


def _all_weights(norm1_g, w_in, b_f, ln_g, ln_b, w_s, b_s, w_pool, pool_scale, w_br, w_out, norm2_g, peer_wq,
                 peer_keys, peer_u, peer_v):
    depth, d = w_in.shape[:2]
    o_f = 3 * FOX_WIDTH
    o_u = o_f + FOX_HEADS
    o_g = o_u + 2 * GMLP_WIDTH + POOL_WIDTH
    wf = jnp.zeros((depth, d, LANES), f32).at[:, :, :FOX_HEADS].set(w_in[:, :, o_f:o_u])
    bf = jnp.zeros((depth, 1, LANES), f32).at[:, 0, :FOX_HEADS].set(b_f)
    pdim = POOL_WIDTH // POOL_GROUPS
    wpool = jnp.zeros((depth, POOL_WIDTH, POOL_WIDTH), f32)
    for g in range(POOL_GROUPS):
        wpool = wpool.at[:, g * pdim:(g + 1) * pdim, g * pdim:(g + 1) * pdim].set(w_pool[:, g])
    spare = LANES - HEAD_DIM

    def head_lanes(cols):
        cols = cols.reshape(depth, d, FOX_HEADS, HEAD_DIM)
        return jnp.pad(cols, ((0, 0), (0, 0), (0, 0), (0, spare))).reshape(depth, d, ATTN_WIDTH)

    qc = jnp.zeros((FOX_HEADS, LANES), f32).at[:, HEAD_DIM:HEAD_DIM + C_PIECES].set(-1.0).reshape(1, 1, ATTN_WIDTH)
    wqk = jnp.concatenate([head_lanes(w_in[:, :, :FOX_WIDTH]), head_lanes(w_in[:, :, FOX_WIDTH:2 * FOX_WIDTH])], axis=2)
    ws, bs = _gating_weights(w_s, b_s, GMLP_CHUNK)
    return {
        "g1": norm1_g[:, None], "wqk": wqk.astype(bf16), "qc": jnp.broadcast_to(qc, (depth, 1, ATTN_WIDTH)),
        "wkv": w_in[:, :, FOX_WIDTH:o_f].astype(bf16), "wf": wf.astype(bf16), "bf": bf,
        "wugp": w_in[:, :, o_u:o_g].astype(bf16), "lng": ln_g[:, None], "lnb": ln_b[:, None],
        "wg": w_in[:, :, o_g:].astype(bf16),
        "ws": ws, "bs": bs, "wpool": wpool.astype(bf16), "pscale": pool_scale[:, None],
        "wbr": w_br.astype(bf16), "wout": w_out.astype(bf16),
        "g2": norm2_g[:, None], "wqt": jnp.transpose(peer_wq, (0, 2, 1)).astype(bf16),
        "keys": peer_keys.reshape(depth, 2 * PEER_HEADS, N_KEYS, PEER_HALF).astype(bf16),
        "pu": peer_u.astype(bf16), "pvt": jnp.transpose(peer_v, (0, 2, 1)).astype(bf16),
    }


def _pick(n, prefs):
    for t in prefs:
        if n % t == 0:
            return t
    return n


def _layer(x, w, l, *, k_hist, v_hist, logf_hist, pool_hist, lc, tiles):
    nseq, L, d = x.shape
    n = nseq * L
    q, k, v, kb, vb, logf, u, vn, zp, gate = _inproj(x.reshape(n, d), w, l, tiles["inproj"])
    r3 = lambda a: a.reshape(nseq, L, a.shape[-1])
    q, kb, vb, logf3, u, vn, zp, gate = map(r3, (q, kb, vb, logf, u, vn, zp, gate))
    past = 0 if k_hist is None else k_hist.shape[1]
    spare = LANES - HEAD_DIM
    if past:
        k_hist = jnp.pad(k_hist.astype(bf16), ((0, 0), (0, 0), (0, 0), (0, spare)))
        kb = jnp.concatenate([k_hist.reshape(nseq, past, ATTN_WIDTH), kb], axis=1)
        vb = jnp.concatenate([v_hist.reshape(nseq, past, FOX_WIDTH).astype(bf16), vb], axis=1)
        lf_all = jnp.concatenate([logf_hist.astype(f32), logf3], axis=1)
    else:
        lf_all = logf3
    lk = past + L
    tk = tiles["tk"]
    lk_pad = -(-lk // tk) * tk
    if lk_pad != lk:
        padk = ((0, 0), (0, lk_pad - lk), (0, 0))
        kb, vb, lf_all = jnp.pad(kb, padk), jnp.pad(vb, padk), jnp.pad(lf_all, padk)
    nh = ATTN_HEADS_PER_STEP
    c_t, c_hi, c_mid, c_lo = _cumsum_lanes(jnp.transpose(lf_all, (0, 2, 1)))
    cq = jnp.transpose(c_t.reshape(nseq, FOX_HEADS // nh, nh, lk_pad)[:, :, :, past:past + L], (0, 1, 3, 2))
    hg = FOX_HEADS // nh
    pieces = jnp.stack([c_hi, c_mid, c_lo], axis=-1).reshape(nseq, hg, nh, lk_pad, C_PIECES)
    pieces = jnp.transpose(pieces, (0, 3, 1, 2, 4)).reshape(nseq, lk_pad, hg, nh * C_PIECES)
    pieces = jnp.pad(pieces, ((0, 0), (0, 0), (0, 0), (0, LANES - nh * C_PIECES))).reshape(nseq, lk_pad, hg * LANES)
    src = jnp.arange(LANES)[:, None]
    dst = jnp.arange(nh * LANES)[None, :]
    place = ((dst // LANES == src // C_PIECES) & (dst % LANES == HEAD_DIM + src % C_PIECES)
             & (src < nh * C_PIECES)).astype(bf16)
    o_a = _attention(q, kb, pieces, place, vb, cq, tq=tiles["tq"], tk=tk, q_off=past)
    tn = tiles["merge"]
    nt = L // tn
    first = jnp.zeros((nseq, HALO, POOL_WIDTH), f32)
    if pool_hist is not None:
        first = first.at[:, HALO - POOL_HIST:].set(pool_hist.astype(f32))
    tails = zp.reshape(nseq, nt, tn, POOL_WIDTH)[:, :nt - 1, tn - HALO:, :]
    prev = jnp.concatenate([first[:, None], tails], axis=1)
    x_mid = _merge(x, o_a, u, vn, zp, prev, gate, w, l, tn=tn, lc=lc, pos0=past)
    ht, sc, st = _route(x_mid.reshape(n, d), w, l, tiles["route"])
    x_new = _experts(x_mid.reshape(n, d), ht, sc, st, w, l, tn=tiles["experts"], ec=tiles["ec"])
    new_pool = zp[:, L - POOL_HIST:]
    return x_new.reshape(nseq, L, d), r3(k), r3(v), logf3, new_pool, vn


def kernel(x_prompt, x_sample, cache_fox_k, cache_fox_v, cache_fox_logf, state_pool, norm1_g, w_in, b_f, ln_g, ln_b,
           w_s, b_s, w_pool, pool_scale, w_br, w_out, norm2_g, peer_wq, peer_keys, peer_u, peer_v, final_g):
    depth = w_in.shape[0]
    bp, sp, d = x_prompt.shape
    bd, ls, _ = x_sample.shape
    assert sp % GMLP_CHUNK == 0 and ls <= GMLP_CHUNK and ls >= POOL_HIST and ls % 16 == 0
    params = (norm1_g, w_in, b_f, ln_g, ln_b, w_s, b_s, w_pool, pool_scale, w_br, w_out, norm2_g, peer_wq, peer_keys,
              peer_u, peer_v)
    tiles_p = {"inproj": _pick(bp * sp, (256,)), "tq": _pick(sp, (512, 256, 128)), "tk": _pick(sp, (512, 256, 128)),
               "merge": _pick(sp, (256, 128)), "route": _pick(bp * sp, (512, 256, 128)),
               "experts": _pick(bp * sp, (512, 256, 128)), "ec": EXPERT_CHUNK}
    ns = bd * ls
    keys_s = -(-(cache_fox_k.shape[2] + ls) // LANES) * LANES
    tiles_s = {"inproj": ns, "tq": ls, "tk": keys_s, "merge": ls, "route": ns, "experts": ns, "ec": EXPERT_CHUNK}
    xp, xs = x_prompt, x_sample
    outs_p, outs_s = [], []
    wp = _all_weights(*params)
    wsm = dict(wp)
    wsm["ws"], wsm["bs"] = _gating_weights(w_s, b_s, ls)
    for l in range(depth):
        xp, k, v, lf, pool, _ = _layer(xp, wp, l, k_hist=None, v_hist=None, logf_hist=None, pool_hist=None,
                                       lc=GMLP_CHUNK, tiles=tiles_p)
        outs_p.append((k.reshape(bp, sp, FOX_HEADS, HEAD_DIM), v.reshape(bp, sp, FOX_HEADS, HEAD_DIM), lf, pool))
        xs, k, v, lf, pool, vn = _layer(xs, wsm, l, k_hist=cache_fox_k[l], v_hist=cache_fox_v[l],
                                        logf_hist=cache_fox_logf[l], pool_hist=state_pool[l], lc=ls, tiles=tiles_s)
        outs_s.append((k.reshape(bd, ls, FOX_HEADS, HEAD_DIM), v.reshape(bd, ls, FOX_HEADS, HEAD_DIM), lf, pool, vn))
    g = final_g[None]
    y_prompt = _final_norm(xp.reshape(bp * sp, d), g, _pick(bp * sp, (512, 256, 128))).reshape(bp, sp, d)
    y_sample = _final_norm(xs.reshape(ns, d), g, ns).reshape(bd, ls, d)
    stack = lambda outs, i: jnp.stack([o[i] for o in outs])
    return (y_prompt, y_sample, stack(outs_p, 0), stack(outs_p, 1), stack(outs_p, 2), stack(outs_p, 3),
            stack(outs_s, 0), stack(outs_s, 1), stack(outs_s, 2), stack(outs_s, 3), stack(outs_s, 4))
```

```python
import functools

import jax
import jax.numpy as jnp
from jax import lax
from jax.experimental import pallas as pl
from jax.experimental.pallas import tpu as pltpu

f32 = jnp.float32
bf16 = jnp.bfloat16

EPS = 1e-6
FOX_HEADS = 8
HEAD_DIM = 64
FOX_WIDTH = FOX_HEADS * HEAD_DIM
GMLP_GROUPS = 4
GMLP_WIDTH = 256
GMLP_CHUNK = 128
SGU_BLOCK = 64
POOL_GROUPS = 4
POOL_WIDTH = 256
POOL_WINDOWS = (2, 4, 8, 16)
POOL_HIST = 15
HALO = 16
N_BRANCH = 3
PEER_HEADS = 8
N_KEYS = 128
PEER_TOPK = 16
PEER_HALF = 128

LANES = 128
SUBLANES = 8
EXPERT_CHUNK = SUBLANES * N_KEYS
TOP_ROWS = 3 * SUBLANES
VMEM_LIMIT = 56 * 1024 * 1024
NEG_INF = float("-inf")
LOG2E = 1.4426950408889634
C_PIECES = 3
ATTN_HEADS_PER_STEP = 4
ATTN_WIDTH = FOX_HEADS * LANES


def _params(*sem):
    return pltpu.CompilerParams(dimension_semantics=sem, vmem_limit_bytes=VMEM_LIMIT)


def _const_spec(shape):
    nd = len(shape)
    return pl.BlockSpec(shape, lambda *_: (0,) * nd)


def _layer_spec(a, l):
    nd = a.ndim - 1
    return pl.BlockSpec((None,) + a.shape[1:], lambda *_: (l,) + (0,) * nd)


def _rms(x, g):
    return x * lax.rsqrt(jnp.mean(x * x, axis=-1, keepdims=True) + EPS) * g


def _gelu(x):
    return 0.5 * x * (1.0 + lax.erf(x * 0.7071067811865476))


def _inproj_body(x_ref, g1_ref, wqk_ref, qc_ref, wkv_ref, wf_ref, bf_ref, wugp_ref, lng_ref, lnb_ref, wg_ref,
                 q_ref, k_ref, v_ref, kb_ref, vb_ref, lf_ref, u_ref, vn_ref, zp_ref, gate_ref):
    h = _rms(x_ref[...], g1_ref[...]).astype(bf16)
    zqk = jnp.dot(h, wqk_ref[...], preferred_element_type=f32)
    q_ref[...] = (zqk[:, :ATTN_WIDTH] * (HEAD_DIM ** -0.5 * LOG2E) + qc_ref[...]).astype(bf16)
    kb_ref[...] = zqk[:, ATTN_WIDTH:].astype(bf16)
    zkv = jnp.dot(h, wkv_ref[...], preferred_element_type=f32)
    v = zkv[:, FOX_WIDTH:]
    k_ref[...] = zkv[:, :FOX_WIDTH]
    v_ref[...] = v
    vb_ref[...] = v.astype(bf16)
    zf = jnp.dot(h, wf_ref[...], preferred_element_type=f32) + bf_ref[...]
    lf_ref[...] = jax.nn.log_sigmoid(zf)[:, :FOX_HEADS]
    zugp = jnp.dot(h, wugp_ref[...], preferred_element_type=f32)
    u_ref[...] = _gelu(zugp[:, :GMLP_WIDTH])
    gv = _gelu(zugp[:, GMLP_WIDTH:2 * GMLP_WIDTH])
    mu = jnp.mean(gv, axis=-1, keepdims=True)
    gc = gv - mu
    vn_ref[...] = gc * lax.rsqrt(jnp.mean(gc * gc, axis=-1, keepdims=True) + EPS) * lng_ref[...] + lnb_ref[...]
    zp_ref[...] = zugp[:, 2 * GMLP_WIDTH:]
    zg = jnp.dot(h, wg_ref[...], preferred_element_type=f32)
    gate_ref[...] = 1.0 / (1.0 + jnp.exp(-zg))


def _inproj(x, w, l, tn):
    n, d = x.shape
    row = lambda width: pl.BlockSpec((tn, width), lambda i: (i, 0))
    outs = [
        (ATTN_WIDTH, bf16), (FOX_WIDTH, f32), (FOX_WIDTH, f32), (ATTN_WIDTH, bf16), (FOX_WIDTH, bf16),
        (FOX_HEADS, f32), (GMLP_WIDTH, f32), (GMLP_WIDTH, f32), (POOL_WIDTH, f32), (N_BRANCH * d, f32),
    ]
    weights = [w["g1"], w["wqk"], w["qc"], w["wkv"], w["wf"], w["bf"], w["wugp"], w["lng"], w["lnb"], w["wg"]]
    return pl.pallas_call(
        _inproj_body,
        grid=(n // tn,),
        in_specs=[row(d)] + [_layer_spec(a, l) for a in weights],
        out_specs=[row(width) for width, _ in outs],
        out_shape=[jax.ShapeDtypeStruct((n, width), dt) for width, dt in outs],
        compiler_params=_params("parallel"),
        name="inproj",
    )(x, *weights)


def _cumsum_body(x_ref, c_ref, hi_ref, mid_ref, lo_ref):
    x = x_ref[0]
    n = x.shape[-1]
    lane = lax.broadcasted_iota(jnp.int32, x.shape, 1)
    shift = 1
    while shift < n:
        x = x + jnp.where(lane >= shift, pltpu.roll(x, shift, 1), 0.0)
        shift *= 2
    c = x * LOG2E
    c_ref[0] = c
    hi = c.astype(bf16)
    r1 = c - hi.astype(f32)
    mid = r1.astype(bf16)
    hi_ref[0] = hi
    mid_ref[0] = mid
    lo_ref[0] = (r1 - mid.astype(f32)).astype(bf16)


def _cumsum_lanes(x):
    b, r, n = x.shape
    spec = pl.BlockSpec((1, r, n), lambda i: (i, 0, 0))
    return pl.pallas_call(
        _cumsum_body, grid=(b,), in_specs=[spec], out_specs=[spec] * 4,
        out_shape=[jax.ShapeDtypeStruct(x.shape, f32)] + [jax.ShapeDtypeStruct(x.shape, bf16)] * 3,
        compiler_params=_params("parallel"), name="cumsum",
    )(x)


def _attn_body(q_ref, k_ref, pc_ref, place_ref, v_ref, cq_ref, o_ref, ka_ref, *, tq, tk, q_off):
    qi = pl.program_id(2)
    lane = lax.broadcasted_iota(jnp.int32, (tq, LANES), 1)
    row0 = q_off + qi * tq
    n_full = (row0 + 1) // tk
    n_all = (row0 + tq + tk - 1) // tk
    nh = ATTN_HEADS_PER_STEP
    lk = k_ref.shape[1]

    @pl.when(qi == 0)
    def _():
        def place(t, _):
            rows = pl.ds(pl.multiple_of(t * tk, tk), tk)
            moved = jnp.dot(pc_ref[0, rows, :], place_ref[...], preferred_element_type=f32)
            ka_ref[rows, :] = (k_ref[0, rows, :].astype(f32) + moved).astype(bf16)
            return 0

        lax.fori_loop(0, lk // tk, place, 0)

    qs = [q_ref[0, :, hh * LANES:(hh + 1) * LANES] for hh in range(nh)]
    cqs = [jnp.broadcast_to(cq_ref[0, 0, :, hh:hh + 1], (tq, LANES)) for hh in range(nh)]
    n_col = tk // LANES
    lane_k = lax.broadcasted_iota(jnp.int32, (tk, LANES), 1)

    def step(t, carry, masked):
        start = pl.multiple_of(t * tk, tk)
        if masked:
            kpos = start + lax.broadcasted_iota(jnp.int32, (tq, tk), 1)
            qpos = row0 + lax.broadcasted_iota(jnp.int32, (tq, tk), 0)
            allowed = kpos <= qpos
        out = []
        for hh in range(nh):
            m, acc = carry[hh]
            ks = ka_ref[pl.ds(start, tk), hh * LANES:(hh + 1) * LANES]
            vs = v_ref[0, pl.ds(start, tk), (hh // 2) * LANES:(hh // 2 + 1) * LANES]
            own = (lane_k < HEAD_DIM) if hh % 2 == 0 else (lane_k >= HEAD_DIM)
            vs = jnp.where(own, vs, jnp.ones((), bf16))
            s = lax.dot_general(qs[hh], ks, (((1,), (1,)), ((), ())), preferred_element_type=f32)
            if masked:
                s = jnp.where(allowed, s, NEG_INF)
            cols = [s[:, j * LANES:(j + 1) * LANES] for j in range(n_col)]
            smax = functools.reduce(jnp.maximum, cols)
            m_new = jnp.maximum(m, jnp.max(smax, axis=1, keepdims=True) + cqs[hh])
            alpha = jnp.exp2(m - m_new)
            r = cqs[hh] - m_new
            p = jnp.concatenate([jnp.exp2(col + r).astype(bf16) for col in cols], axis=1)
            acc = alpha * acc + jnp.dot(p, vs, preferred_element_type=f32)
            out.append((m_new, acc))
        return tuple(out)

    one = (jnp.full((tq, LANES), NEG_INF, f32), jnp.zeros((tq, LANES), f32))
    carry = lax.fori_loop(0, n_full, functools.partial(step, masked=False), (one,) * nh)
    carry = lax.fori_loop(n_full, n_all, functools.partial(step, masked=True), carry)
    outs = []
    for hh, (_, acc) in enumerate(carry):
        sums = (lane >= HEAD_DIM) if hh % 2 == 0 else (lane < HEAD_DIM)
        outs.append(acc / jnp.max(jnp.where(sums, acc, 0.0), axis=1, keepdims=True))
    for pr in range(nh // 2):
        o_ref[0, :, pr * LANES:(pr + 1) * LANES] = jnp.where(
            lane < HEAD_DIM, outs[2 * pr], outs[2 * pr + 1]).astype(o_ref.dtype)


def _attention(q, k, pieces, place, v, cq, *, tq, tk, q_off):
    b, lq, _ = q.shape
    lk = k.shape[1]
    nh = ATTN_HEADS_PER_STEP
    return pl.pallas_call(
        functools.partial(_attn_body, tq=tq, tk=tk, q_off=q_off),
        grid=(b, FOX_HEADS // nh, lq // tq),
        in_specs=[
            pl.BlockSpec((1, tq, nh * LANES), lambda b_, h_, i: (b_, i, h_)),
            pl.BlockSpec((1, lk, nh * LANES), lambda b_, h_, i: (b_, 0, h_)),
            pl.BlockSpec((1, lk, LANES), lambda b_, h_, i: (b_, 0, h_)),
            _const_spec(place.shape),
            pl.BlockSpec((1, lk, nh * HEAD_DIM), lambda b_, h_, i: (b_, 0, h_)),
            pl.BlockSpec((1, 1, tq, nh), lambda b_, h_, i: (b_, h_, i, 0)),
        ],
        out_specs=pl.BlockSpec((1, tq, nh * HEAD_DIM), lambda b_, h_, i: (b_, i, h_)),
        out_shape=jax.ShapeDtypeStruct((b, lq, FOX_WIDTH), bf16),
        scratch_shapes=[pltpu.VMEM((lk, nh * LANES), bf16)],
        compiler_params=_params("parallel", "parallel", "arbitrary"),
        name="fox_attention",
    )(q, k, pieces, place, v, cq)


def _merge_body(x_ref, oa_ref, u_ref, vn_ref, zp_ref, prev_ref, gate_ref, ws_ref, bs_ref, wpool_ref, pscale_ref,
                wbr_ref, wout_ref, o_ref, *, tn, lc, pos0):
    j = pl.program_id(1)
    d = x_ref.shape[-1]
    t_io = lax.broadcasted_iota(jnp.int32, (lc, lc), 0)
    s_io = lax.broadcasted_iota(jnp.int32, (lc, lc), 1)
    causal = (s_io // SGU_BLOCK) <= (t_io // SGU_BLOCK)
    grp = lax.broadcasted_iota(jnp.int32, (lc, GMLP_WIDTH), 1) // (GMLP_WIDTH // GMLP_GROUPS)
    wmask = [jnp.where(causal, ws_ref[g], 0.0).astype(bf16) for g in range(GMLP_GROUPS)]
    ob = []
    for c in range(tn // lc):
        vc = vn_ref[0, c * lc:(c + 1) * lc, :].astype(bf16)
        s = bs_ref[...]
        for g in range(GMLP_GROUPS):
            sg = jnp.dot(wmask[g], vc, preferred_element_type=f32)
            s = s + jnp.where(grp == g, sg, 0.0)
        ob.append(u_ref[0, c * lc:(c + 1) * lc, :] * s)
    o_b = ob[0] if len(ob) == 1 else jnp.concatenate(ob, axis=0)
    zp = zp_ref[0]
    ext = jnp.concatenate([prev_ref[0, 0], zp], axis=0)
    s2 = ext + pltpu.roll(ext, 1, 0)
    s4 = s2 + pltpu.roll(s2, 2, 0)
    s8 = s4 + pltpu.roll(s4, 4, 0)
    s16 = s8 + pltpu.roll(s8, 8, 0)
    pgrp = lax.broadcasted_iota(jnp.int32, (tn, POOL_WIDTH), 1) // (POOL_WIDTH // POOL_GROUPS)
    pos = pos0 + j * tn + lax.broadcasted_iota(jnp.int32, (tn, POOL_WIDTH), 0)
    win = jnp.where(pgrp == 0, s2[HALO:], jnp.where(pgrp == 1, s4[HALO:], jnp.where(pgrp == 2, s8[HALO:], s16[HALO:])))
    width = jnp.where(pgrp == 0, 2, jnp.where(pgrp == 1, 4, jnp.where(pgrp == 2, 8, 16)))
    cnt = jnp.minimum(pos + 1, width).astype(f32)
    dlt = win / cnt - zp
    o_c = jnp.dot(dlt.astype(bf16), wpool_ref[...], preferred_element_type=f32) * pscale_ref[...]
    ya = jnp.dot(oa_ref[0], wbr_ref[:FOX_WIDTH, :], preferred_element_type=f32)
    yb = jnp.dot(o_b.astype(bf16), wbr_ref[FOX_WIDTH:FOX_WIDTH + GMLP_WIDTH, :], preferred_element_type=f32)
    yc = jnp.dot(o_c.astype(bf16), wbr_ref[FOX_WIDTH + GMLP_WIDTH:, :], preferred_element_type=f32)
    merged = gate_ref[0, :, :d] * ya + gate_ref[0, :, d:2 * d] * yb + gate_ref[0, :, 2 * d:] * yc
    o_ref[0] = x_ref[0] + jnp.dot(merged.astype(bf16), wout_ref[...], preferred_element_type=f32)


def _merge(x, oa, u, vn, zp, prev, gate, w, l, *, tn, lc, pos0):
    nseq, L, d = x.shape
    seq = lambda width: pl.BlockSpec((1, tn, width), lambda b_, j: (b_, j, 0))
    weights = [w["ws"], w["bs"], w["wpool"], w["pscale"], w["wbr"], w["wout"]]
    return pl.pallas_call(
        functools.partial(_merge_body, tn=tn, lc=lc, pos0=pos0),
        grid=(nseq, L // tn),
        in_specs=[seq(d), seq(FOX_WIDTH), seq(GMLP_WIDTH), seq(GMLP_WIDTH), seq(POOL_WIDTH),
                  pl.BlockSpec((1, 1, HALO, POOL_WIDTH), lambda b_, j: (b_, j, 0, 0)), seq(N_BRANCH * d)]
        + [_layer_spec(a, l) for a in weights],
        out_specs=seq(d),
        out_shape=jax.ShapeDtypeStruct(x.shape, f32),
        compiler_params=_params("parallel", "parallel"),
        name="merge",
    )(x, oa, u, vn, zp, prev, gate, *weights)


def _merge_sort_pairs(n):
    def merge(lo, hi, r):
        step = r * 2
        if step < hi - lo:
            yield from merge(lo, hi, step)
            yield from merge(lo + r, hi, step)
            yield from [(i, i + r) for i in range(lo + r, hi - r, step)]
        else:
            yield (lo, lo + r)

    def sort(lo, hi):
        if hi - lo >= 1:
            mid = lo + (hi - lo) // 2
            yield from sort(lo, mid)
            yield from sort(mid + 1, hi)
            yield from merge(lo, hi, 1)

    return list(sort(0, n - 1))


def _top_values_sorted(block, k):
    n = block.shape[0] // SUBLANES
    groups = [block[g * SUBLANES:(g + 1) * SUBLANES, :] for g in range(n)]
    size = 1
    while size < n:
        size *= 2
    groups += [None] * (size - n)
    for i, j in _merge_sort_pairs(size):
        a, b = groups[i], groups[j]
        if a is None:
            groups[i], groups[j] = b, None
        elif b is not None:
            groups[i], groups[j] = jnp.maximum(a, b), jnp.minimum(a, b)
    groups = groups[:n]
    tops = []
    for rnd in range(k):
        m = jnp.max(groups[0], axis=0, keepdims=True)
        tops.append(m)
        hit = groups[0] == m
        depth = min(n, k - rnd)
        for g in range(depth - 1):
            groups[g] = jnp.where(hit, groups[g + 1], groups[g])
        groups[depth - 1] = jnp.where(hit, NEG_INF, groups[depth - 1])
    return tops


def _route_body(x_ref, g2_ref, wqt_ref, keys_ref, ht_ref, sc_ref, st_ref, top_ref, *, tn):
    h = _rms(x_ref[...], g2_ref[...]).astype(bf16)
    ht = h.T
    ht_ref[...] = ht
    qt = jnp.dot(wqt_ref[...], ht, preferred_element_type=f32).astype(bf16)
    n_sets = 2 * PEER_HEADS
    for r in range(n_sets):
        sc_ref[r * N_KEYS:(r + 1) * N_KEYS, :] = jnp.dot(
            keys_ref[r], qt[r * PEER_HALF:(r + 1) * PEER_HALF, :], preferred_element_type=f32)

    kk = PEER_TOPK + 1

    def half_top(r, _):
        base = pl.multiple_of(r * N_KEYS, N_KEYS)
        for tb in range(tn // LANES):
            blk = sc_ref[pl.ds(base, N_KEYS), tb * LANES:(tb + 1) * LANES]
            tops = _top_values_sorted(blk, kk)
            top_ref[r, :, tb * LANES:(tb + 1) * LANES] = jnp.concatenate(
                tops + [tops[-1]] * (TOP_ROWS - kk), axis=0)
        return 0

    lax.fori_loop(0, n_sets, half_top, 0)

    def head_stats(hd, _):
        for tb in range(tn // LANES):
            sl = slice(tb * LANES, (tb + 1) * LANES)
            a = top_ref[2 * hd, :, sl]
            b = top_ref[2 * hd + 1, :, sl]
            rows8 = lax.broadcasted_iota(jnp.int32, (SUBLANES, LANES), 0)
            never = jnp.full((SUBLANES, LANES), NEG_INF, f32)
            blocks = [(a[0:1], b[0:SUBLANES]), (a[0:1], b[SUBLANES:2 * SUBLANES])]
            for p in range(1, SUBLANES):
                blocks.append((a[p:p + 1], jnp.where(rows8 < kk // (p + 1), b[0:SUBLANES], never)))
            blocks.append((a[SUBLANES:2 * SUBLANES], b[0:1]))
            blocks.append((jnp.where(rows8 == 0, a[PEER_TOPK:PEER_TOPK + 1], a[0:1]),
                           jnp.where(rows8 == 0, b[0:1], jnp.where(rows8 == 1, b[PEER_TOPK:PEER_TOPK + 1], never))))
            cand = jnp.concatenate([ab + bb for ab, bb in blocks], axis=0)
            tops = _top_values_sorted(cand, kk)
            tau = 0.5 * (tops[-2] + tops[-1])
            z = jnp.zeros((1, LANES), f32)
            for ab, bb in blocks:
                kept = bb >= tau - ab
                z = z + jnp.sum(jnp.where(kept, jnp.exp((ab - a[0:1]) + (bb - b[0:1])), 0.0), axis=0, keepdims=True)
            st_ref[hd, :, sl] = jnp.concatenate(
                [a[0:1], b[0:1], tau, 0.5 / z, jnp.zeros((SUBLANES - 4, LANES), f32)], axis=0)
        return 0

    lax.fori_loop(0, PEER_HEADS, head_stats, 0)


def _route(x, w, l, tn):
    n, d = x.shape
    n_sets = 2 * PEER_HEADS
    return pl.pallas_call(
        functools.partial(_route_body, tn=tn),
        grid=(n // tn,),
        in_specs=[pl.BlockSpec((tn, d), lambda i: (i, 0)), _layer_spec(w["g2"], l), _layer_spec(w["wqt"], l),
                  _layer_spec(w["keys"], l)],
        out_specs=[pl.BlockSpec((d, tn), lambda i: (0, i)), pl.BlockSpec((n_sets * N_KEYS, tn), lambda i: (0, i)),
                   pl.BlockSpec((PEER_HEADS, SUBLANES, tn), lambda i: (0, 0, i))],
        out_shape=[jax.ShapeDtypeStruct((d, n), bf16), jax.ShapeDtypeStruct((n_sets * N_KEYS, n), f32),
                   jax.ShapeDtypeStruct((PEER_HEADS, SUBLANES, n), f32)],
        scratch_shapes=[pltpu.VMEM((n_sets, TOP_ROWS, tn), f32)],
        compiler_params=_params("parallel"),
        name="peer_route",
    )(x, w["g2"], w["wqt"], w["keys"])


def _n_halves(tn):
    return 2 if tn % (2 * LANES) == 0 else 1


def _experts_body(x_ref, ht_ref, sc_ref, st_ref, pu_ref, pvc_ref, o_ref, q_ref, act0_ref, act1_ref, acc_ref, pvp_ref,
                  *, tn, ec):
    c = pl.program_id(1)
    n_sub = ec // N_KEYS
    n_half = _n_halves(tn)
    half = tn // n_half
    acts = (act0_ref, act1_ref)

    @pl.when(c == 0)
    def _():
        for hd in range(PEER_HEADS):
            s2 = sc_ref[(2 * hd + 1) * N_KEYS:(2 * hd + 2) * N_KEYS, :]
            q_ref[hd] = jnp.exp(s2 - st_ref[hd, 1:2, :]) * st_ref[hd, 3:4, :]
        acc_ref[...] = jnp.zeros_like(acc_ref)
        acts[n_half - 1][...] = jnp.zeros_like(acts[n_half - 1])
        pvp_ref[...] = jnp.zeros_like(pvp_ref)

    i0 = pl.multiple_of(c * n_sub, n_sub)

    def hidden(hf):
        return jnp.dot(pu_ref[...], ht_ref[:, hf * half:(hf + 1) * half], preferred_element_type=f32)

    def values(pv_ref, hf):
        acc_ref[:, hf * half:(hf + 1) * half] += jnp.dot(pv_ref[...], acts[hf][...], preferred_element_type=f32)

    def weigh(hf, a):
        for tb in range(half // LANES):
            lanes = slice(hf * half + tb * LANES, hf * half + (tb + 1) * LANES)
            thr_rows, p_rows = [], []
            for hd in range(PEER_HEADS):
                s1 = sc_ref[pl.ds(2 * hd * N_KEYS + i0, n_sub), lanes]
                thr_rows.append(st_ref[hd, 2:3, lanes] - s1)
                p_rows.append(jnp.exp(s1 - st_ref[hd, 0:1, lanes]))
            for ii in range(n_sub):
                gsum = jnp.zeros((N_KEYS, LANES), f32)
                for hd in range(PEER_HEADS):
                    s2 = sc_ref[(2 * hd + 1) * N_KEYS:(2 * hd + 2) * N_KEYS, lanes]
                    kept = s2 >= thr_rows[hd][ii:ii + 1]
                    gsum = gsum + jnp.where(kept, q_ref[hd, :, lanes], 0.0) * p_rows[hd][ii:ii + 1]
                aa = a[ii * N_KEYS:(ii + 1) * N_KEYS, tb * LANES:(tb + 1) * LANES]
                acts[hf][ii * N_KEYS:(ii + 1) * N_KEYS, tb * LANES:(tb + 1) * LANES] = (
                    (aa + aa * lax.erf(aa * 0.7071067811865476)) * gsum).astype(bf16)

    a0 = hidden(0)
    values(pvp_ref, n_half - 1)
    weigh(0, a0)
    if n_half == 2:
        a1 = hidden(1)
        values(pvc_ref, 0)
        weigh(1, a1)
    pvp_ref[...] = pvc_ref[...]

    @pl.when(c == pl.num_programs(1) - 1)
    def _():
        values(pvc_ref, n_half - 1)
        o_ref[...] = x_ref[...] + acc_ref[...].T


def _experts(x, ht, sc, st, w, l, *, tn, ec):
    n, d = x.shape
    n_exp = w["pu"].shape[1]
    n_sets = 2 * PEER_HEADS
    return pl.pallas_call(
        functools.partial(_experts_body, tn=tn, ec=ec),
        grid=(n // tn, n_exp // ec),
        in_specs=[
            pl.BlockSpec((tn, d), lambda t, c: (t, 0)),
            pl.BlockSpec((d, tn), lambda t, c: (0, t)),
            pl.BlockSpec((n_sets * N_KEYS, tn), lambda t, c: (0, t)),
            pl.BlockSpec((PEER_HEADS, SUBLANES, tn), lambda t, c: (0, 0, t)),
            pl.BlockSpec((None, ec, d), lambda t, c: (l, c, 0)),
            pl.BlockSpec((None, d, ec), lambda t, c: (l, 0, c)),
        ],
        out_specs=pl.BlockSpec((tn, d), lambda t, c: (t, 0)),
        out_shape=jax.ShapeDtypeStruct((n, d), f32),
        scratch_shapes=[
            pltpu.VMEM((PEER_HEADS, N_KEYS, tn), f32),
            pltpu.VMEM((ec, tn // _n_halves(tn)), bf16), pltpu.VMEM((ec, tn // _n_halves(tn)), bf16),
            pltpu.VMEM((d, tn), f32), pltpu.VMEM((d, ec), bf16),
        ],
        compiler_params=_params("parallel", "arbitrary"),
        name="peer_experts",
    )(x, ht, sc, st, w["pu"], w["pvt"])


def _final_body(x_ref, g_ref, o_ref):
    o_ref[...] = _rms(x_ref[...], g_ref[...])


def _final_norm(x, g, tn):
    n, d = x.shape
    spec = pl.BlockSpec((tn, d), lambda i: (i, 0))
    return pl.pallas_call(
        _final_body, grid=(n // tn,), in_specs=[spec, _const_spec(g.shape)], out_specs=spec,
        out_shape=jax.ShapeDtypeStruct(x.shape, f32), compiler_params=_params("parallel"), name="final_norm",
    )(x, g)


def _gating_weights(w_s, b_s, lc):
    gdim = GMLP_WIDTH // GMLP_GROUPS
    return w_s[:, :, :lc, :lc], jnp.repeat(jnp.transpose(b_s[:, :, :lc], (0, 2, 1)), gdim, axis=2)


def _all_weights(norm1_g, w_in, b_f, ln_g, ln_b, w_s, b_s, w_pool, pool_scale, w_br, w_out, norm2_g, peer_wq,
                 peer_keys, peer_u, peer_v):
    depth, d = w_in.shape[:2]
    o_f = 3 * FOX_WIDTH
    o_u = o_f + FOX_HEADS
    o_g = o_u + 2 * GMLP_WIDTH + POOL_WIDTH
    wf = jnp.zeros((depth, d, LANES), f32).at[:, :, :FOX_HEADS].set(w_in[:, :, o_f:o_u])
    bf = jnp.zeros((depth, 1, LANES), f32).at[:, 0, :FOX_HEADS].set(b_f)
    pdim = POOL_WIDTH // POOL_GROUPS
    wpool = jnp.zeros((depth, POOL_WIDTH, POOL_WIDTH), f32)
    for g in range(POOL_GROUPS):
        wpool = wpool.at[:, g * pdim:(g + 1) * pdim, g * pdim:(g + 1) * pdim].set(w_pool[:, g])
    spare = LANES - HEAD_DIM

    def head_lanes(cols):
        cols = cols.reshape(depth, d, FOX_HEADS, HEAD_DIM)
        return jnp.pad(cols, ((0, 0), (0, 0), (0, 0), (0, spare))).reshape(depth, d, ATTN_WIDTH)

    qc = jnp.zeros((FOX_HEADS, LANES), f32).at[:, HEAD_DIM:HEAD_DIM + C_PIECES].set(-1.0).reshape(1, 1, ATTN_WIDTH)
    wqk = jnp.concatenate([head_lanes(w_in[:, :, :FOX_WIDTH]), head_lanes(w_in[:, :, FOX_WIDTH:2 * FOX_WIDTH])], axis=2)
    ws, bs = _gating_weights(w_s, b_s, GMLP_CHUNK)
    return {
        "g1": norm1_g[:, None], "wqk": wqk.astype(bf16), "qc": jnp.broadcast_to(qc, (depth, 1, ATTN_WIDTH)),
        "wkv": w_in[:, :, FOX_WIDTH:o_f].astype(bf16), "wf": wf.astype(bf16), "bf": bf,
        "wugp": w_in[:, :, o_u:o_g].astype(bf16), "lng": ln_g[:, None], "lnb": ln_b[:, None],
        "wg": w_in[:, :, o_g:].astype(bf16),
        "ws": ws, "bs": bs, "wpool": wpool.astype(bf16), "pscale": pool_scale[:, None],
        "wbr": w_br.astype(bf16), "wout": w_out.astype(bf16),
        "g2": norm2_g[:, None], "wqt": jnp.transpose(peer_wq, (0, 2, 1)).astype(bf16),
        "keys": peer_keys.reshape(depth, 2 * PEER_HEADS, N_KEYS, PEER_HALF).astype(bf16),
        "pu": peer_u.astype(bf16), "pvt": jnp.transpose(peer_v, (0, 2, 1)).astype(bf16),
    }


def _pick(n, prefs):
    for t in prefs:
        if n % t == 0:
            return t
    return n


def _layer(x, w, l, *, k_hist, v_hist, logf_hist, pool_hist, lc, tiles):
    nseq, L, d = x.shape
    n = nseq * L
    q, k, v, kb, vb, logf, u, vn, zp, gate = _inproj(x.reshape(n, d), w, l, tiles["inproj"])
    r3 = lambda a: a.reshape(nseq, L, a.shape[-1])
    q, kb, vb, logf3, u, vn, zp, gate = map(r3, (q, kb, vb, logf, u, vn, zp, gate))
    past = 0 if k_hist is None else k_hist.shape[1]
    spare = LANES - HEAD_DIM
    if past:
        k_hist = jnp.pad(k_hist.astype(bf16), ((0, 0), (0, 0), (0, 0), (0, spare)))
        kb = jnp.concatenate([k_hist.reshape(nseq, past, ATTN_WIDTH), kb], axis=1)
        vb = jnp.concatenate([v_hist.reshape(nseq, past, FOX_WIDTH).astype(bf16), vb], axis=1)
        lf_all = jnp.concatenate([logf_hist.astype(f32), logf3], axis=1)
    else:
        lf_all = logf3
    lk = past + L
    tk = tiles["tk"]
    lk_pad = -(-lk // tk) * tk
    if lk_pad != lk:
        padk = ((0, 0), (0, lk_pad - lk), (0, 0))
        kb, vb, lf_all = jnp.pad(kb, padk), jnp.pad(vb, padk), jnp.pad(lf_all, padk)
    nh = ATTN_HEADS_PER_STEP
    c_t, c_hi, c_mid, c_lo = _cumsum_lanes(jnp.transpose(lf_all, (0, 2, 1)))
    cq = jnp.transpose(c_t.reshape(nseq, FOX_HEADS // nh, nh, lk_pad)[:, :, :, past:past + L], (0, 1, 3, 2))
    hg = FOX_HEADS // nh
    pieces = jnp.stack([c_hi, c_mid, c_lo], axis=-1).reshape(nseq, hg, nh, lk_pad, C_PIECES)
    pieces = jnp.transpose(pieces, (0, 3, 1, 2, 4)).reshape(nseq, lk_pad, hg, nh * C_PIECES)
    pieces = jnp.pad(pieces, ((0, 0), (0, 0), (0, 0), (0, LANES - nh * C_PIECES))).reshape(nseq, lk_pad, hg * LANES)
    src = jnp.arange(LANES)[:, None]
    dst = jnp.arange(nh * LANES)[None, :]
    place = ((dst // LANES == src // C_PIECES) & (dst % LANES == HEAD_DIM + src % C_PIECES)
             & (src < nh * C_PIECES)).astype(bf16)
    o_a = _attention(q, kb, pieces, place, vb, cq, tq=tiles["tq"], tk=tk, q_off=past)
    tn = tiles["merge"]
    nt = L // tn
    first = jnp.zeros((nseq, HALO, POOL_WIDTH), f32)
    if pool_hist is not None:
        first = first.at[:, HALO - POOL_HIST:].set(pool_hist.astype(f32))
    tails = zp.reshape(nseq, nt, tn, POOL_WIDTH)[:, :nt - 1, tn - HALO:, :]
    prev = jnp.concatenate([first[:, None], tails], axis=1)
    x_mid = _merge(x, o_a, u, vn, zp, prev, gate, w, l, tn=tn, lc=lc, pos0=past)
    ht, sc, st = _route(x_mid.reshape(n, d), w, l, tiles["route"])
    x_new = _experts(x_mid.reshape(n, d), ht, sc, st, w, l, tn=tiles["experts"], ec=tiles["ec"])
    new_pool = zp[:, L - POOL_HIST:]
    return x_new.reshape(nseq, L, d), r3(k), r3(v), logf3, new_pool, vn


def kernel(x_prompt, x_sample, cache_fox_k, cache_fox_v, cache_fox_logf, state_pool, norm1_g, w_in, b_f, ln_g, ln_b,
           w_s, b_s, w_pool, pool_scale, w_br, w_out, norm2_g, peer_wq, peer_keys, peer_u, peer_v, final_g):
    depth = w_in.shape[0]
    bp, sp, d = x_prompt.shape
    bd, ls, _ = x_sample.shape
    assert sp % GMLP_CHUNK == 0 and ls <= GMLP_CHUNK and ls >= POOL_HIST and ls % 16 == 0
    params = (norm1_g, w_in, b_f, ln_g, ln_b, w_s, b_s, w_pool, pool_scale, w_br, w_out, norm2_g, peer_wq, peer_keys,
              peer_u, peer_v)
    tiles_p = {"inproj": _pick(bp * sp, (256,)), "tq": _pick(sp, (512, 256, 128)), "tk": _pick(sp, (1024, 512, 256, 128)),
               "merge": _pick(sp, (256, 128)), "route": _pick(bp * sp, (512, 256, 128)),
               "experts": _pick(bp * sp, (512, 256, 128)), "ec": EXPERT_CHUNK}
    ns = bd * ls
    keys_s = -(-(cache_fox_k.shape[2] + ls) // LANES) * LANES
    tiles_s = {"inproj": ns, "tq": ls, "tk": keys_s, "merge": ls, "route": ns, "experts": ns, "ec": EXPERT_CHUNK}
    xp, xs = x_prompt, x_sample
    outs_p, outs_s = [], []
    wp = _all_weights(*params)
    wsm = dict(wp)
    wsm["ws"], wsm["bs"] = _gating_weights(w_s, b_s, ls)
    for l in range(depth):
        xp, k, v, lf, pool, _ = _layer(xp, wp, l, k_hist=None, v_hist=None, logf_hist=None, pool_hist=None,
                                       lc=GMLP_CHUNK, tiles=tiles_p)
        outs_p.append((k.reshape(bp, sp, FOX_HEADS, HEAD_DIM), v.reshape(bp, sp, FOX_HEADS, HEAD_DIM), lf, pool))
        xs, k, v, lf, pool, vn = _layer(xs, wsm, l, k_hist=cache_fox_k[l], v_hist=cache_fox_v[l],
                                        logf_hist=cache_fox_logf[l], pool_hist=state_pool[l], lc=ls, tiles=tiles_s)
        outs_s.append((k.reshape(bd, ls, FOX_HEADS, HEAD_DIM), v.reshape(bd, ls, FOX_HEADS, HEAD_DIM), lf, pool, vn))
    g = final_g[None]
    y_prompt = _final_norm(xp.reshape(bp * sp, d), g, _pick(bp * sp, (512, 256, 128))).reshape(bp, sp, d)
    y_sample = _final_norm(xs.reshape(ns, d), g, ns).reshape(bd, ls, d)
    stack = lambda outs, i: jnp.stack([o[i] for o in outs])
    return (y_prompt, y_sample, stack(outs_p, 0), stack(outs_p, 1), stack(outs_p, 2), stack(outs_p, 3),
            stack(outs_s, 0), stack(outs_s, 1), stack(outs_s, 2), stack(outs_s, 3), stack(outs_s, 4))
```

```python
import functools

import jax
import jax.numpy as jnp
from jax import lax
from jax.experimental import pallas as pl
from jax.experimental.pallas import tpu as pltpu

f32 = jnp.float32
bf16 = jnp.bfloat16

EPS = 1e-6
FOX_HEADS = 8
HEAD_DIM = 64
FOX_WIDTH = FOX_HEADS * HEAD_DIM
GMLP_GROUPS = 4
GMLP_WIDTH = 256
GMLP_CHUNK = 128
SGU_BLOCK = 64
POOL_GROUPS = 4
POOL_WIDTH = 256
POOL_WINDOWS = (2, 4, 8, 16)
POOL_HIST = 15
HALO = 16
N_BRANCH = 3
PEER_HEADS = 8
N_KEYS = 128
PEER_TOPK = 16
PEER_HALF = 128

LANES = 128
SUBLANES = 8
EXPERT_CHUNK = SUBLANES * N_KEYS
TOP_ROWS = 3 * SUBLANES
VMEM_LIMIT = 56 * 1024 * 1024
NEG_INF = float("-inf")
LOG2E = 1.4426950408889634
C_PIECES = 3
ATTN_HEADS_PER_STEP = 4
ATTN_WIDTH = FOX_HEADS * LANES


def _params(*sem):
    return pltpu.CompilerParams(dimension_semantics=sem, vmem_limit_bytes=VMEM_LIMIT)


def _const_spec(shape):
    nd = len(shape)
    return pl.BlockSpec(shape, lambda *_: (0,) * nd)


def _layer_spec(a, l):
    nd = a.ndim - 1
    return pl.BlockSpec((None,) + a.shape[1:], lambda *_: (l,) + (0,) * nd)


def _rms(x, g):
    return x * lax.rsqrt(jnp.mean(x * x, axis=-1, keepdims=True) + EPS) * g


def _gelu(x):
    return 0.5 * x * (1.0 + lax.erf(x * 0.7071067811865476))


def _inproj_body(x_ref, g1_ref, wqk_ref, qc_ref, wkvt_ref, wv_ref, wf_ref, bf_ref, wugp_ref, lng_ref, lnb_ref, wg_ref,
                 q_ref, kt_ref, vt_ref, kb_ref, vb_ref, lf_ref, u_ref, vn_ref, zp_ref, gate_ref):
    h = _rms(x_ref[...], g1_ref[...]).astype(bf16)
    zqk = jnp.dot(h, wqk_ref[...], preferred_element_type=f32)
    q_ref[...] = (zqk[:, :ATTN_WIDTH] * (HEAD_DIM ** -0.5 * LOG2E) + qc_ref[...]).astype(bf16)
    kb_ref[...] = zqk[:, ATTN_WIDTH:].astype(bf16)
    zkvt = lax.dot_general(wkvt_ref[...], h, (((1,), (1,)), ((), ())), preferred_element_type=f32)
    kt_ref[...] = zkvt[:FOX_WIDTH, :]
    vt_ref[...] = zkvt[FOX_WIDTH:, :]
    vb_ref[...] = jnp.dot(h, wv_ref[...], preferred_element_type=f32).astype(bf16)
    zf = jnp.dot(h, wf_ref[...], preferred_element_type=f32) + bf_ref[...]
    lf_ref[...] = jax.nn.log_sigmoid(zf)[:, :FOX_HEADS]
    zugp = jnp.dot(h, wugp_ref[...], preferred_element_type=f32)
    u_ref[...] = _gelu(zugp[:, :GMLP_WIDTH])
    gv = _gelu(zugp[:, GMLP_WIDTH:2 * GMLP_WIDTH])
    mu = jnp.mean(gv, axis=-1, keepdims=True)
    gc = gv - mu
    vn_ref[...] = gc * lax.rsqrt(jnp.mean(gc * gc, axis=-1, keepdims=True) + EPS) * lng_ref[...] + lnb_ref[...]
    zp_ref[...] = zugp[:, 2 * GMLP_WIDTH:]
    zg = jnp.dot(h, wg_ref[...], preferred_element_type=f32)
    gate_ref[...] = 1.0 / (1.0 + jnp.exp(-zg))


def _inproj(x, w, l, tn):
    n, d = x.shape
    row = lambda width: pl.BlockSpec((tn, width), lambda i: (i, 0))
    col = pl.BlockSpec((FOX_WIDTH, tn), lambda i: (0, i))
    outs = [
        (ATTN_WIDTH, bf16), None, None, (ATTN_WIDTH, bf16), (FOX_WIDTH, bf16),
        (FOX_HEADS, f32), (GMLP_WIDTH, f32), (GMLP_WIDTH, f32), (POOL_WIDTH, f32), (N_BRANCH * d, f32),
    ]
    weights = [w["g1"], w["wqk"], w["qc"], w["wkvt"], w["wv"], w["wf"], w["bf"], w["wugp"], w["lng"], w["lnb"], w["wg"]]
    return pl.pallas_call(
        _inproj_body,
        grid=(n // tn,),
        in_specs=[row(d)] + [_layer_spec(a, l) for a in weights],
        out_specs=[col if o is None else row(o[0]) for o in outs],
        out_shape=[jax.ShapeDtypeStruct((FOX_WIDTH, n), f32) if o is None else jax.ShapeDtypeStruct((n, o[0]), o[1])
                   for o in outs],
        compiler_params=_params("parallel"),
        name="inproj",
    )(x, *weights)


def _cumsum_body(x_ref, c_ref, hi_ref, mid_ref, lo_ref):
    x = x_ref[0]
    n = x.shape[-1]
    lane = lax.broadcasted_iota(jnp.int32, x.shape, 1)
    shift = 1
    while shift < n:
        x = x + jnp.where(lane >= shift, pltpu.roll(x, shift, 1), 0.0)
        shift *= 2
    c = x * LOG2E
    c_ref[0] = c
    hi = c.astype(bf16)
    r1 = c - hi.astype(f32)
    mid = r1.astype(bf16)
    hi_ref[0] = hi
    mid_ref[0] = mid
    lo_ref[0] = (r1 - mid.astype(f32)).astype(bf16)


def _cumsum_lanes(x):
    b, r, n = x.shape
    spec = pl.BlockSpec((1, r, n), lambda i: (i, 0, 0))
    return pl.pallas_call(
        _cumsum_body, grid=(b,), in_specs=[spec], out_specs=[spec] * 4,
        out_shape=[jax.ShapeDtypeStruct(x.shape, f32)] + [jax.ShapeDtypeStruct(x.shape, bf16)] * 3,
        compiler_params=_params("parallel"), name="cumsum",
    )(x)


def _attn_body(q_ref, k_ref, pc_ref, place_ref, v_ref, cq_ref, o_ref, ka_ref, *, tq, tk, q_off):
    qi = pl.program_id(2)
    lane = lax.broadcasted_iota(jnp.int32, (tq, LANES), 1)
    row0 = q_off + qi * tq
    ts = tq if (tk % tq == 0 and tq % LANES == 0) else tk
    n_wide = row0 // tk
    n_full = (row0 + 1) // ts
    n_all = (row0 + tq + ts - 1) // ts
    nh = ATTN_HEADS_PER_STEP
    lk = k_ref.shape[1]

    @pl.when(qi == 0)
    def _():
        def place(t, _):
            rows = pl.ds(pl.multiple_of(t * tk, tk), tk)
            moved = jnp.dot(pc_ref[0, rows, :], place_ref[...], preferred_element_type=f32)
            ka_ref[rows, :] = (k_ref[0, rows, :].astype(f32) + moved).astype(bf16)
            return 0

        lax.fori_loop(0, lk // tk, place, 0)

    qs = [q_ref[0, :, hh * LANES:(hh + 1) * LANES] for hh in range(nh)]
    cqs = [jnp.broadcast_to(cq_ref[0, 0, :, hh:hh + 1], (tq, LANES)) for hh in range(nh)]

    def step(t, carry, width, masked):
        n_col = width // LANES
        lane_k = lax.broadcasted_iota(jnp.int32, (width, LANES), 1)
        start = pl.multiple_of(t * width, width)
        if masked:
            kpos = start + lax.broadcasted_iota(jnp.int32, (tq, width), 1)
            qpos = row0 + lax.broadcasted_iota(jnp.int32, (tq, width), 0)
            allowed = kpos <= qpos
        out = []
        for hh in range(nh):
            m, acc = carry[hh]
            ks = ka_ref[pl.ds(start, width), hh * LANES:(hh + 1) * LANES]
            vs = v_ref[0, pl.ds(start, width), (hh // 2) * LANES:(hh // 2 + 1) * LANES]
            own = (lane_k < HEAD_DIM) if hh % 2 == 0 else (lane_k >= HEAD_DIM)
            vs = jnp.where(own, vs, jnp.ones((), bf16))
            s = lax.dot_general(qs[hh], ks, (((1,), (1,)), ((), ())), preferred_element_type=f32)
            if masked:
                s = jnp.where(allowed, s, NEG_INF)
            cols = [s[:, j * LANES:(j + 1) * LANES] for j in range(n_col)]
            smax = functools.reduce(jnp.maximum, cols)
            m_new = jnp.maximum(m, jnp.max(smax, axis=1, keepdims=True) + cqs[hh])
            alpha = jnp.exp2(m - m_new)
            r = cqs[hh] - m_new
            p = jnp.concatenate([jnp.exp2(col + r).astype(bf16) for col in cols], axis=1)
            acc = alpha * acc + jnp.dot(p, vs, preferred_element_type=f32)
            out.append((m_new, acc))
        return tuple(out)

    one = (jnp.full((tq, LANES), NEG_INF, f32), jnp.zeros((tq, LANES), f32))
    carry = lax.fori_loop(0, n_wide, functools.partial(step, width=tk, masked=False), (one,) * nh)
    carry = lax.fori_loop(n_wide * (tk // ts), n_full, functools.partial(step, width=ts, masked=False), carry)
    carry = lax.fori_loop(n_full, n_all, functools.partial(step, width=ts, masked=True), carry)
    outs = []
    for hh, (_, acc) in enumerate(carry):
        sums = (lane >= HEAD_DIM) if hh % 2 == 0 else (lane < HEAD_DIM)
        outs.append(acc / jnp.max(jnp.where(sums, acc, 0.0), axis=1, keepdims=True))
    for pr in range(nh // 2):
        o_ref[0, :, pr * LANES:(pr + 1) * LANES] = jnp.where(
            lane < HEAD_DIM, outs[2 * pr], outs[2 * pr + 1]).astype(o_ref.dtype)


def _attention(q, k, pieces, place, v, cq, *, tq, tk, q_off):
    b, lq, _ = q.shape
    lk = k.shape[1]
    nh = ATTN_HEADS_PER_STEP
    return pl.pallas_call(
        functools.partial(_attn_body, tq=tq, tk=tk, q_off=q_off),
        grid=(b, FOX_HEADS // nh, lq // tq),
        in_specs=[
            pl.BlockSpec((1, tq, nh * LANES), lambda b_, h_, i: (b_, i, h_)),
            pl.BlockSpec((1, lk, nh * LANES), lambda b_, h_, i: (b_, 0, h_)),
            pl.BlockSpec((1, lk, LANES), lambda b_, h_, i: (b_, 0, h_)),
            _const_spec(place.shape),
            pl.BlockSpec((1, lk, nh * HEAD_DIM), lambda b_, h_, i: (b_, 0, h_)),
            pl.BlockSpec((1, 1, tq, nh), lambda b_, h_, i: (b_, h_, i, 0)),
        ],
        out_specs=pl.BlockSpec((1, tq, nh * HEAD_DIM), lambda b_, h_, i: (b_, i, h_)),
        out_shape=jax.ShapeDtypeStruct((b, lq, FOX_WIDTH), bf16),
        scratch_shapes=[pltpu.VMEM((lk, nh * LANES), bf16)],
        compiler_params=_params("parallel", "parallel", "arbitrary"),
        name="fox_attention",
    )(q, k, pieces, place, v, cq)


def _merge_body(x_ref, oa_ref, u_ref, vn_ref, zp_ref, prev_ref, gate_ref, ws_ref, bs_ref, wpool_ref, pscale_ref,
                wbr_ref, wout_ref, o_ref, *, tn, lc, pos0):
    j = pl.program_id(1)
    d = x_ref.shape[-1]
    t_io = lax.broadcasted_iota(jnp.int32, (lc, lc), 0)
    s_io = lax.broadcasted_iota(jnp.int32, (lc, lc), 1)
    causal = (s_io // SGU_BLOCK) <= (t_io // SGU_BLOCK)
    grp = lax.broadcasted_iota(jnp.int32, (lc, GMLP_WIDTH), 1) // (GMLP_WIDTH // GMLP_GROUPS)
    wmask = [jnp.where(causal, ws_ref[g], 0.0).astype(bf16) for g in range(GMLP_GROUPS)]
    ob = []
    for c in range(tn // lc):
        vc = vn_ref[0, c * lc:(c + 1) * lc, :].astype(bf16)
        s = bs_ref[...]
        for g in range(GMLP_GROUPS):
            sg = jnp.dot(wmask[g], vc, preferred_element_type=f32)
            s = s + jnp.where(grp == g, sg, 0.0)
        ob.append(u_ref[0, c * lc:(c + 1) * lc, :] * s)
    o_b = ob[0] if len(ob) == 1 else jnp.concatenate(ob, axis=0)
    zp = zp_ref[0]
    ext = jnp.concatenate([prev_ref[0, 0], zp], axis=0)
    s2 = ext + pltpu.roll(ext, 1, 0)
    s4 = s2 + pltpu.roll(s2, 2, 0)
    s8 = s4 + pltpu.roll(s4, 4, 0)
    s16 = s8 + pltpu.roll(s8, 8, 0)
    pgrp = lax.broadcasted_iota(jnp.int32, (tn, POOL_WIDTH), 1) // (POOL_WIDTH // POOL_GROUPS)
    pos = pos0 + j * tn + lax.broadcasted_iota(jnp.int32, (tn, POOL_WIDTH), 0)
    win = jnp.where(pgrp == 0, s2[HALO:], jnp.where(pgrp == 1, s4[HALO:], jnp.where(pgrp == 2, s8[HALO:], s16[HALO:])))
    width = jnp.where(pgrp == 0, 2, jnp.where(pgrp == 1, 4, jnp.where(pgrp == 2, 8, 16)))
    cnt = jnp.minimum(pos + 1, width).astype(f32)
    dlt = win / cnt - zp
    o_c = jnp.dot(dlt.astype(bf16), wpool_ref[...], preferred_element_type=f32) * pscale_ref[...]
    ya = jnp.dot(oa_ref[0], wbr_ref[:FOX_WIDTH, :], preferred_element_type=f32)
    yb = jnp.dot(o_b.astype(bf16), wbr_ref[FOX_WIDTH:FOX_WIDTH + GMLP_WIDTH, :], preferred_element_type=f32)
    yc = jnp.dot(o_c.astype(bf16), wbr_ref[FOX_WIDTH + GMLP_WIDTH:, :], preferred_element_type=f32)
    merged = gate_ref[0, :, :d] * ya + gate_ref[0, :, d:2 * d] * yb + gate_ref[0, :, 2 * d:] * yc
    o_ref[0] = x_ref[0] + jnp.dot(merged.astype(bf16), wout_ref[...], preferred_element_type=f32)


def _merge(x, oa, u, vn, zp, prev, gate, w, l, *, tn, lc, pos0):
    nseq, L, d = x.shape
    seq = lambda width: pl.BlockSpec((1, tn, width), lambda b_, j: (b_, j, 0))
    weights = [w["ws"], w["bs"], w["wpool"], w["pscale"], w["wbr"], w["wout"]]
    return pl.pallas_call(
        functools.partial(_merge_body, tn=tn, lc=lc, pos0=pos0),
        grid=(nseq, L // tn),
        in_specs=[seq(d), seq(FOX_WIDTH), seq(GMLP_WIDTH), seq(GMLP_WIDTH), seq(POOL_WIDTH),
                  pl.BlockSpec((1, 1, HALO, POOL_WIDTH), lambda b_, j: (b_, j, 0, 0)), seq(N_BRANCH * d)]
        + [_layer_spec(a, l) for a in weights],
        out_specs=seq(d),
        out_shape=jax.ShapeDtypeStruct(x.shape, f32),
        compiler_params=_params("parallel", "parallel"),
        name="merge",
    )(x, oa, u, vn, zp, prev, gate, *weights)


def _merge_sort_pairs(n):
    def merge(lo, hi, r):
        step = r * 2
        if step < hi - lo:
            yield from merge(lo, hi, step)
            yield from merge(lo + r, hi, step)
            yield from [(i, i + r) for i in range(lo + r, hi - r, step)]
        else:
            yield (lo, lo + r)

    def sort(lo, hi):
        if hi - lo >= 1:
            mid = lo + (hi - lo) // 2
            yield from sort(lo, mid)
            yield from sort(mid + 1, hi)
            yield from merge(lo, hi, 1)

    return list(sort(0, n - 1))


def _top_values_sorted(block, k):
    n = block.shape[0] // SUBLANES
    groups = [block[g * SUBLANES:(g + 1) * SUBLANES, :] for g in range(n)]
    size = 1
    while size < n:
        size *= 2
    groups += [None] * (size - n)
    for i, j in _merge_sort_pairs(size):
        a, b = groups[i], groups[j]
        if a is None:
            groups[i], groups[j] = b, None
        elif b is not None:
            groups[i], groups[j] = jnp.maximum(a, b), jnp.minimum(a, b)
    groups = groups[:n]
    tops = []
    for rnd in range(k):
        m = jnp.max(groups[0], axis=0, keepdims=True)
        tops.append(m)
        hit = groups[0] == m
        depth = min(n, k - rnd)
        for g in range(depth - 1):
            groups[g] = jnp.where(hit, groups[g + 1], groups[g])
        groups[depth - 1] = jnp.where(hit, NEG_INF, groups[depth - 1])
    return tops


def _route_body(x_ref, g2_ref, wqt_ref, keys_ref, ht_ref, sc_ref, st_ref, top_ref, *, tn):
    h = _rms(x_ref[...], g2_ref[...]).astype(bf16)
    ht = h.T
    ht_ref[...] = ht
    qt = jnp.dot(wqt_ref[...], ht, preferred_element_type=f32).astype(bf16)
    n_sets = 2 * PEER_HEADS
    for r in range(n_sets):
        sc_ref[r * N_KEYS:(r + 1) * N_KEYS, :] = jnp.dot(
            keys_ref[r], qt[r * PEER_HALF:(r + 1) * PEER_HALF, :], preferred_element_type=f32)

    kk = PEER_TOPK + 1

    def half_top(r, _):
        base = pl.multiple_of(r * N_KEYS, N_KEYS)
        for tb in range(tn // LANES):
            blk = sc_ref[pl.ds(base, N_KEYS), tb * LANES:(tb + 1) * LANES]
            tops = _top_values_sorted(blk, kk)
            top_ref[r, :, tb * LANES:(tb + 1) * LANES] = jnp.concatenate(
                tops + [tops[-1]] * (TOP_ROWS - kk), axis=0)
        return 0

    lax.fori_loop(0, n_sets, half_top, 0)

    def head_stats(hd, _):
        for tb in range(tn // LANES):
            sl = slice(tb * LANES, (tb + 1) * LANES)
            a = top_ref[2 * hd, :, sl]
            b = top_ref[2 * hd + 1, :, sl]
            rows8 = lax.broadcasted_iota(jnp.int32, (SUBLANES, LANES), 0)
            never = jnp.full((SUBLANES, LANES), NEG_INF, f32)
            blocks = [(a[0:1], b[0:SUBLANES]), (a[0:1], b[SUBLANES:2 * SUBLANES])]
            for p in range(1, SUBLANES):
                blocks.append((a[p:p + 1], jnp.where(rows8 < kk // (p + 1), b[0:SUBLANES], never)))
            blocks.append((a[SUBLANES:2 * SUBLANES], b[0:1]))
            blocks.append((jnp.where(rows8 == 0, a[PEER_TOPK:PEER_TOPK + 1], a[0:1]),
                           jnp.where(rows8 == 0, b[0:1], jnp.where(rows8 == 1, b[PEER_TOPK:PEER_TOPK + 1], never))))
            cand = jnp.concatenate([ab + bb for ab, bb in blocks], axis=0)
            tops = _top_values_sorted(cand, kk)
            tau = 0.5 * (tops[-2] + tops[-1])
            z = jnp.zeros((1, LANES), f32)
            for ab, bb in blocks:
                kept = bb >= tau - ab
                z = z + jnp.sum(jnp.where(kept, jnp.exp((ab - a[0:1]) + (bb - b[0:1])), 0.0), axis=0, keepdims=True)
            st_ref[hd, :, sl] = jnp.concatenate(
                [a[0:1], b[0:1], tau, 0.5 / z, jnp.zeros((SUBLANES - 4, LANES), f32)], axis=0)
        return 0

    lax.fori_loop(0, PEER_HEADS, head_stats, 0)


def _route(x, w, l, tn):
    n, d = x.shape
    n_sets = 2 * PEER_HEADS
    return pl.pallas_call(
        functools.partial(_route_body, tn=tn),
        grid=(n // tn,),
        in_specs=[pl.BlockSpec((tn, d), lambda i: (i, 0)), _layer_spec(w["g2"], l), _layer_spec(w["wqt"], l),
                  _layer_spec(w["keys"], l)],
        out_specs=[pl.BlockSpec((d, tn), lambda i: (0, i)), pl.BlockSpec((n_sets * N_KEYS, tn), lambda i: (0, i)),
                   pl.BlockSpec((PEER_HEADS, SUBLANES, tn), lambda i: (0, 0, i))],
        out_shape=[jax.ShapeDtypeStruct((d, n), bf16), jax.ShapeDtypeStruct((n_sets * N_KEYS, n), f32),
                   jax.ShapeDtypeStruct((PEER_HEADS, SUBLANES, n), f32)],
        scratch_shapes=[pltpu.VMEM((n_sets, TOP_ROWS, tn), f32)],
        compiler_params=_params("parallel"),
        name="peer_route",
    )(x, w["g2"], w["wqt"], w["keys"])


def _n_halves(tn):
    return 2 if tn % (2 * LANES) == 0 else 1


def _experts_body(x_ref, ht_ref, sc_ref, st_ref, pu_ref, pvc_ref, o_ref, q_ref, act0_ref, act1_ref, acc_ref, pvp_ref,
                  *, tn, ec):
    c = pl.program_id(1)
    n_sub = ec // N_KEYS
    n_half = _n_halves(tn)
    half = tn // n_half
    acts = (act0_ref, act1_ref)

    @pl.when(c == 0)
    def _():
        for hd in range(PEER_HEADS):
            s2 = sc_ref[(2 * hd + 1) * N_KEYS:(2 * hd + 2) * N_KEYS, :]
            q_ref[hd] = jnp.exp(s2 - st_ref[hd, 1:2, :]) * st_ref[hd, 3:4, :]
        acc_ref[...] = jnp.zeros_like(acc_ref)
        acts[n_half - 1][...] = jnp.zeros_like(acts[n_half - 1])
        pvp_ref[...] = jnp.zeros_like(pvp_ref)

    i0 = pl.multiple_of(c * n_sub, n_sub)

    def hidden(hf):
        return jnp.dot(pu_ref[...], ht_ref[:, hf * half:(hf + 1) * half], preferred_element_type=f32)

    def values(pv_ref, hf):
        acc_ref[:, hf * half:(hf + 1) * half] += jnp.dot(pv_ref[...], acts[hf][...], preferred_element_type=f32)

    def weigh(hf, a):
        for tb in range(half // LANES):
            lanes = slice(hf * half + tb * LANES, hf * half + (tb + 1) * LANES)
            thr_rows, p_rows = [], []
            for hd in range(PEER_HEADS):
                s1 = sc_ref[pl.ds(2 * hd * N_KEYS + i0, n_sub), lanes]
                thr_rows.append(st_ref[hd, 2:3, lanes] - s1)
                p_rows.append(jnp.exp(s1 - st_ref[hd, 0:1, lanes]))
            for ii in range(n_sub):
                gsum = jnp.zeros((N_KEYS, LANES), f32)
                for hd in range(PEER_HEADS):
                    s2 = sc_ref[(2 * hd + 1) * N_KEYS:(2 * hd + 2) * N_KEYS, lanes]
                    kept = s2 >= thr_rows[hd][ii:ii + 1]
                    gsum = gsum + jnp.where(kept, q_ref[hd, :, lanes], 0.0) * p_rows[hd][ii:ii + 1]
                aa = a[ii * N_KEYS:(ii + 1) * N_KEYS, tb * LANES:(tb + 1) * LANES]
                acts[hf][ii * N_KEYS:(ii + 1) * N_KEYS, tb * LANES:(tb + 1) * LANES] = (
                    (aa + aa * lax.erf(aa * 0.7071067811865476)) * gsum).astype(bf16)

    a0 = hidden(0)
    values(pvp_ref, n_half - 1)
    weigh(0, a0)
    if n_half == 2:
        a1 = hidden(1)
        values(pvc_ref, 0)
        weigh(1, a1)
    pvp_ref[...] = pvc_ref[...]

    @pl.when(c == pl.num_programs(1) - 1)
    def _():
        values(pvc_ref, n_half - 1)
        o_ref[...] = x_ref[...] + acc_ref[...].T


def _experts(x, ht, sc, st, w, l, *, tn, ec):
    n, d = x.shape
    n_exp = w["pu"].shape[1]
    n_sets = 2 * PEER_HEADS
    return pl.pallas_call(
        functools.partial(_experts_body, tn=tn, ec=ec),
        grid=(n // tn, n_exp // ec),
        in_specs=[
            pl.BlockSpec((tn, d), lambda t, c: (t, 0)),
            pl.BlockSpec((d, tn), lambda t, c: (0, t)),
            pl.BlockSpec((n_sets * N_KEYS, tn), lambda t, c: (0, t)),
            pl.BlockSpec((PEER_HEADS, SUBLANES, tn), lambda t, c: (0, 0, t)),
            pl.BlockSpec((None, ec, d), lambda t, c: (l, c, 0)),
            pl.BlockSpec((None, d, ec), lambda t, c: (l, 0, c)),
        ],
        out_specs=pl.BlockSpec((tn, d), lambda t, c: (t, 0)),
        out_shape=jax.ShapeDtypeStruct((n, d), f32),
        scratch_shapes=[
            pltpu.VMEM((PEER_HEADS, N_KEYS, tn), f32),
            pltpu.VMEM((ec, tn // _n_halves(tn)), bf16), pltpu.VMEM((ec, tn // _n_halves(tn)), bf16),
            pltpu.VMEM((d, tn), f32), pltpu.VMEM((d, ec), bf16),
        ],
        compiler_params=_params("parallel", "arbitrary"),
        name="peer_experts",
    )(x, ht, sc, st, w["pu"], w["pvt"])


def _final_body(x_ref, g_ref, o_ref):
    o_ref[...] = _rms(x_ref[...], g_ref[...])


def _final_norm(x, g, tn):
    n, d = x.shape
    spec = pl.BlockSpec((tn, d), lambda i: (i, 0))
    return pl.pallas_call(
        _final_body, grid=(n // tn,), in_specs=[spec, _const_spec(g.shape)], out_specs=spec,
        out_shape=jax.ShapeDtypeStruct(x.shape, f32), compiler_params=_params("parallel"), name="final_norm",
    )(x, g)


def _gating_weights(w_s, b_s, lc):
    gdim = GMLP_WIDTH // GMLP_GROUPS
    return w_s[:, :, :lc, :lc], jnp.repeat(jnp.transpose(b_s[:, :, :lc], (0, 2, 1)), gdim, axis=2)


def _all_weights(norm1_g, w_in, b_f, ln_g, ln_b, w_s, b_s, w_pool, pool_scale, w_br, w_out, norm2_g, peer_wq,
                 peer_keys, peer_u, peer_v):
    depth, d = w_in.shape[:2]
    o_f = 3 * FOX_WIDTH
    o_u = o_f + FOX_HEADS
    o_g = o_u + 2 * GMLP_WIDTH + POOL_WIDTH
    wf = jnp.zeros((depth, d, LANES), f32).at[:, :, :FOX_HEADS].set(w_in[:, :, o_f:o_u])
    bf = jnp.zeros((depth, 1, LANES), f32).at[:, 0, :FOX_HEADS].set(b_f)
    pdim = POOL_WIDTH // POOL_GROUPS
    wpool = jnp.zeros((depth, POOL_WIDTH, POOL_WIDTH), f32)
    for g in range(POOL_GROUPS):
        wpool = wpool.at[:, g * pdim:(g + 1) * pdim, g * pdim:(g + 1) * pdim].set(w_pool[:, g])
    spare = LANES - HEAD_DIM

    def head_lanes(cols):
        cols = cols.reshape(depth, d, FOX_HEADS, HEAD_DIM)
        return jnp.pad(cols, ((0, 0), (0, 0), (0, 0), (0, spare))).reshape(depth, d, ATTN_WIDTH)

    qc = jnp.zeros((FOX_HEADS, LANES), f32).at[:, HEAD_DIM:HEAD_DIM + C_PIECES].set(-1.0).reshape(1, 1, ATTN_WIDTH)
    wqk = jnp.concatenate([head_lanes(w_in[:, :, :FOX_WIDTH]), head_lanes(w_in[:, :, FOX_WIDTH:2 * FOX_WIDTH])], axis=2)
    ws, bs = _gating_weights(w_s, b_s, GMLP_CHUNK)
    return {
        "g1": norm1_g[:, None], "wqk": wqk.astype(bf16), "qc": jnp.broadcast_to(qc, (depth, 1, ATTN_WIDTH)),
        "wkvt": jnp.transpose(w_in[:, :, FOX_WIDTH:o_f], (0, 2, 1)).astype(bf16),
        "wv": w_in[:, :, 2 * FOX_WIDTH:o_f].astype(bf16), "wf": wf.astype(bf16), "bf": bf,
        "wugp": w_in[:, :, o_u:o_g].astype(bf16), "lng": ln_g[:, None], "lnb": ln_b[:, None],
        "wg": w_in[:, :, o_g:].astype(bf16),
        "ws": ws, "bs": bs, "wpool": wpool.astype(bf16), "pscale": pool_scale[:, None],
        "wbr": w_br.astype(bf16), "wout": w_out.astype(bf16),
        "g2": norm2_g[:, None], "wqt": jnp.transpose(peer_wq, (0, 2, 1)).astype(bf16),
        "keys": peer_keys.reshape(depth, 2 * PEER_HEADS, N_KEYS, PEER_HALF).astype(bf16),
        "pu": peer_u.astype(bf16), "pvt": jnp.transpose(peer_v, (0, 2, 1)).astype(bf16),
    }


def _pick(n, prefs):
    for t in prefs:
        if n % t == 0:
            return t
    return n


def _layer(x, w, l, *, k_hist, v_hist, logf_hist, pool_hist, lc, tiles):
    nseq, L, d = x.shape
    n = nseq * L
    q, kt, vt, kb, vb, logf, u, vn, zp, gate = _inproj(x.reshape(n, d), w, l, tiles["inproj"])
    r3 = lambda a: a.reshape(nseq, L, a.shape[-1])
    q, kb, vb, logf3, u, vn, zp, gate = map(r3, (q, kb, vb, logf, u, vn, zp, gate))
    past = 0 if k_hist is None else k_hist.shape[1]
    spare = LANES - HEAD_DIM
    if past:
        k_hist = jnp.pad(k_hist.astype(bf16), ((0, 0), (0, 0), (0, 0), (0, spare)))
        kb = jnp.concatenate([k_hist.reshape(nseq, past, ATTN_WIDTH), kb], axis=1)
        vb = jnp.concatenate([v_hist.reshape(nseq, past, FOX_WIDTH).astype(bf16), vb], axis=1)
        lf_all = jnp.concatenate([logf_hist.astype(f32), logf3], axis=1)
    else:
        lf_all = logf3
    lk = past + L
    tk = tiles["tk"]
    lk_pad = -(-lk // tk) * tk
    if lk_pad != lk:
        padk = ((0, 0), (0, lk_pad - lk), (0, 0))
        kb, vb, lf_all = jnp.pad(kb, padk), jnp.pad(vb, padk), jnp.pad(lf_all, padk)
    nh = ATTN_HEADS_PER_STEP
    c_t, c_hi, c_mid, c_lo = _cumsum_lanes(jnp.transpose(lf_all, (0, 2, 1)))
    cq = jnp.transpose(c_t.reshape(nseq, FOX_HEADS // nh, nh, lk_pad)[:, :, :, past:past + L], (0, 1, 3, 2))
    hg = FOX_HEADS // nh
    pieces = jnp.stack([c_hi, c_mid, c_lo], axis=-1).reshape(nseq, hg, nh, lk_pad, C_PIECES)
    pieces = jnp.transpose(pieces, (0, 3, 1, 2, 4)).reshape(nseq, lk_pad, hg, nh * C_PIECES)
    pieces = jnp.pad(pieces, ((0, 0), (0, 0), (0, 0), (0, LANES - nh * C_PIECES))).reshape(nseq, lk_pad, hg * LANES)
    src = jnp.arange(LANES)[:, None]
    dst = jnp.arange(nh * LANES)[None, :]
    place = ((dst // LANES == src // C_PIECES) & (dst % LANES == HEAD_DIM + src % C_PIECES)
             & (src < nh * C_PIECES)).astype(bf16)
    o_a = _attention(q, kb, pieces, place, vb, cq, tq=tiles["tq"], tk=tk, q_off=past)
    tn = tiles["merge"]
    nt = L // tn
    first = jnp.zeros((nseq, HALO, POOL_WIDTH), f32)
    if pool_hist is not None:
        first = first.at[:, HALO - POOL_HIST:].set(pool_hist.astype(f32))
    tails = zp.reshape(nseq, nt, tn, POOL_WIDTH)[:, :nt - 1, tn - HALO:, :]
    prev = jnp.concatenate([first[:, None], tails], axis=1)
    x_mid = _merge(x, o_a, u, vn, zp, prev, gate, w, l, tn=tn, lc=lc, pos0=past)
    ht, sc, st = _route(x_mid.reshape(n, d), w, l, tiles["route"])
    x_new = _experts(x_mid.reshape(n, d), ht, sc, st, w, l, tn=tiles["experts"], ec=tiles["ec"])
    new_pool = zp[:, L - POOL_HIST:]
    cache = lambda t: jnp.transpose(t.reshape(FOX_HEADS, HEAD_DIM, nseq, L), (2, 3, 0, 1))
    return x_new.reshape(nseq, L, d), cache(kt), cache(vt), logf3, new_pool, vn


def kernel(x_prompt, x_sample, cache_fox_k, cache_fox_v, cache_fox_logf, state_pool, norm1_g, w_in, b_f, ln_g, ln_b,
           w_s, b_s, w_pool, pool_scale, w_br, w_out, norm2_g, peer_wq, peer_keys, peer_u, peer_v, final_g):
    depth = w_in.shape[0]
    bp, sp, d = x_prompt.shape
    bd, ls, _ = x_sample.shape
    assert sp % GMLP_CHUNK == 0 and ls <= GMLP_CHUNK and ls >= POOL_HIST and ls % 16 == 0
    params = (norm1_g, w_in, b_f, ln_g, ln_b, w_s, b_s, w_pool, pool_scale, w_br, w_out, norm2_g, peer_wq, peer_keys,
              peer_u, peer_v)
    tiles_p = {"inproj": _pick(bp * sp, (256,)), "tq": _pick(sp, (512, 256, 128)), "tk": _pick(sp, (1024, 512, 256, 128)),
               "merge": _pick(sp, (256, 128)), "route": _pick(bp * sp, (512, 256, 128)),
               "experts": _pick(bp * sp, (512, 256, 128)), "ec": EXPERT_CHUNK}
    ns = bd * ls
    keys_s = -(-(cache_fox_k.shape[2] + ls) // LANES) * LANES
    tiles_s = {"inproj": ns, "tq": ls, "tk": keys_s, "merge": ls, "route": ns, "experts": ns, "ec": EXPERT_CHUNK}
    xp, xs = x_prompt, x_sample
    outs_p, outs_s = [], []
    wp = _all_weights(*params)
    wsm = dict(wp)
    wsm["ws"], wsm["bs"] = _gating_weights(w_s, b_s, ls)
    for l in range(depth):
        xp, k, v, lf, pool, _ = _layer(xp, wp, l, k_hist=None, v_hist=None, logf_hist=None, pool_hist=None,
                                       lc=GMLP_CHUNK, tiles=tiles_p)
        outs_p.append((k, v, lf, pool))
        xs, k, v, lf, pool, vn = _layer(xs, wsm, l, k_hist=cache_fox_k[l], v_hist=cache_fox_v[l],
                                        logf_hist=cache_fox_logf[l], pool_hist=state_pool[l], lc=ls, tiles=tiles_s)
        outs_s.append((k, v, lf, pool, vn))
    g = final_g[None]
    y_prompt = _final_norm(xp.reshape(bp * sp, d), g, _pick(bp * sp, (512, 256, 128))).reshape(bp, sp, d)
    y_sample = _final_norm(xs.reshape(ns, d), g, ns).reshape(bd, ls, d)
    stack = lambda outs, i: jnp.stack([o[i] for o in outs])
    return (y_prompt, y_sample, stack(outs_p, 0), stack(outs_p, 1), stack(outs_p, 2), stack(outs_p, 3),
            stack(outs_s, 0), stack(outs_s, 1), stack(outs_s, 2), stack(outs_s, 3), stack(outs_s, 4))
```

```python
import functools

import jax
import jax.numpy as jnp
from jax import lax
from jax.experimental import pallas as pl
from jax.experimental.pallas import tpu as pltpu

f32 = jnp.float32
bf16 = jnp.bfloat16

EPS = 1e-6
FOX_HEADS = 8
HEAD_DIM = 64
FOX_WIDTH = FOX_HEADS * HEAD_DIM
GMLP_GROUPS = 4
GMLP_WIDTH = 256
GMLP_CHUNK = 128
SGU_BLOCK = 64
POOL_GROUPS = 4
POOL_WIDTH = 256
POOL_WINDOWS = (2, 4, 8, 16)
POOL_HIST = 15
HALO = 16
N_BRANCH = 3
PEER_HEADS = 8
N_KEYS = 128
PEER_TOPK = 16
PEER_HALF = 128

LANES = 128
SUBLANES = 8
EXPERT_CHUNK = SUBLANES * N_KEYS
HIDDEN_KEYS = 2
TOP_ROWS = 3 * SUBLANES
VMEM_LIMIT = 56 * 1024 * 1024
NEG_INF = float("-inf")
LOG2E = 1.4426950408889634
C_PIECES = 3
ATTN_HEADS_PER_STEP = 4
ATTN_WIDTH = FOX_HEADS * LANES


def _params(*sem):
    return pltpu.CompilerParams(dimension_semantics=sem, vmem_limit_bytes=VMEM_LIMIT)


def _const_spec(shape):
    nd = len(shape)
    return pl.BlockSpec(shape, lambda *_: (0,) * nd)


def _layer_spec(a, l):
    nd = a.ndim - 1
    return pl.BlockSpec((None,) + a.shape[1:], lambda *_: (l,) + (0,) * nd)


def _rms(x, g):
    return x * lax.rsqrt(jnp.mean(x * x, axis=-1, keepdims=True) + EPS) * g


def _gelu(x):
    return 0.5 * x * (1.0 + lax.erf(x * 0.7071067811865476))


def _inproj_body(x_ref, g1_ref, wqk_ref, qc_ref, wkv_ref, wf_ref, bf_ref, wugp_ref, lng_ref, lnb_ref, wg_ref,
                 q_ref, k_ref, v_ref, kb_ref, vb_ref, lf_ref, u_ref, vn_ref, zp_ref, gate_ref):
    h = _rms(x_ref[...], g1_ref[...]).astype(bf16)
    zqk = jnp.dot(h, wqk_ref[...], preferred_element_type=f32)
    q_ref[...] = (zqk[:, :ATTN_WIDTH] * (HEAD_DIM ** -0.5 * LOG2E) + qc_ref[...]).astype(bf16)
    kb_ref[...] = zqk[:, ATTN_WIDTH:].astype(bf16)
    zkv = jnp.dot(h, wkv_ref[...], preferred_element_type=f32)
    v = zkv[:, FOX_WIDTH:]
    k_ref[...] = zkv[:, :FOX_WIDTH]
    v_ref[...] = v
    vb_ref[...] = v.astype(bf16)
    zf = jnp.dot(h, wf_ref[...], preferred_element_type=f32) + bf_ref[...]
    lf_ref[...] = jax.nn.log_sigmoid(zf)[:, :FOX_HEADS]
    zugp = jnp.dot(h, wugp_ref[...], preferred_element_type=f32)
    u_ref[...] = _gelu(zugp[:, :GMLP_WIDTH])
    gv = _gelu(zugp[:, GMLP_WIDTH:2 * GMLP_WIDTH])
    mu = jnp.mean(gv, axis=-1, keepdims=True)
    gc = gv - mu
    vn_ref[...] = gc * lax.rsqrt(jnp.mean(gc * gc, axis=-1, keepdims=True) + EPS) * lng_ref[...] + lnb_ref[...]
    zp_ref[...] = zugp[:, 2 * GMLP_WIDTH:]
    zg = jnp.dot(h, wg_ref[...], preferred_element_type=f32)
    gate_ref[...] = 1.0 / (1.0 + jnp.exp(-zg))


def _inproj(x, w, l, tn):
    n, d = x.shape
    row = lambda width: pl.BlockSpec((tn, width), lambda i: (i, 0))
    outs = [
        (ATTN_WIDTH, bf16), (FOX_WIDTH, f32), (FOX_WIDTH, f32), (ATTN_WIDTH, bf16), (FOX_WIDTH, bf16),
        (FOX_HEADS, f32), (GMLP_WIDTH, f32), (GMLP_WIDTH, f32), (POOL_WIDTH, f32), (N_BRANCH * d, f32),
    ]
    weights = [w["g1"], w["wqk"], w["qc"], w["wkv"], w["wf"], w["bf"], w["wugp"], w["lng"], w["lnb"], w["wg"]]
    return pl.pallas_call(
        _inproj_body,
        grid=(n // tn,),
        in_specs=[row(d)] + [_layer_spec(a, l) for a in weights],
        out_specs=[row(width) for width, _ in outs],
        out_shape=[jax.ShapeDtypeStruct((n, width), dt) for width, dt in outs],
        compiler_params=_params("parallel"),
        name="inproj",
    )(x, *weights)


def _cumsum_body(x_ref, c_ref, hi_ref, mid_ref, lo_ref):
    x = x_ref[0]
    n = x.shape[-1]
    lane = lax.broadcasted_iota(jnp.int32, x.shape, 1)
    shift = 1
    while shift < n:
        x = x + jnp.where(lane >= shift, pltpu.roll(x, shift, 1), 0.0)
        shift *= 2
    c = x * LOG2E
    c_ref[0] = c
    hi = c.astype(bf16)
    r1 = c - hi.astype(f32)
    mid = r1.astype(bf16)
    hi_ref[0] = hi
    mid_ref[0] = mid
    lo_ref[0] = (r1 - mid.astype(f32)).astype(bf16)


def _cumsum_lanes(x):
    b, r, n = x.shape
    spec = pl.BlockSpec((1, r, n), lambda i: (i, 0, 0))
    return pl.pallas_call(
        _cumsum_body, grid=(b,), in_specs=[spec], out_specs=[spec] * 4,
        out_shape=[jax.ShapeDtypeStruct(x.shape, f32)] + [jax.ShapeDtypeStruct(x.shape, bf16)] * 3,
        compiler_params=_params("parallel"), name="cumsum",
    )(x)


def _attn_body(q_ref, k_ref, pc_ref, place_ref, v_ref, cq_ref, o_ref, ka_ref, *, tq, tk, q_off):
    qi = pl.program_id(2)
    lane = lax.broadcasted_iota(jnp.int32, (tq, LANES), 1)
    row0 = q_off + qi * tq
    n_full = (row0 + 1) // tk
    n_all = (row0 + tq + tk - 1) // tk
    nh = ATTN_HEADS_PER_STEP
    lk = k_ref.shape[1]

    @pl.when(qi == 0)
    def _():
        def place(t, _):
            rows = pl.ds(pl.multiple_of(t * tk, tk), tk)
            moved = jnp.dot(pc_ref[0, rows, :], place_ref[...], preferred_element_type=f32)
            ka_ref[rows, :] = (k_ref[0, rows, :].astype(f32) + moved).astype(bf16)
            return 0

        lax.fori_loop(0, lk // tk, place, 0)

    qs = [q_ref[0, :, hh * LANES:(hh + 1) * LANES] for hh in range(nh)]
    cqs = [jnp.broadcast_to(cq_ref[0, 0, :, hh:hh + 1], (tq, LANES)) for hh in range(nh)]
    n_col = tk // LANES
    lane_k = lax.broadcasted_iota(jnp.int32, (tk, LANES), 1)

    def step(t, carry, masked):
        start = pl.multiple_of(t * tk, tk)
        if masked:
            kpos = start + lax.broadcasted_iota(jnp.int32, (tq, tk), 1)
            qpos = row0 + lax.broadcasted_iota(jnp.int32, (tq, tk), 0)
            allowed = kpos <= qpos
        out = []
        for hh in range(nh):
            m, acc = carry[hh]
            ks = ka_ref[pl.ds(start, tk), hh * LANES:(hh + 1) * LANES]
            vs = v_ref[0, pl.ds(start, tk), (hh // 2) * LANES:(hh // 2 + 1) * LANES]
            own = (lane_k < HEAD_DIM) if hh % 2 == 0 else (lane_k >= HEAD_DIM)
            vs = jnp.where(own, vs, jnp.ones((), bf16))
            s = lax.dot_general(qs[hh], ks, (((1,), (1,)), ((), ())), preferred_element_type=f32)
            if masked:
                s = jnp.where(allowed, s, NEG_INF)
            cols = [s[:, j * LANES:(j + 1) * LANES] for j in range(n_col)]
            smax = functools.reduce(jnp.maximum, cols)
            m_new = jnp.maximum(m, jnp.max(smax, axis=1, keepdims=True) + cqs[hh])
            alpha = jnp.exp2(m - m_new)
            r = cqs[hh] - m_new
            p = jnp.concatenate([jnp.exp2(col + r).astype(bf16) for col in cols], axis=1)
            acc = alpha * acc + jnp.dot(p, vs, preferred_element_type=f32)
            out.append((m_new, acc))
        return tuple(out)

    one = (jnp.full((tq, LANES), NEG_INF, f32), jnp.zeros((tq, LANES), f32))
    carry = lax.fori_loop(0, n_full, functools.partial(step, masked=False), (one,) * nh)
    carry = lax.fori_loop(n_full, n_all, functools.partial(step, masked=True), carry)
    outs = []
    for hh, (_, acc) in enumerate(carry):
        sums = (lane >= HEAD_DIM) if hh % 2 == 0 else (lane < HEAD_DIM)
        outs.append(acc / jnp.max(jnp.where(sums, acc, 0.0), axis=1, keepdims=True))
    for pr in range(nh // 2):
        o_ref[0, :, pr * LANES:(pr + 1) * LANES] = jnp.where(
            lane < HEAD_DIM, outs[2 * pr], outs[2 * pr + 1]).astype(o_ref.dtype)


def _attention(q, k, pieces, place, v, cq, *, tq, tk, q_off):
    b, lq, _ = q.shape
    lk = k.shape[1]
    nh = ATTN_HEADS_PER_STEP
    return pl.pallas_call(
        functools.partial(_attn_body, tq=tq, tk=tk, q_off=q_off),
        grid=(b, FOX_HEADS // nh, lq // tq),
        in_specs=[
            pl.BlockSpec((1, tq, nh * LANES), lambda b_, h_, i: (b_, i, h_)),
            pl.BlockSpec((1, lk, nh * LANES), lambda b_, h_, i: (b_, 0, h_)),
            pl.BlockSpec((1, lk, LANES), lambda b_, h_, i: (b_, 0, h_)),
            _const_spec(place.shape),
            pl.BlockSpec((1, lk, nh * HEAD_DIM), lambda b_, h_, i: (b_, 0, h_)),
            pl.BlockSpec((1, 1, tq, nh), lambda b_, h_, i: (b_, h_, i, 0)),
        ],
        out_specs=pl.BlockSpec((1, tq, nh * HEAD_DIM), lambda b_, h_, i: (b_, i, h_)),
        out_shape=jax.ShapeDtypeStruct((b, lq, FOX_WIDTH), bf16),
        scratch_shapes=[pltpu.VMEM((lk, nh * LANES), bf16)],
        compiler_params=_params("parallel", "parallel", "arbitrary"),
        name="fox_attention",
    )(q, k, pieces, place, v, cq)


def _merge_body(x_ref, oa_ref, u_ref, vn_ref, zp_ref, prev_ref, gate_ref, ws_ref, bs_ref, wpool_ref, pscale_ref,
                wbr_ref, wout_ref, o_ref, *, tn, lc, pos0):
    j = pl.program_id(1)
    d = x_ref.shape[-1]
    t_io = lax.broadcasted_iota(jnp.int32, (lc, lc), 0)
    s_io = lax.broadcasted_iota(jnp.int32, (lc, lc), 1)
    causal = (s_io // SGU_BLOCK) <= (t_io // SGU_BLOCK)
    grp = lax.broadcasted_iota(jnp.int32, (lc, GMLP_WIDTH), 1) // (GMLP_WIDTH // GMLP_GROUPS)
    wmask = [jnp.where(causal, ws_ref[g], 0.0).astype(bf16) for g in range(GMLP_GROUPS)]
    ob = []
    for c in range(tn // lc):
        vc = vn_ref[0, c * lc:(c + 1) * lc, :].astype(bf16)
        s = bs_ref[...]
        for g in range(GMLP_GROUPS):
            sg = jnp.dot(wmask[g], vc, preferred_element_type=f32)
            s = s + jnp.where(grp == g, sg, 0.0)
        ob.append(u_ref[0, c * lc:(c + 1) * lc, :] * s)
    o_b = ob[0] if len(ob) == 1 else jnp.concatenate(ob, axis=0)
    zp = zp_ref[0]
    ext = jnp.concatenate([prev_ref[0, 0], zp], axis=0)
    s2 = ext + pltpu.roll(ext, 1, 0)
    s4 = s2 + pltpu.roll(s2, 2, 0)
    s8 = s4 + pltpu.roll(s4, 4, 0)
    s16 = s8 + pltpu.roll(s8, 8, 0)
    pgrp = lax.broadcasted_iota(jnp.int32, (tn, POOL_WIDTH), 1) // (POOL_WIDTH // POOL_GROUPS)
    pos = pos0 + j * tn + lax.broadcasted_iota(jnp.int32, (tn, POOL_WIDTH), 0)
    win = jnp.where(pgrp == 0, s2[HALO:], jnp.where(pgrp == 1, s4[HALO:], jnp.where(pgrp == 2, s8[HALO:], s16[HALO:])))
    width = jnp.where(pgrp == 0, 2, jnp.where(pgrp == 1, 4, jnp.where(pgrp == 2, 8, 16)))
    cnt = jnp.minimum(pos + 1, width).astype(f32)
    dlt = win / cnt - zp
    o_c = jnp.dot(dlt.astype(bf16), wpool_ref[...], preferred_element_type=f32) * pscale_ref[...]
    ya = jnp.dot(oa_ref[0], wbr_ref[:FOX_WIDTH, :], preferred_element_type=f32)
    yb = jnp.dot(o_b.astype(bf16), wbr_ref[FOX_WIDTH:FOX_WIDTH + GMLP_WIDTH, :], preferred_element_type=f32)
    yc = jnp.dot(o_c.astype(bf16), wbr_ref[FOX_WIDTH + GMLP_WIDTH:, :], preferred_element_type=f32)
    merged = gate_ref[0, :, :d] * ya + gate_ref[0, :, d:2 * d] * yb + gate_ref[0, :, 2 * d:] * yc
    o_ref[0] = x_ref[0] + jnp.dot(merged.astype(bf16), wout_ref[...], preferred_element_type=f32)


def _merge(x, oa, u, vn, zp, prev, gate, w, l, *, tn, lc, pos0):
    nseq, L, d = x.shape
    seq = lambda width: pl.BlockSpec((1, tn, width), lambda b_, j: (b_, j, 0))
    weights = [w["ws"], w["bs"], w["wpool"], w["pscale"], w["wbr"], w["wout"]]
    return pl.pallas_call(
        functools.partial(_merge_body, tn=tn, lc=lc, pos0=pos0),
        grid=(nseq, L // tn),
        in_specs=[seq(d), seq(FOX_WIDTH), seq(GMLP_WIDTH), seq(GMLP_WIDTH), seq(POOL_WIDTH),
                  pl.BlockSpec((1, 1, HALO, POOL_WIDTH), lambda b_, j: (b_, j, 0, 0)), seq(N_BRANCH * d)]
        + [_layer_spec(a, l) for a in weights],
        out_specs=seq(d),
        out_shape=jax.ShapeDtypeStruct(x.shape, f32),
        compiler_params=_params("parallel", "parallel"),
        name="merge",
    )(x, oa, u, vn, zp, prev, gate, *weights)


def _merge_sort_pairs(n):
    def merge(lo, hi, r):
        step = r * 2
        if step < hi - lo:
            yield from merge(lo, hi, step)
            yield from merge(lo + r, hi, step)
            yield from [(i, i + r) for i in range(lo + r, hi - r, step)]
        else:
            yield (lo, lo + r)

    def sort(lo, hi):
        if hi - lo >= 1:
            mid = lo + (hi - lo) // 2
            yield from sort(lo, mid)
            yield from sort(mid + 1, hi)
            yield from merge(lo, hi, 1)

    return list(sort(0, n - 1))


def _top_values_sorted(block, k):
    n = block.shape[0] // SUBLANES
    groups = [block[g * SUBLANES:(g + 1) * SUBLANES, :] for g in range(n)]
    size = 1
    while size < n:
        size *= 2
    groups += [None] * (size - n)
    for i, j in _merge_sort_pairs(size):
        a, b = groups[i], groups[j]
        if a is None:
            groups[i], groups[j] = b, None
        elif b is not None:
            groups[i], groups[j] = jnp.maximum(a, b), jnp.minimum(a, b)
    groups = groups[:n]
    tops = []
    for rnd in range(k):
        m = jnp.max(groups[0], axis=0, keepdims=True)
        tops.append(m)
        hit = groups[0] == m
        depth = min(n, k - rnd)
        for g in range(depth - 1):
            groups[g] = jnp.where(hit, groups[g + 1], groups[g])
        groups[depth - 1] = jnp.where(hit, NEG_INF, groups[depth - 1])
    return tops


def _route_body(x_ref, g2_ref, wqt_ref, keys_ref, ht_ref, sc_ref, st_ref, top_ref, *, tn):
    h = _rms(x_ref[...], g2_ref[...]).astype(bf16)
    ht = h.T
    ht_ref[...] = ht
    qt = jnp.dot(wqt_ref[...], ht, preferred_element_type=f32).astype(bf16)
    n_sets = 2 * PEER_HEADS
    for r in range(n_sets):
        sc_ref[r * N_KEYS:(r + 1) * N_KEYS, :] = jnp.dot(
            keys_ref[r], qt[r * PEER_HALF:(r + 1) * PEER_HALF, :], preferred_element_type=f32)

    kk = PEER_TOPK + 1

    def half_top(r, _):
        base = pl.multiple_of(r * N_KEYS, N_KEYS)
        for tb in range(tn // LANES):
            blk = sc_ref[pl.ds(base, N_KEYS), tb * LANES:(tb + 1) * LANES]
            tops = _top_values_sorted(blk, kk)
            top_ref[r, :, tb * LANES:(tb + 1) * LANES] = jnp.concatenate(
                tops + [tops[-1]] * (TOP_ROWS - kk), axis=0)
        return 0

    lax.fori_loop(0, n_sets, half_top, 0)

    def head_stats(hd, _):
        for tb in range(tn // LANES):
            sl = slice(tb * LANES, (tb + 1) * LANES)
            a = top_ref[2 * hd, :, sl]
            b = top_ref[2 * hd + 1, :, sl]
            rows8 = lax.broadcasted_iota(jnp.int32, (SUBLANES, LANES), 0)
            never = jnp.full((SUBLANES, LANES), NEG_INF, f32)
            blocks = [(a[0:1], b[0:SUBLANES]), (a[0:1], b[SUBLANES:2 * SUBLANES])]
            for p in range(1, SUBLANES):
                blocks.append((a[p:p + 1], jnp.where(rows8 < kk // (p + 1), b[0:SUBLANES], never)))
            blocks.append((a[SUBLANES:2 * SUBLANES], b[0:1]))
            blocks.append((jnp.where(rows8 == 0, a[PEER_TOPK:PEER_TOPK + 1], a[0:1]),
                           jnp.where(rows8 == 0, b[0:1], jnp.where(rows8 == 1, b[PEER_TOPK:PEER_TOPK + 1], never))))
            cand = jnp.concatenate([ab + bb for ab, bb in blocks], axis=0)
            tops = _top_values_sorted(cand, kk)
            tau = 0.5 * (tops[-2] + tops[-1])
            z = jnp.zeros((1, LANES), f32)
            for ab, bb in blocks:
                kept = bb >= tau - ab
                z = z + jnp.sum(jnp.where(kept, jnp.exp((ab - a[0:1]) + (bb - b[0:1])), 0.0), axis=0, keepdims=True)
            st_ref[hd, :, sl] = jnp.concatenate(
                [a[0:1], b[0:1], tau, 0.5 / z, jnp.zeros((SUBLANES - 4, LANES), f32)], axis=0)
        return 0

    lax.fori_loop(0, PEER_HEADS, head_stats, 0)


def _route(x, w, l, tn):
    n, d = x.shape
    n_sets = 2 * PEER_HEADS
    return pl.pallas_call(
        functools.partial(_route_body, tn=tn),
        grid=(n // tn,),
        in_specs=[pl.BlockSpec((tn, d), lambda i: (i, 0)), _layer_spec(w["g2"], l), _layer_spec(w["wqt"], l),
                  _layer_spec(w["keys"], l)],
        out_specs=[pl.BlockSpec((d, tn), lambda i: (0, i)), pl.BlockSpec((n_sets * N_KEYS, tn), lambda i: (0, i)),
                   pl.BlockSpec((PEER_HEADS, SUBLANES, tn), lambda i: (0, 0, i))],
        out_shape=[jax.ShapeDtypeStruct((d, n), bf16), jax.ShapeDtypeStruct((n_sets * N_KEYS, n), f32),
                   jax.ShapeDtypeStruct((PEER_HEADS, SUBLANES, n), f32)],
        scratch_shapes=[pltpu.VMEM((n_sets, TOP_ROWS, tn), f32)],
        compiler_params=_params("parallel"),
        name="peer_route",
    )(x, w["g2"], w["wqt"], w["keys"])


def _n_halves(tn):
    return 2 if tn % (2 * LANES) == 0 else 1


def _experts_body(x_ref, ht_ref, sc_ref, st_ref, pu_ref, pvc_ref, o_ref, q_ref, act0_ref, act1_ref, acc_ref, pvp_ref,
                  *, tn, ec):
    c = pl.program_id(1)
    n_sub = ec // N_KEYS
    n_half = _n_halves(tn)
    half = tn // n_half
    acts = (act0_ref, act1_ref)

    @pl.when(c == 0)
    def _():
        for hd in range(PEER_HEADS):
            s2 = sc_ref[(2 * hd + 1) * N_KEYS:(2 * hd + 2) * N_KEYS, :]
            q_ref[hd] = jnp.exp(s2 - st_ref[hd, 1:2, :]) * st_ref[hd, 3:4, :]
        acc_ref[...] = jnp.zeros_like(acc_ref)
        acts[n_half - 1][...] = jnp.zeros_like(acts[n_half - 1])
        pvp_ref[...] = jnp.zeros_like(pvp_ref)

    i0 = pl.multiple_of(c * n_sub, n_sub)

    def hidden(hf):
        rows = HIDDEN_KEYS * N_KEYS
        return [jnp.dot(pu_ref[p * rows:(p + 1) * rows, :], ht_ref[:, hf * half:(hf + 1) * half],
                        preferred_element_type=f32) for p in range(ec // rows)]

    def values(pv_ref, hf):
        acc_ref[:, hf * half:(hf + 1) * half] += jnp.dot(pv_ref[...], acts[hf][...], preferred_element_type=f32)

    def weigh(hf, a):
        for tb in range(half // LANES):
            lanes = slice(hf * half + tb * LANES, hf * half + (tb + 1) * LANES)
            thr_rows, p_rows = [], []
            for hd in range(PEER_HEADS):
                s1 = sc_ref[pl.ds(2 * hd * N_KEYS + i0, n_sub), lanes]
                thr_rows.append(st_ref[hd, 2:3, lanes] - s1)
                p_rows.append(jnp.exp(s1 - st_ref[hd, 0:1, lanes]))
            for ii in range(n_sub):
                gsum = jnp.zeros((N_KEYS, LANES), f32)
                for hd in range(PEER_HEADS):
                    s2 = sc_ref[(2 * hd + 1) * N_KEYS:(2 * hd + 2) * N_KEYS, lanes]
                    kept = s2 >= thr_rows[hd][ii:ii + 1]
                    gsum = gsum + jnp.where(kept, q_ref[hd, :, lanes], 0.0) * p_rows[hd][ii:ii + 1]
                part, sub = divmod(ii, HIDDEN_KEYS)
                aa = a[part][sub * N_KEYS:(sub + 1) * N_KEYS, tb * LANES:(tb + 1) * LANES]
                acts[hf][ii * N_KEYS:(ii + 1) * N_KEYS, tb * LANES:(tb + 1) * LANES] = (
                    (aa + aa * lax.erf(aa * 0.7071067811865476)) * gsum).astype(bf16)

    a0 = hidden(0)
    values(pvp_ref, n_half - 1)
    weigh(0, a0)
    if n_half == 2:
        a1 = hidden(1)
        values(pvc_ref, 0)
        weigh(1, a1)
    pvp_ref[...] = pvc_ref[...]

    @pl.when(c == pl.num_programs(1) - 1)
    def _():
        values(pvc_ref, n_half - 1)
        o_ref[...] = x_ref[...] + acc_ref[...].T


def _experts(x, ht, sc, st, w, l, *, tn, ec):
    n, d = x.shape
    n_exp = w["pu"].shape[1]
    n_sets = 2 * PEER_HEADS
    return pl.pallas_call(
        functools.partial(_experts_body, tn=tn, ec=ec),
        grid=(n // tn, n_exp // ec),
        in_specs=[
            pl.BlockSpec((tn, d), lambda t, c: (t, 0)),
            pl.BlockSpec((d, tn), lambda t, c: (0, t)),
            pl.BlockSpec((n_sets * N_KEYS, tn), lambda t, c: (0, t)),
            pl.BlockSpec((PEER_HEADS, SUBLANES, tn), lambda t, c: (0, 0, t)),
            pl.BlockSpec((None, ec, d), lambda t, c: (l, c, 0)),
            pl.BlockSpec((None, d, ec), lambda t, c: (l, 0, c)),
        ],
        out_specs=pl.BlockSpec((tn, d), lambda t, c: (t, 0)),
        out_shape=jax.ShapeDtypeStruct((n, d), f32),
        scratch_shapes=[
            pltpu.VMEM((PEER_HEADS, N_KEYS, tn), f32),
            pltpu.VMEM((ec, tn // _n_halves(tn)), bf16), pltpu.VMEM((ec, tn // _n_halves(tn)), bf16),
            pltpu.VMEM((d, tn), f32), pltpu.VMEM((d, ec), bf16),
        ],
        compiler_params=_params("parallel", "arbitrary"),
        name="peer_experts",
    )(x, ht, sc, st, w["pu"], w["pvt"])


def _final_body(x_ref, g_ref, o_ref):
    o_ref[...] = _rms(x_ref[...], g_ref[...])


def _final_norm(x, g, tn):
    n, d = x.shape
    spec = pl.BlockSpec((tn, d), lambda i: (i, 0))
    return pl.pallas_call(
        _final_body, grid=(n // tn,), in_specs=[spec, _const_spec(g.shape)], out_specs=spec,
        out_shape=jax.ShapeDtypeStruct(x.shape, f32), compiler_params=_params("parallel"), name="final_norm",
    )(x, g)


def _gating_weights(w_s, b_s, lc):
    gdim = GMLP_WIDTH // GMLP_GROUPS
    return w_s[:, :, :lc, :lc], jnp.repeat(jnp.transpose(b_s[:, :, :lc], (0, 2, 1)), gdim, axis=2)


def _all_weights(norm1_g, w_in, b_f, ln_g, ln_b, w_s, b_s, w_pool, pool_scale, w_br, w_out, norm2_g, peer_wq,
                 peer_keys, peer_u, peer_v):
    depth, d = w_in.shape[:2]
    o_f = 3 * FOX_WIDTH
    o_u = o_f + FOX_HEADS
    o_g = o_u + 2 * GMLP_WIDTH + POOL_WIDTH
    wf = jnp.zeros((depth, d, LANES), f32).at[:, :, :FOX_HEADS].set(w_in[:, :, o_f:o_u])
    bf = jnp.zeros((depth, 1, LANES), f32).at[:, 0, :FOX_HEADS].set(b_f)
    pdim = POOL_WIDTH // POOL_GROUPS
    wpool = jnp.zeros((depth, POOL_WIDTH, POOL_WIDTH), f32)
    for g in range(POOL_GROUPS):
        wpool = wpool.at[:, g * pdim:(g + 1) * pdim, g * pdim:(g + 1) * pdim].set(w_pool[:, g])
    spare = LANES - HEAD_DIM

    def head_lanes(cols):
        cols = cols.reshape(depth, d, FOX_HEADS, HEAD_DIM)
        return jnp.pad(cols, ((0, 0), (0, 0), (0, 0), (0, spare))).reshape(depth, d, ATTN_WIDTH)

    qc = jnp.zeros((FOX_HEADS, LANES), f32).at[:, HEAD_DIM:HEAD_DIM + C_PIECES].set(-1.0).reshape(1, 1, ATTN_WIDTH)
    wqk = jnp.concatenate([head_lanes(w_in[:, :, :FOX_WIDTH]), head_lanes(w_in[:, :, FOX_WIDTH:2 * FOX_WIDTH])], axis=2)
    ws, bs = _gating_weights(w_s, b_s, GMLP_CHUNK)
    return {
        "g1": norm1_g[:, None], "wqk": wqk.astype(bf16), "qc": jnp.broadcast_to(qc, (depth, 1, ATTN_WIDTH)),
        "wkv": w_in[:, :, FOX_WIDTH:o_f].astype(bf16), "wf": wf.astype(bf16), "bf": bf,
        "wugp": w_in[:, :, o_u:o_g].astype(bf16), "lng": ln_g[:, None], "lnb": ln_b[:, None],
        "wg": w_in[:, :, o_g:].astype(bf16),
        "ws": ws, "bs": bs, "wpool": wpool.astype(bf16), "pscale": pool_scale[:, None],
        "wbr": w_br.astype(bf16), "wout": w_out.astype(bf16),
        "g2": norm2_g[:, None], "wqt": jnp.transpose(peer_wq, (0, 2, 1)).astype(bf16),
        "keys": peer_keys.reshape(depth, 2 * PEER_HEADS, N_KEYS, PEER_HALF).astype(bf16),
        "pu": peer_u.astype(bf16), "pvt": jnp.transpose(peer_v, (0, 2, 1)).astype(bf16),
    }


def _pick(n, prefs):
    for t in prefs:
        if n % t == 0:
            return t
    return n


def _layer(x, w, l, *, k_hist, v_hist, logf_hist, pool_hist, lc, tiles):
    nseq, L, d = x.shape
    n = nseq * L
    q, k, v, kb, vb, logf, u, vn, zp, gate = _inproj(x.reshape(n, d), w, l, tiles["inproj"])
    r3 = lambda a: a.reshape(nseq, L, a.shape[-1])
    q, kb, vb, logf3, u, vn, zp, gate = map(r3, (q, kb, vb, logf, u, vn, zp, gate))
    past = 0 if k_hist is None else k_hist.shape[1]
    spare = LANES - HEAD_DIM
    if past:
        k_hist = jnp.pad(k_hist.astype(bf16), ((0, 0), (0, 0), (0, 0), (0, spare)))
        kb = jnp.concatenate([k_hist.reshape(nseq, past, ATTN_WIDTH), kb], axis=1)
        vb = jnp.concatenate([v_hist.reshape(nseq, past, FOX_WIDTH).astype(bf16), vb], axis=1)
        lf_all = jnp.concatenate([logf_hist.astype(f32), logf3], axis=1)
    else:
        lf_all = logf3
    lk = past + L
    tk = tiles["tk"]
    lk_pad = -(-lk // tk) * tk
    if lk_pad != lk:
        padk = ((0, 0), (0, lk_pad - lk), (0, 0))
        kb, vb, lf_all = jnp.pad(kb, padk), jnp.pad(vb, padk), jnp.pad(lf_all, padk)
    nh = ATTN_HEADS_PER_STEP
    c_t, c_hi, c_mid, c_lo = _cumsum_lanes(jnp.transpose(lf_all, (0, 2, 1)))
    cq = jnp.transpose(c_t.reshape(nseq, FOX_HEADS // nh, nh, lk_pad)[:, :, :, past:past + L], (0, 1, 3, 2))
    hg = FOX_HEADS // nh
    pieces = jnp.stack([c_hi, c_mid, c_lo], axis=-1).reshape(nseq, hg, nh, lk_pad, C_PIECES)
    pieces = jnp.transpose(pieces, (0, 3, 1, 2, 4)).reshape(nseq, lk_pad, hg, nh * C_PIECES)
    pieces = jnp.pad(pieces, ((0, 0), (0, 0), (0, 0), (0, LANES - nh * C_PIECES))).reshape(nseq, lk_pad, hg * LANES)
    src = jnp.arange(LANES)[:, None]
    dst = jnp.arange(nh * LANES)[None, :]
    place = ((dst // LANES == src // C_PIECES) & (dst % LANES == HEAD_DIM + src % C_PIECES)
             & (src < nh * C_PIECES)).astype(bf16)
    o_a = _attention(q, kb, pieces, place, vb, cq, tq=tiles["tq"], tk=tk, q_off=past)
    tn = tiles["merge"]
    nt = L // tn
    first = jnp.zeros((nseq, HALO, POOL_WIDTH), f32)
    if pool_hist is not None:
        first = first.at[:, HALO - POOL_HIST:].set(pool_hist.astype(f32))
    tails = zp.reshape(nseq, nt, tn, POOL_WIDTH)[:, :nt - 1, tn - HALO:, :]
    prev = jnp.concatenate([first[:, None], tails], axis=1)
    x_mid = _merge(x, o_a, u, vn, zp, prev, gate, w, l, tn=tn, lc=lc, pos0=past)
    ht, sc, st = _route(x_mid.reshape(n, d), w, l, tiles["route"])
    x_new = _experts(x_mid.reshape(n, d), ht, sc, st, w, l, tn=tiles["experts"], ec=tiles["ec"])
    new_pool = zp[:, L - POOL_HIST:]
    return x_new.reshape(nseq, L, d), r3(k), r3(v), logf3, new_pool, vn


def kernel(x_prompt, x_sample, cache_fox_k, cache_fox_v, cache_fox_logf, state_pool, norm1_g, w_in, b_f, ln_g, ln_b,
           w_s, b_s, w_pool, pool_scale, w_br, w_out, norm2_g, peer_wq, peer_keys, peer_u, peer_v, final_g):
    depth = w_in.shape[0]
    bp, sp, d = x_prompt.shape
    bd, ls, _ = x_sample.shape
    assert sp % GMLP_CHUNK == 0 and ls <= GMLP_CHUNK and ls >= POOL_HIST and ls % 16 == 0
    params = (norm1_g, w_in, b_f, ln_g, ln_b, w_s, b_s, w_pool, pool_scale, w_br, w_out, norm2_g, peer_wq, peer_keys,
              peer_u, peer_v)
    tiles_p = {"inproj": _pick(bp * sp, (256,)), "tq": _pick(sp, (512, 256, 128)), "tk": _pick(sp, (1024, 512, 256, 128)),
               "merge": _pick(sp, (256, 128)), "route": _pick(bp * sp, (512, 256, 128)),
               "experts": _pick(bp * sp, (512, 256, 128)), "ec": EXPERT_CHUNK}
    ns = bd * ls
    keys_s = -(-(cache_fox_k.shape[2] + ls) // LANES) * LANES
    tiles_s = {"inproj": ns, "tq": ls, "tk": keys_s, "merge": ls, "route": ns, "experts": ns, "ec": EXPERT_CHUNK}
    xp, xs = x_prompt, x_sample
    outs_p, outs_s = [], []
    wp = _all_weights(*params)
    wsm = dict(wp)
    wsm["ws"], wsm["bs"] = _gating_weights(w_s, b_s, ls)
    for l in range(depth):
        xp, k, v, lf, pool, _ = _layer(xp, wp, l, k_hist=None, v_hist=None, logf_hist=None, pool_hist=None,
                                       lc=GMLP_CHUNK, tiles=tiles_p)
        outs_p.append((k.reshape(bp, sp, FOX_HEADS, HEAD_DIM), v.reshape(bp, sp, FOX_HEADS, HEAD_DIM), lf, pool))
        xs, k, v, lf, pool, vn = _layer(xs, wsm, l, k_hist=cache_fox_k[l], v_hist=cache_fox_v[l],
                                        logf_hist=cache_fox_logf[l], pool_hist=state_pool[l], lc=ls, tiles=tiles_s)
        outs_s.append((k.reshape(bd, ls, FOX_HEADS, HEAD_DIM), v.reshape(bd, ls, FOX_HEADS, HEAD_DIM), lf, pool, vn))
    g = final_g[None]
    y_prompt = _final_norm(xp.reshape(bp * sp, d), g, _pick(bp * sp, (512, 256, 128))).reshape(bp, sp, d)
    y_sample = _final_norm(xs.reshape(ns, d), g, ns).reshape(bd, ls, d)
    stack = lambda outs, i: jnp.stack([o[i] for o in outs])
    return (y_prompt, y_sample, stack(outs_p, 0), stack(outs_p, 1), stack(outs_p, 2), stack(outs_p, 3),
            stack(outs_s, 0), stack(outs_s, 1), stack(outs_s, 2), stack(outs_s, 3), stack(outs_s, 4))
```

```python
import functools

import jax
import jax.numpy as jnp
from jax import lax
from jax.experimental import pallas as pl
from jax.experimental.pallas import tpu as pltpu

f32 = jnp.float32
bf16 = jnp.bfloat16

EPS = 1e-6
FOX_HEADS = 8
HEAD_DIM = 64
FOX_WIDTH = FOX_HEADS * HEAD_DIM
GMLP_GROUPS = 4
GMLP_WIDTH = 256
GMLP_CHUNK = 128
SGU_BLOCK = 64
POOL_GROUPS = 4
POOL_WIDTH = 256
POOL_WINDOWS = (2, 4, 8, 16)
POOL_HIST = 15
HALO = 16
N_BRANCH = 3
PEER_HEADS = 8
N_KEYS = 128
PEER_TOPK = 16
PEER_HALF = 128

LANES = 128
SUBLANES = 8
EXPERT_CHUNK = SUBLANES * N_KEYS
TOP_ROWS = 3 * SUBLANES
VMEM_LIMIT = 56 * 1024 * 1024
NEG_INF = float("-inf")
LOG2E = 1.4426950408889634
C_PIECES = 3
ATTN_HEADS_PER_STEP = 4
ATTN_WIDTH = FOX_HEADS * LANES


def _params(*sem):
    return pltpu.CompilerParams(dimension_semantics=sem, vmem_limit_bytes=VMEM_LIMIT)


def _const_spec(shape):
    nd = len(shape)
    return pl.BlockSpec(shape, lambda *_: (0,) * nd)


def _layer_spec(a, l):
    nd = a.ndim - 1
    return pl.BlockSpec((None,) + a.shape[1:], lambda *_: (l,) + (0,) * nd)


def _rms(x, g):
    return x * lax.rsqrt(jnp.mean(x * x, axis=-1, keepdims=True) + EPS) * g


def _gelu(x):
    return 0.5 * x * (1.0 + lax.erf(x * 0.7071067811865476))


def _inproj_body(x_ref, g1_ref, wqk_ref, qc_ref, wkv_ref, wf_ref, bf_ref, wugp_ref, lng_ref, lnb_ref, wg_ref,
                 q_ref, k_ref, v_ref, kb_ref, vb_ref, lf_ref, u_ref, vn_ref, zp_ref, gate_ref):
    h = _rms(x_ref[...], g1_ref[...]).astype(bf16)
    zqk = jnp.dot(h, wqk_ref[...], preferred_element_type=f32)
    q_ref[...] = (zqk[:, :ATTN_WIDTH] * (HEAD_DIM ** -0.5 * LOG2E) + qc_ref[...]).astype(bf16)
    kb_ref[...] = zqk[:, ATTN_WIDTH:].astype(bf16)
    zkv = jnp.dot(h, wkv_ref[...], preferred_element_type=f32)
    v = zkv[:, FOX_WIDTH:]
    k_ref[...] = zkv[:, :FOX_WIDTH]
    v_ref[...] = v
    vb_ref[...] = v.astype(bf16)
    zf = jnp.dot(h, wf_ref[...], preferred_element_type=f32) + bf_ref[...]
    lf_ref[...] = jax.nn.log_sigmoid(zf)[:, :FOX_HEADS]
    zugp = jnp.dot(h, wugp_ref[...], preferred_element_type=f32)
    u_ref[...] = _gelu(zugp[:, :GMLP_WIDTH])
    gv = _gelu(zugp[:, GMLP_WIDTH:2 * GMLP_WIDTH])
    mu = jnp.mean(gv, axis=-1, keepdims=True)
    gc = gv - mu
    vn_ref[...] = gc * lax.rsqrt(jnp.mean(gc * gc, axis=-1, keepdims=True) + EPS) * lng_ref[...] + lnb_ref[...]
    zp_ref[...] = zugp[:, 2 * GMLP_WIDTH:]
    zg = jnp.dot(h, wg_ref[...], preferred_element_type=f32)
    gate_ref[...] = (1.0 / (1.0 + jnp.exp(-zg))).astype(bf16)


def _inproj(x, w, l, tn):
    n, d = x.shape
    row = lambda width: pl.BlockSpec((tn, width), lambda i: (i, 0))
    outs = [
        (ATTN_WIDTH, bf16), (FOX_WIDTH, f32), (FOX_WIDTH, f32), (ATTN_WIDTH, bf16), (FOX_WIDTH, bf16),
        (FOX_HEADS, f32), (GMLP_WIDTH, f32), (GMLP_WIDTH, f32), (POOL_WIDTH, f32), (N_BRANCH * d, bf16),
    ]
    weights = [w["g1"], w["wqk"], w["qc"], w["wkv"], w["wf"], w["bf"], w["wugp"], w["lng"], w["lnb"], w["wg"]]
    return pl.pallas_call(
        _inproj_body,
        grid=(n // tn,),
        in_specs=[row(d)] + [_layer_spec(a, l) for a in weights],
        out_specs=[row(width) for width, _ in outs],
        out_shape=[jax.ShapeDtypeStruct((n, width), dt) for width, dt in outs],
        compiler_params=_params("parallel"),
        name="inproj",
    )(x, *weights)


def _cumsum_body(x_ref, c_ref, hi_ref, mid_ref, lo_ref):
    x = x_ref[0]
    n = x.shape[-1]
    lane = lax.broadcasted_iota(jnp.int32, x.shape, 1)
    shift = 1
    while shift < n:
        x = x + jnp.where(lane >= shift, pltpu.roll(x, shift, 1), 0.0)
        shift *= 2
    c = x * LOG2E
    c_ref[0] = c
    hi = c.astype(bf16)
    r1 = c - hi.astype(f32)
    mid = r1.astype(bf16)
    hi_ref[0] = hi
    mid_ref[0] = mid
    lo_ref[0] = (r1 - mid.astype(f32)).astype(bf16)


def _cumsum_lanes(x):
    b, r, n = x.shape
    spec = pl.BlockSpec((1, r, n), lambda i: (i, 0, 0))
    return pl.pallas_call(
        _cumsum_body, grid=(b,), in_specs=[spec], out_specs=[spec] * 4,
        out_shape=[jax.ShapeDtypeStruct(x.shape, f32)] + [jax.ShapeDtypeStruct(x.shape, bf16)] * 3,
        compiler_params=_params("parallel"), name="cumsum",
    )(x)


def _attn_body(q_ref, k_ref, pc_ref, place_ref, v_ref, cq_ref, o_ref, ka_ref, *, tq, tk, q_off):
    qi = pl.program_id(2)
    lane = lax.broadcasted_iota(jnp.int32, (tq, LANES), 1)
    row0 = q_off + qi * tq
    n_full = (row0 + 1) // tk
    n_all = (row0 + tq + tk - 1) // tk
    nh = ATTN_HEADS_PER_STEP
    lk = k_ref.shape[1]

    @pl.when(qi == 0)
    def _():
        def place(t, _):
            rows = pl.ds(pl.multiple_of(t * tk, tk), tk)
            moved = jnp.dot(pc_ref[0, rows, :], place_ref[...], preferred_element_type=f32)
            ka_ref[rows, :] = (k_ref[0, rows, :].astype(f32) + moved).astype(bf16)
            return 0

        lax.fori_loop(0, lk // tk, place, 0)

    qs = [q_ref[0, :, hh * LANES:(hh + 1) * LANES] for hh in range(nh)]
    cqs = [jnp.broadcast_to(cq_ref[0, 0, :, hh:hh + 1], (tq, LANES)) for hh in range(nh)]
    n_col = tk // LANES
    lane_k = lax.broadcasted_iota(jnp.int32, (tk, LANES), 1)

    def step(t, carry, masked):
        start = pl.multiple_of(t * tk, tk)
        if masked:
            kpos = start + lax.broadcasted_iota(jnp.int32, (tq, tk), 1)
            qpos = row0 + lax.broadcasted_iota(jnp.int32, (tq, tk), 0)
            allowed = kpos <= qpos
        out = []
        for hh in range(nh):
            m, acc = carry[hh]
            ks = ka_ref[pl.ds(start, tk), hh * LANES:(hh + 1) * LANES]
            vs = v_ref[0, pl.ds(start, tk), (hh // 2) * LANES:(hh // 2 + 1) * LANES]
            own = (lane_k < HEAD_DIM) if hh % 2 == 0 else (lane_k >= HEAD_DIM)
            vs = jnp.where(own, vs, jnp.ones((), bf16))
            s = lax.dot_general(qs[hh], ks, (((1,), (1,)), ((), ())), preferred_element_type=f32)
            if masked:
                s = jnp.where(allowed, s, NEG_INF)
            cols = [s[:, j * LANES:(j + 1) * LANES] for j in range(n_col)]
            smax = functools.reduce(jnp.maximum, cols)
            m_new = jnp.maximum(m, jnp.max(smax, axis=1, keepdims=True) + cqs[hh])
            alpha = jnp.exp2(m - m_new)
            r = cqs[hh] - m_new
            p = jnp.concatenate([jnp.exp2(col + r).astype(bf16) for col in cols], axis=1)
            acc = alpha * acc + jnp.dot(p, vs, preferred_element_type=f32)
            out.append((m_new, acc))
        return tuple(out)

    one = (jnp.full((tq, LANES), NEG_INF, f32), jnp.zeros((tq, LANES), f32))
    carry = lax.fori_loop(0, n_full, functools.partial(step, masked=False), (one,) * nh)
    carry = lax.fori_loop(n_full, n_all, functools.partial(step, masked=True), carry)
    outs = []
    for hh, (_, acc) in enumerate(carry):
        sums = (lane >= HEAD_DIM) if hh % 2 == 0 else (lane < HEAD_DIM)
        outs.append(acc / jnp.max(jnp.where(sums, acc, 0.0), axis=1, keepdims=True))
    for pr in range(nh // 2):
        o_ref[0, :, pr * LANES:(pr + 1) * LANES] = jnp.where(
            lane < HEAD_DIM, outs[2 * pr], outs[2 * pr + 1]).astype(o_ref.dtype)


def _attention(q, k, pieces, place, v, cq, *, tq, tk, q_off):
    b, lq, _ = q.shape
    lk = k.shape[1]
    nh = ATTN_HEADS_PER_STEP
    return pl.pallas_call(
        functools.partial(_attn_body, tq=tq, tk=tk, q_off=q_off),
        grid=(b, FOX_HEADS // nh, lq // tq),
        in_specs=[
            pl.BlockSpec((1, tq, nh * LANES), lambda b_, h_, i: (b_, i, h_)),
            pl.BlockSpec((1, lk, nh * LANES), lambda b_, h_, i: (b_, 0, h_)),
            pl.BlockSpec((1, lk, LANES), lambda b_, h_, i: (b_, 0, h_)),
            _const_spec(place.shape),
            pl.BlockSpec((1, lk, nh * HEAD_DIM), lambda b_, h_, i: (b_, 0, h_)),
            pl.BlockSpec((1, 1, tq, nh), lambda b_, h_, i: (b_, h_, i, 0)),
        ],
        out_specs=pl.BlockSpec((1, tq, nh * HEAD_DIM), lambda b_, h_, i: (b_, i, h_)),
        out_shape=jax.ShapeDtypeStruct((b, lq, FOX_WIDTH), bf16),
        scratch_shapes=[pltpu.VMEM((lk, nh * LANES), bf16)],
        compiler_params=_params("parallel", "parallel", "arbitrary"),
        name="fox_attention",
    )(q, k, pieces, place, v, cq)


def _merge_body(x_ref, oa_ref, u_ref, vn_ref, zp_ref, prev_ref, gate_ref, ws_ref, bs_ref, wpool_ref, pscale_ref,
                wbr_ref, wout_ref, o_ref, *, tn, lc, pos0):
    j = pl.program_id(1)
    d = x_ref.shape[-1]
    t_io = lax.broadcasted_iota(jnp.int32, (lc, lc), 0)
    s_io = lax.broadcasted_iota(jnp.int32, (lc, lc), 1)
    causal = (s_io // SGU_BLOCK) <= (t_io // SGU_BLOCK)
    grp = lax.broadcasted_iota(jnp.int32, (lc, GMLP_WIDTH), 1) // (GMLP_WIDTH // GMLP_GROUPS)
    wmask = [jnp.where(causal, ws_ref[g], 0.0).astype(bf16) for g in range(GMLP_GROUPS)]
    ob = []
    for c in range(tn // lc):
        vc = vn_ref[0, c * lc:(c + 1) * lc, :].astype(bf16)
        s = bs_ref[...]
        for g in range(GMLP_GROUPS):
            sg = jnp.dot(wmask[g], vc, preferred_element_type=f32)
            s = s + jnp.where(grp == g, sg, 0.0)
        ob.append(u_ref[0, c * lc:(c + 1) * lc, :] * s)
    o_b = ob[0] if len(ob) == 1 else jnp.concatenate(ob, axis=0)
    zp = zp_ref[0]
    ext = jnp.concatenate([prev_ref[0, 0], zp], axis=0)
    s2 = ext + pltpu.roll(ext, 1, 0)
    s4 = s2 + pltpu.roll(s2, 2, 0)
    s8 = s4 + pltpu.roll(s4, 4, 0)
    s16 = s8 + pltpu.roll(s8, 8, 0)
    pgrp = lax.broadcasted_iota(jnp.int32, (tn, POOL_WIDTH), 1) // (POOL_WIDTH // POOL_GROUPS)
    pos = pos0 + j * tn + lax.broadcasted_iota(jnp.int32, (tn, POOL_WIDTH), 0)
    win = jnp.where(pgrp == 0, s2[HALO:], jnp.where(pgrp == 1, s4[HALO:], jnp.where(pgrp == 2, s8[HALO:], s16[HALO:])))
    width = jnp.where(pgrp == 0, 2, jnp.where(pgrp == 1, 4, jnp.where(pgrp == 2, 8, 16)))
    cnt = jnp.minimum(pos + 1, width).astype(f32)
    dlt = win / cnt - zp
    o_c = jnp.dot(dlt.astype(bf16), wpool_ref[...], preferred_element_type=f32) * pscale_ref[...]
    ya = jnp.dot(oa_ref[0], wbr_ref[:FOX_WIDTH, :], preferred_element_type=f32)
    yb = jnp.dot(o_b.astype(bf16), wbr_ref[FOX_WIDTH:FOX_WIDTH + GMLP_WIDTH, :], preferred_element_type=f32)
    yc = jnp.dot(o_c.astype(bf16), wbr_ref[FOX_WIDTH + GMLP_WIDTH:, :], preferred_element_type=f32)
    gates = [gate_ref[0, :, br * d:(br + 1) * d].astype(f32) for br in range(N_BRANCH)]
    merged = gates[0] * ya + gates[1] * yb + gates[2] * yc
    o_ref[0] = x_ref[0] + jnp.dot(merged.astype(bf16), wout_ref[...], preferred_element_type=f32)


def _merge(x, oa, u, vn, zp, prev, gate, w, l, *, tn, lc, pos0):
    nseq, L, d = x.shape
    seq = lambda width: pl.BlockSpec((1, tn, width), lambda b_, j: (b_, j, 0))
    weights = [w["ws"], w["bs"], w["wpool"], w["pscale"], w["wbr"], w["wout"]]
    return pl.pallas_call(
        functools.partial(_merge_body, tn=tn, lc=lc, pos0=pos0),
        grid=(nseq, L // tn),
        in_specs=[seq(d), seq(FOX_WIDTH), seq(GMLP_WIDTH), seq(GMLP_WIDTH), seq(POOL_WIDTH),
                  pl.BlockSpec((1, 1, HALO, POOL_WIDTH), lambda b_, j: (b_, j, 0, 0)), seq(N_BRANCH * d)]
        + [_layer_spec(a, l) for a in weights],
        out_specs=seq(d),
        out_shape=jax.ShapeDtypeStruct(x.shape, f32),
        compiler_params=_params("parallel", "parallel"),
        name="merge",
    )(x, oa, u, vn, zp, prev, gate, *weights)


def _merge_sort_pairs(n):
    def merge(lo, hi, r):
        step = r * 2
        if step < hi - lo:
            yield from merge(lo, hi, step)
            yield from merge(lo + r, hi, step)
            yield from [(i, i + r) for i in range(lo + r, hi - r, step)]
        else:
            yield (lo, lo + r)

    def sort(lo, hi):
        if hi - lo >= 1:
            mid = lo + (hi - lo) // 2
            yield from sort(lo, mid)
            yield from sort(mid + 1, hi)
            yield from merge(lo, hi, 1)

    return list(sort(0, n - 1))


def _top_values_sorted(block, k):
    n = block.shape[0] // SUBLANES
    groups = [block[g * SUBLANES:(g + 1) * SUBLANES, :] for g in range(n)]
    size = 1
    while size < n:
        size *= 2
    groups += [None] * (size - n)
    for i, j in _merge_sort_pairs(size):
        a, b = groups[i], groups[j]
        if a is None:
            groups[i], groups[j] = b, None
        elif b is not None:
            groups[i], groups[j] = jnp.maximum(a, b), jnp.minimum(a, b)
    groups = groups[:n]
    tops = []
    for rnd in range(k):
        m = jnp.max(groups[0], axis=0, keepdims=True)
        tops.append(m)
        hit = groups[0] == m
        depth = min(n, k - rnd)
        for g in range(depth - 1):
            groups[g] = jnp.where(hit, groups[g + 1], groups[g])
        groups[depth - 1] = jnp.where(hit, NEG_INF, groups[depth - 1])
    return tops


def _route_body(x_ref, g2_ref, wqt_ref, keys_ref, ht_ref, sc_ref, st_ref, top_ref, *, tn):
    h = _rms(x_ref[...], g2_ref[...]).astype(bf16)
    ht = h.T
    ht_ref[...] = ht
    qt = jnp.dot(wqt_ref[...], ht, preferred_element_type=f32).astype(bf16)
    n_sets = 2 * PEER_HEADS
    for r in range(n_sets):
        sc_ref[r * N_KEYS:(r + 1) * N_KEYS, :] = jnp.dot(
            keys_ref[r], qt[r * PEER_HALF:(r + 1) * PEER_HALF, :], preferred_element_type=f32)

    kk = PEER_TOPK + 1

    def half_top(r, _):
        base = pl.multiple_of(r * N_KEYS, N_KEYS)
        for tb in range(tn // LANES):
            blk = sc_ref[pl.ds(base, N_KEYS), tb * LANES:(tb + 1) * LANES]
            tops = _top_values_sorted(blk, kk)
            top_ref[r, :, tb * LANES:(tb + 1) * LANES] = jnp.concatenate(
                tops + [tops[-1]] * (TOP_ROWS - kk), axis=0)
        return 0

    lax.fori_loop(0, n_sets, half_top, 0)

    def head_stats(hd, _):
        for tb in range(tn // LANES):
            sl = slice(tb * LANES, (tb + 1) * LANES)
            a = top_ref[2 * hd, :, sl]
            b = top_ref[2 * hd + 1, :, sl]
            rows8 = lax.broadcasted_iota(jnp.int32, (SUBLANES, LANES), 0)
            never = jnp.full((SUBLANES, LANES), NEG_INF, f32)
            blocks = [(a[0:1], b[0:SUBLANES]), (a[0:1], b[SUBLANES:2 * SUBLANES])]
            for p in range(1, SUBLANES):
                blocks.append((a[p:p + 1], jnp.where(rows8 < kk // (p + 1), b[0:SUBLANES], never)))
            blocks.append((a[SUBLANES:2 * SUBLANES], b[0:1]))
            blocks.append((jnp.where(rows8 == 0, a[PEER_TOPK:PEER_TOPK + 1], a[0:1]),
                           jnp.where(rows8 == 0, b[0:1], jnp.where(rows8 == 1, b[PEER_TOPK:PEER_TOPK + 1], never))))
            cand = jnp.concatenate([ab + bb for ab, bb in blocks], axis=0)
            tops = _top_values_sorted(cand, kk)
            tau = 0.5 * (tops[-2] + tops[-1])
            z = jnp.zeros((1, LANES), f32)
            for ab, bb in blocks:
                kept = bb >= tau - ab
                z = z + jnp.sum(jnp.where(kept, jnp.exp((ab - a[0:1]) + (bb - b[0:1])), 0.0), axis=0, keepdims=True)
            st_ref[hd, :, sl] = jnp.concatenate(
                [a[0:1], b[0:1], tau, 0.5 / z, jnp.zeros((SUBLANES - 4, LANES), f32)], axis=0)
        return 0

    lax.fori_loop(0, PEER_HEADS, head_stats, 0)


def _route(x, w, l, tn):
    n, d = x.shape
    n_sets = 2 * PEER_HEADS
    return pl.pallas_call(
        functools.partial(_route_body, tn=tn),
        grid=(n // tn,),
        in_specs=[pl.BlockSpec((tn, d), lambda i: (i, 0)), _layer_spec(w["g2"], l), _layer_spec(w["wqt"], l),
                  _layer_spec(w["keys"], l)],
        out_specs=[pl.BlockSpec((d, tn), lambda i: (0, i)), pl.BlockSpec((n_sets * N_KEYS, tn), lambda i: (0, i)),
                   pl.BlockSpec((PEER_HEADS, SUBLANES, tn), lambda i: (0, 0, i))],
        out_shape=[jax.ShapeDtypeStruct((d, n), bf16), jax.ShapeDtypeStruct((n_sets * N_KEYS, n), f32),
                   jax.ShapeDtypeStruct((PEER_HEADS, SUBLANES, n), f32)],
        scratch_shapes=[pltpu.VMEM((n_sets, TOP_ROWS, tn), f32)],
        compiler_params=_params("parallel"),
        name="peer_route",
    )(x, w["g2"], w["wqt"], w["keys"])


def _n_halves(tn):
    return 2 if tn % (2 * LANES) == 0 else 1


def _experts_body(x_ref, ht_ref, sc_ref, st_ref, pu_ref, pvc_ref, o_ref, q_ref, act0_ref, act1_ref, acc_ref, pvp_ref,
                  *, tn, ec):
    c = pl.program_id(1)
    n_sub = ec // N_KEYS
    n_half = _n_halves(tn)
    half = tn // n_half
    acts = (act0_ref, act1_ref)

    @pl.when(c == 0)
    def _():
        for hd in range(PEER_HEADS):
            s2 = sc_ref[(2 * hd + 1) * N_KEYS:(2 * hd + 2) * N_KEYS, :]
            q_ref[hd] = jnp.exp(s2 - st_ref[hd, 1:2, :]) * st_ref[hd, 3:4, :]
        acc_ref[...] = jnp.zeros_like(acc_ref)
        acts[n_half - 1][...] = jnp.zeros_like(acts[n_half - 1])
        pvp_ref[...] = jnp.zeros_like(pvp_ref)

    i0 = pl.multiple_of(c * n_sub, n_sub)

    def hidden(hf):
        return jnp.dot(pu_ref[...], ht_ref[:, hf * half:(hf + 1) * half], preferred_element_type=f32)

    def values(pv_ref, hf):
        acc_ref[:, hf * half:(hf + 1) * half] += jnp.dot(pv_ref[...], acts[hf][...], preferred_element_type=f32)

    def weigh(hf, a):
        for tb in range(half // LANES):
            lanes = slice(hf * half + tb * LANES, hf * half + (tb + 1) * LANES)
            thr_rows, p_rows = [], []
            for hd in range(PEER_HEADS):
                s1 = sc_ref[pl.ds(2 * hd * N_KEYS + i0, n_sub), lanes]
                thr_rows.append(st_ref[hd, 2:3, lanes] - s1)
                p_rows.append(jnp.exp(s1 - st_ref[hd, 0:1, lanes]))
            for ii in range(n_sub):
                gsum = jnp.zeros((N_KEYS, LANES), f32)
                for hd in range(PEER_HEADS):
                    s2 = sc_ref[(2 * hd + 1) * N_KEYS:(2 * hd + 2) * N_KEYS, lanes]
                    kept = s2 >= thr_rows[hd][ii:ii + 1]
                    gsum = gsum + jnp.where(kept, q_ref[hd, :, lanes], 0.0) * p_rows[hd][ii:ii + 1]
                aa = a[ii * N_KEYS:(ii + 1) * N_KEYS, tb * LANES:(tb + 1) * LANES]
                acts[hf][ii * N_KEYS:(ii + 1) * N_KEYS, tb * LANES:(tb + 1) * LANES] = (
                    (aa + aa * lax.erf(aa * 0.7071067811865476)) * gsum).astype(bf16)

    a0 = hidden(0)
    values(pvp_ref, n_half - 1)
    weigh(0, a0)
    if n_half == 2:
        a1 = hidden(1)
        values(pvc_ref, 0)
        weigh(1, a1)
    pvp_ref[...] = pvc_ref[...]

    @pl.when(c == pl.num_programs(1) - 1)
    def _():
        values(pvc_ref, n_half - 1)
        o_ref[...] = x_ref[...] + acc_ref[...].T


def _experts(x, ht, sc, st, w, l, *, tn, ec):
    n, d = x.shape
    n_exp = w["pu"].shape[1]
    n_sets = 2 * PEER_HEADS
    return pl.pallas_call(
        functools.partial(_experts_body, tn=tn, ec=ec),
        grid=(n // tn, n_exp // ec),
        in_specs=[
            pl.BlockSpec((tn, d), lambda t, c: (t, 0)),
            pl.BlockSpec((d, tn), lambda t, c: (0, t)),
            pl.BlockSpec((n_sets * N_KEYS, tn), lambda t, c: (0, t)),
            pl.BlockSpec((PEER_HEADS, SUBLANES, tn), lambda t, c: (0, 0, t)),
            pl.BlockSpec((None, ec, d), lambda t, c: (l, c, 0)),
            pl.BlockSpec((None, d, ec), lambda t, c: (l, 0, c)),
        ],
        out_specs=pl.BlockSpec((tn, d), lambda t, c: (t, 0)),
        out_shape=jax.ShapeDtypeStruct((n, d), f32),
        scratch_shapes=[
            pltpu.VMEM((PEER_HEADS, N_KEYS, tn), f32),
            pltpu.VMEM((ec, tn // _n_halves(tn)), bf16), pltpu.VMEM((ec, tn // _n_halves(tn)), bf16),
            pltpu.VMEM((d, tn), f32), pltpu.VMEM((d, ec), bf16),
        ],
        compiler_params=_params("parallel", "arbitrary"),
        name="peer_experts",
    )(x, ht, sc, st, w["pu"], w["pvt"])


def _final_body(x_ref, g_ref, o_ref):
    o_ref[...] = _rms(x_ref[...], g_ref[...])


def _final_norm(x, g, tn):
    n, d = x.shape
    spec = pl.BlockSpec((tn, d), lambda i: (i, 0))
    return pl.pallas_call(
        _final_body, grid=(n // tn,), in_specs=[spec, _const_spec(g.shape)], out_specs=spec,
        out_shape=jax.ShapeDtypeStruct(x.shape, f32), compiler_params=_params("parallel"), name="final_norm",
    )(x, g)


def _gating_weights(w_s, b_s, lc):
    gdim = GMLP_WIDTH // GMLP_GROUPS
    return w_s[:, :, :lc, :lc], jnp.repeat(jnp.transpose(b_s[:, :, :lc], (0, 2, 1)), gdim, axis=2)


def _all_weights(norm1_g, w_in, b_f, ln_g, ln_b, w_s, b_s, w_pool, pool_scale, w_br, w_out, norm2_g, peer_wq,
                 peer_keys, peer_u, peer_v):
    depth, d = w_in.shape[:2]
    o_f = 3 * FOX_WIDTH
    o_u = o_f + FOX_HEADS
    o_g = o_u + 2 * GMLP_WIDTH + POOL_WIDTH
    wf = jnp.zeros((depth, d, LANES), f32).at[:, :, :FOX_HEADS].set(w_in[:, :, o_f:o_u])
    bf = jnp.zeros((depth, 1, LANES), f32).at[:, 0, :FOX_HEADS].set(b_f)
    pdim = POOL_WIDTH // POOL_GROUPS
    wpool = jnp.zeros((depth, POOL_WIDTH, POOL_WIDTH), f32)
    for g in range(POOL_GROUPS):
        wpool = wpool.at[:, g * pdim:(g + 1) * pdim, g * pdim:(g + 1) * pdim].set(w_pool[:, g])
    spare = LANES - HEAD_DIM

    def head_lanes(cols):
        cols = cols.reshape(depth, d, FOX_HEADS, HEAD_DIM)
        return jnp.pad(cols, ((0, 0), (0, 0), (0, 0), (0, spare))).reshape(depth, d, ATTN_WIDTH)

    qc = jnp.zeros((FOX_HEADS, LANES), f32).at[:, HEAD_DIM:HEAD_DIM + C_PIECES].set(-1.0).reshape(1, 1, ATTN_WIDTH)
    wqk = jnp.concatenate([head_lanes(w_in[:, :, :FOX_WIDTH]), head_lanes(w_in[:, :, FOX_WIDTH:2 * FOX_WIDTH])], axis=2)
    ws, bs = _gating_weights(w_s, b_s, GMLP_CHUNK)
    return {
        "g1": norm1_g[:, None], "wqk": wqk.astype(bf16), "qc": jnp.broadcast_to(qc, (depth, 1, ATTN_WIDTH)),
        "wkv": w_in[:, :, FOX_WIDTH:o_f].astype(bf16), "wf": wf.astype(bf16), "bf": bf,
        "wugp": w_in[:, :, o_u:o_g].astype(bf16), "lng": ln_g[:, None], "lnb": ln_b[:, None],
        "wg": w_in[:, :, o_g:].astype(bf16),
        "ws": ws, "bs": bs, "wpool": wpool.astype(bf16), "pscale": pool_scale[:, None],
        "wbr": w_br.astype(bf16), "wout": w_out.astype(bf16),
        "g2": norm2_g[:, None], "wqt": jnp.transpose(peer_wq, (0, 2, 1)).astype(bf16),
        "keys": peer_keys.reshape(depth, 2 * PEER_HEADS, N_KEYS, PEER_HALF).astype(bf16),
        "pu": peer_u.astype(bf16), "pvt": jnp.transpose(peer_v, (0, 2, 1)).astype(bf16),
    }


def _pick(n, prefs):
    for t in prefs:
        if n % t == 0:
            return t
    return n


def _layer(x, w, l, *, k_hist, v_hist, logf_hist, pool_hist, lc, tiles):
    nseq, L, d = x.shape
    n = nseq * L
    q, k, v, kb, vb, logf, u, vn, zp, gate = _inproj(x.reshape(n, d), w, l, tiles["inproj"])
    r3 = lambda a: a.reshape(nseq, L, a.shape[-1])
    q, kb, vb, logf3, u, vn, zp, gate = map(r3, (q, kb, vb, logf, u, vn, zp, gate))
    past = 0 if k_hist is None else k_hist.shape[1]
    spare = LANES - HEAD_DIM
    if past:
        k_hist = jnp.pad(k_hist.astype(bf16), ((0, 0), (0, 0), (0, 0), (0, spare)))
        kb = jnp.concatenate([k_hist.reshape(nseq, past, ATTN_WIDTH), kb], axis=1)
        vb = jnp.concatenate([v_hist.reshape(nseq, past, FOX_WIDTH).astype(bf16), vb], axis=1)
        lf_all = jnp.concatenate([logf_hist.astype(f32), logf3], axis=1)
    else:
        lf_all = logf3
    lk = past + L
    tk = tiles["tk"]
    lk_pad = -(-lk // tk) * tk
    if lk_pad != lk:
        padk = ((0, 0), (0, lk_pad - lk), (0, 0))
        kb, vb, lf_all = jnp.pad(kb, padk), jnp.pad(vb, padk), jnp.pad(lf_all, padk)
    nh = ATTN_HEADS_PER_STEP
    c_t, c_hi, c_mid, c_lo = _cumsum_lanes(jnp.transpose(lf_all, (0, 2, 1)))
    cq = jnp.transpose(c_t.reshape(nseq, FOX_HEADS // nh, nh, lk_pad)[:, :, :, past:past + L], (0, 1, 3, 2))
    hg = FOX_HEADS // nh
    pieces = jnp.stack([c_hi, c_mid, c_lo], axis=-1).reshape(nseq, hg, nh, lk_pad, C_PIECES)
    pieces = jnp.transpose(pieces, (0, 3, 1, 2, 4)).reshape(nseq, lk_pad, hg, nh * C_PIECES)
    pieces = jnp.pad(pieces, ((0, 0), (0, 0), (0, 0), (0, LANES - nh * C_PIECES))).reshape(nseq, lk_pad, hg * LANES)
    src = jnp.arange(LANES)[:, None]
    dst = jnp.arange(nh * LANES)[None, :]
    place = ((dst // LANES == src // C_PIECES) & (dst % LANES == HEAD_DIM + src % C_PIECES)
             & (src < nh * C_PIECES)).astype(bf16)
    o_a = _attention(q, kb, pieces, place, vb, cq, tq=tiles["tq"], tk=tk, q_off=past)
    tn = tiles["merge"]
    nt = L // tn
    first = jnp.zeros((nseq, HALO, POOL_WIDTH), f32)
    if pool_hist is not None:
        first = first.at[:, HALO - POOL_HIST:].set(pool_hist.astype(f32))
    tails = zp.reshape(nseq, nt, tn, POOL_WIDTH)[:, :nt - 1, tn - HALO:, :]
    prev = jnp.concatenate([first[:, None], tails], axis=1)
    x_mid = _merge(x, o_a, u, vn, zp, prev, gate, w, l, tn=tn, lc=lc, pos0=past)
    ht, sc, st = _route(x_mid.reshape(n, d), w, l, tiles["route"])
    x_new = _experts(x_mid.reshape(n, d), ht, sc, st, w, l, tn=tiles["experts"], ec=tiles["ec"])
    new_pool = zp[:, L - POOL_HIST:]
    return x_new.reshape(nseq, L, d), r3(k), r3(v), logf3, new_pool, vn


def kernel(x_prompt, x_sample, cache_fox_k, cache_fox_v, cache_fox_logf, state_pool, norm1_g, w_in, b_f, ln_g, ln_b,
           w_s, b_s, w_pool, pool_scale, w_br, w_out, norm2_g, peer_wq, peer_keys, peer_u, peer_v, final_g):
    depth = w_in.shape[0]
    bp, sp, d = x_prompt.shape
    bd, ls, _ = x_sample.shape
    assert sp % GMLP_CHUNK == 0 and ls <= GMLP_CHUNK and ls >= POOL_HIST and ls % 16 == 0
    params = (norm1_g, w_in, b_f, ln_g, ln_b, w_s, b_s, w_pool, pool_scale, w_br, w_out, norm2_g, peer_wq, peer_keys,
              peer_u, peer_v)
    tiles_p = {"inproj": _pick(bp * sp, (256,)), "tq": _pick(sp, (512, 256, 128)), "tk": _pick(sp, (1024, 512, 256, 128)),
               "merge": _pick(sp, (256, 128)), "route": _pick(bp * sp, (512, 256, 128)),
               "experts": _pick(bp * sp, (512, 256, 128)), "ec": EXPERT_CHUNK}
    ns = bd * ls
    keys_s = -(-(cache_fox_k.shape[2] + ls) // LANES) * LANES
    tiles_s = {"inproj": ns, "tq": ls, "tk": keys_s, "merge": ls, "route": ns, "experts": ns, "ec": EXPERT_CHUNK}
    xp, xs = x_prompt, x_sample
    outs_p, outs_s = [], []
    wp = _all_weights(*params)
    wsm = dict(wp)
    wsm["ws"], wsm["bs"] = _gating_weights(w_s, b_s, ls)
    for l in range(depth):
        xp, k, v, lf, pool, _ = _layer(xp, wp, l, k_hist=None, v_hist=None, logf_hist=None, pool_hist=None,
                                       lc=GMLP_CHUNK, tiles=tiles_p)
        outs_p.append((k.reshape(bp, sp, FOX_HEADS, HEAD_DIM), v.reshape(bp, sp, FOX_HEADS, HEAD_DIM), lf, pool))
        xs, k, v, lf, pool, vn = _layer(xs, wsm, l, k_hist=cache_fox_k[l], v_hist=cache_fox_v[l],
                                        logf_hist=cache_fox_logf[l], pool_hist=state_pool[l], lc=ls, tiles=tiles_s)
        outs_s.append((k.reshape(bd, ls, FOX_HEADS, HEAD_DIM), v.reshape(bd, ls, FOX_HEADS, HEAD_DIM), lf, pool, vn))
    g = final_g[None]
    y_prompt = _final_norm(xp.reshape(bp * sp, d), g, _pick(bp * sp, (512, 256, 128))).reshape(bp, sp, d)
    y_sample = _final_norm(xs.reshape(ns, d), g, ns).reshape(bd, ls, d)
    stack = lambda outs, i: jnp.stack([o[i] for o in outs])
    return (y_prompt, y_sample, stack(outs_p, 0), stack(outs_p, 1), stack(outs_p, 2), stack(outs_p, 3),
            stack(outs_s, 0), stack(outs_s, 1), stack(outs_s, 2), stack(outs_s, 3), stack(outs_s, 4))
```

```python
import functools

import jax
import jax.numpy as jnp
from jax import lax
from jax.experimental import pallas as pl
from jax.experimental.pallas import tpu as pltpu

f32 = jnp.float32
bf16 = jnp.bfloat16

EPS = 1e-6
FOX_HEADS = 8
HEAD_DIM = 64
FOX_WIDTH = FOX_HEADS * HEAD_DIM
GMLP_GROUPS = 4
GMLP_WIDTH = 256
GMLP_CHUNK = 128
SGU_BLOCK = 64
POOL_GROUPS = 4
POOL_WIDTH = 256
POOL_WINDOWS = (2, 4, 8, 16)
POOL_HIST = 15
HALO = 16
N_BRANCH = 3
PEER_HEADS = 8
N_KEYS = 128
PEER_TOPK = 16
PEER_HALF = 128

LANES = 128
SUBLANES = 8
EXPERT_CHUNK = SUBLANES * N_KEYS
TOP_ROWS = 3 * SUBLANES
VMEM_LIMIT = 56 * 1024 * 1024
NEG_INF = float("-inf")
LOG2E = 1.4426950408889634
C_PIECES = 3
ATTN_HEADS_PER_STEP = 4
ATTN_WIDTH = FOX_HEADS * LANES


def _params(*sem):
    return pltpu.CompilerParams(dimension_semantics=sem, vmem_limit_bytes=VMEM_LIMIT)


def _const_spec(shape):
    nd = len(shape)
    return pl.BlockSpec(shape, lambda *_: (0,) * nd)


def _layer_spec(a, l):
    nd = a.ndim - 1
    return pl.BlockSpec((None,) + a.shape[1:], lambda *_: (l,) + (0,) * nd)


def _rms(x, g):
    return x * lax.rsqrt(jnp.mean(x * x, axis=-1, keepdims=True) + EPS) * g


def _gelu(x):
    return 0.5 * x * (1.0 + lax.erf(x * 0.7071067811865476))


def _inproj_body(x_ref, g1_ref, wqk_ref, qc_ref, wkv_ref, wf_ref, bf_ref, wugp_ref, lng_ref, lnb_ref, wg_ref,
                 q_ref, k_ref, v_ref, kb_ref, vb_ref, lf_ref, u_ref, vn_ref, zp_ref, gate_ref):
    h = _rms(x_ref[...], g1_ref[...]).astype(bf16)
    zqk = jnp.dot(h, wqk_ref[...], preferred_element_type=f32)
    q_ref[...] = (zqk[:, :ATTN_WIDTH] * (HEAD_DIM ** -0.5 * LOG2E) + qc_ref[...]).astype(bf16)
    kb_ref[...] = zqk[:, ATTN_WIDTH:].astype(bf16)
    zkv = jnp.dot(h, wkv_ref[...], preferred_element_type=f32)
    v = zkv[:, FOX_WIDTH:]
    k_ref[...] = zkv[:, :FOX_WIDTH]
    v_ref[...] = v
    vb_ref[...] = v.astype(bf16)
    zf = jnp.dot(h, wf_ref[...], preferred_element_type=f32) + bf_ref[...]
    lf_ref[...] = jax.nn.log_sigmoid(zf)[:, :FOX_HEADS]
    zugp = jnp.dot(h, wugp_ref[...], preferred_element_type=f32)
    u_ref[...] = _gelu(zugp[:, :GMLP_WIDTH])
    gv = _gelu(zugp[:, GMLP_WIDTH:2 * GMLP_WIDTH])
    mu = jnp.mean(gv, axis=-1, keepdims=True)
    gc = gv - mu
    vn_ref[...] = gc * lax.rsqrt(jnp.mean(gc * gc, axis=-1, keepdims=True) + EPS) * lng_ref[...] + lnb_ref[...]
    zp_ref[...] = zugp[:, 2 * GMLP_WIDTH:]
    zg = jnp.dot(h, wg_ref[...], preferred_element_type=f32)
    gate_ref[...] = (1.0 / (1.0 + jnp.exp(-zg))).astype(bf16)


def _inproj(x, w, l, tn):
    n, d = x.shape
    row = lambda width: pl.BlockSpec((tn, width), lambda i: (i, 0))
    outs = [
        (ATTN_WIDTH, bf16), (FOX_WIDTH, f32), (FOX_WIDTH, f32), (ATTN_WIDTH, bf16), (FOX_WIDTH, bf16),
        (FOX_HEADS, f32), (GMLP_WIDTH, f32), (GMLP_WIDTH, f32), (POOL_WIDTH, f32), (N_BRANCH * d, bf16),
    ]
    weights = [w["g1"], w["wqk"], w["qc"], w["wkv"], w["wf"], w["bf"], w["wugp"], w["lng"], w["lnb"], w["wg"]]
    return pl.pallas_call(
        _inproj_body,
        grid=(n // tn,),
        in_specs=[row(d)] + [_layer_spec(a, l) for a in weights],
        out_specs=[row(width) for width, _ in outs],
        out_shape=[jax.ShapeDtypeStruct((n, width), dt) for width, dt in outs],
        compiler_params=_params("parallel"),
        name="inproj",
    )(x, *weights)


def _cumsum_body(x_ref, c_ref, hi_ref, mid_ref, lo_ref):
    x = x_ref[0]
    n = x.shape[-1]
    lane = lax.broadcasted_iota(jnp.int32, x.shape, 1)
    shift = 1
    while shift < n:
        x = x + jnp.where(lane >= shift, pltpu.roll(x, shift, 1), 0.0)
        shift *= 2
    c = x * LOG2E
    c_ref[0] = c
    hi = c.astype(bf16)
    r1 = c - hi.astype(f32)
    mid = r1.astype(bf16)
    hi_ref[0] = hi
    mid_ref[0] = mid
    lo_ref[0] = (r1 - mid.astype(f32)).astype(bf16)


def _cumsum_lanes(x):
    b, r, n = x.shape
    spec = pl.BlockSpec((1, r, n), lambda i: (i, 0, 0))
    return pl.pallas_call(
        _cumsum_body, grid=(b,), in_specs=[spec], out_specs=[spec] * 4,
        out_shape=[jax.ShapeDtypeStruct(x.shape, f32)] + [jax.ShapeDtypeStruct(x.shape, bf16)] * 3,
        compiler_params=_params("parallel"), name="cumsum",
    )(x)


def _attn_body(q_ref, k_ref, pc_ref, place_ref, v_ref, cq_ref, o_ref, ka_ref, *, tq, tk, q_off):
    qi = pl.program_id(2)
    lane = lax.broadcasted_iota(jnp.int32, (tq, LANES), 1)
    row0 = q_off + qi * tq
    n_full = (row0 + 1) // tk
    n_all = (row0 + tq + tk - 1) // tk
    nh = ATTN_HEADS_PER_STEP
    lk = k_ref.shape[1]

    @pl.when(qi == 0)
    def _():
        def place(t, _):
            rows = pl.ds(pl.multiple_of(t * tk, tk), tk)
            moved = jnp.dot(pc_ref[0, rows, :], place_ref[...], preferred_element_type=f32)
            ka_ref[rows, :] = (k_ref[0, rows, :].astype(f32) + moved).astype(bf16)
            return 0

        lax.fori_loop(0, lk // tk, place, 0)

    qs = [q_ref[0, :, hh * LANES:(hh + 1) * LANES] for hh in range(nh)]
    cqs = [jnp.broadcast_to(cq_ref[0, 0, :, hh:hh + 1], (tq, LANES)) for hh in range(nh)]
    n_col = tk // LANES
    lane_k = lax.broadcasted_iota(jnp.int32, (tk, LANES), 1)

    def step(t, carry, masked):
        start = pl.multiple_of(t * tk, tk)
        if masked:
            kpos = start + lax.broadcasted_iota(jnp.int32, (tq, tk), 1)
            qpos = row0 + lax.broadcasted_iota(jnp.int32, (tq, tk), 0)
            allowed = kpos <= qpos
        out = []
        for hh in range(nh):
            m, acc = carry[hh]
            ks = ka_ref[pl.ds(start, tk), hh * LANES:(hh + 1) * LANES]
            vs = v_ref[0, pl.ds(start, tk), (hh // 2) * LANES:(hh // 2 + 1) * LANES]
            own = (lane_k < HEAD_DIM) if hh % 2 == 0 else (lane_k >= HEAD_DIM)
            vs = jnp.where(own, vs, jnp.ones((), bf16))
            s = lax.dot_general(qs[hh], ks, (((1,), (1,)), ((), ())), preferred_element_type=f32)
            if masked:
                s = jnp.where(allowed, s, NEG_INF)
            cols = [s[:, j * LANES:(j + 1) * LANES] for j in range(n_col)]
            smax = functools.reduce(jnp.maximum, cols)
            m_new = jnp.maximum(m, jnp.max(smax, axis=1, keepdims=True) + cqs[hh])
            alpha = jnp.exp2(m - m_new)
            r = cqs[hh] - m_new
            p = jnp.concatenate([jnp.exp2(col + r).astype(bf16) for col in cols], axis=1)
            acc = alpha * acc + jnp.dot(p, vs, preferred_element_type=f32)
            out.append((m_new, acc))
        return tuple(out)

    one = (jnp.full((tq, LANES), NEG_INF, f32), jnp.zeros((tq, LANES), f32))
    carry = lax.fori_loop(0, n_full, functools.partial(step, masked=False), (one,) * nh)
    carry = lax.fori_loop(n_full, n_all, functools.partial(step, masked=True), carry)
    outs = []
    for hh, (_, acc) in enumerate(carry):
        sums = (lane >= HEAD_DIM) if hh % 2 == 0 else (lane < HEAD_DIM)
        outs.append(acc / jnp.max(jnp.where(sums, acc, 0.0), axis=1, keepdims=True))
    for pr in range(nh // 2):
        o_ref[0, :, pr * LANES:(pr + 1) * LANES] = jnp.where(
            lane < HEAD_DIM, outs[2 * pr], outs[2 * pr + 1]).astype(o_ref.dtype)


def _attention(q, k, pieces, place, v, cq, *, tq, tk, q_off):
    b, lq, _ = q.shape
    lk = k.shape[1]
    nh = ATTN_HEADS_PER_STEP
    return pl.pallas_call(
        functools.partial(_attn_body, tq=tq, tk=tk, q_off=q_off),
        grid=(b, FOX_HEADS // nh, lq // tq),
        in_specs=[
            pl.BlockSpec((1, tq, nh * LANES), lambda b_, h_, i: (b_, i, h_)),
            pl.BlockSpec((1, lk, nh * LANES), lambda b_, h_, i: (b_, 0, h_)),
            pl.BlockSpec((1, lk, LANES), lambda b_, h_, i: (b_, 0, h_)),
            _const_spec(place.shape),
            pl.BlockSpec((1, lk, nh * HEAD_DIM), lambda b_, h_, i: (b_, 0, h_)),
            pl.BlockSpec((1, 1, tq, nh), lambda b_, h_, i: (b_, h_, i, 0)),
        ],
        out_specs=pl.BlockSpec((1, tq, nh * HEAD_DIM), lambda b_, h_, i: (b_, i, h_)),
        out_shape=jax.ShapeDtypeStruct((b, lq, FOX_WIDTH), bf16),
        scratch_shapes=[pltpu.VMEM((lk, nh * LANES), bf16)],
        compiler_params=_params("parallel", "parallel", "arbitrary"),
        name="fox_attention",
    )(q, k, pieces, place, v, cq)


def _merge_body(x_ref, oa_ref, u_ref, vn_ref, zp_ref, prev_ref, gate_ref, ws_ref, bs_ref, wpool_ref, pscale_ref,
                wbr_ref, wout_ref, o_ref, *, tn, lc, pos0):
    j = pl.program_id(1)
    d = x_ref.shape[-1]
    t_io = lax.broadcasted_iota(jnp.int32, (lc, lc), 0)
    s_io = lax.broadcasted_iota(jnp.int32, (lc, lc), 1)
    causal = (s_io // SGU_BLOCK) <= (t_io // SGU_BLOCK)
    grp = lax.broadcasted_iota(jnp.int32, (lc, GMLP_WIDTH), 1) // (GMLP_WIDTH // GMLP_GROUPS)
    wmask = [jnp.where(causal, ws_ref[g], 0.0).astype(bf16) for g in range(GMLP_GROUPS)]
    ob = []
    for c in range(tn // lc):
        vc = vn_ref[0, c * lc:(c + 1) * lc, :].astype(bf16)
        s = bs_ref[...]
        for g in range(GMLP_GROUPS):
            sg = jnp.dot(wmask[g], vc, preferred_element_type=f32)
            s = s + jnp.where(grp == g, sg, 0.0)
        ob.append(u_ref[0, c * lc:(c + 1) * lc, :] * s)
    o_b = ob[0] if len(ob) == 1 else jnp.concatenate(ob, axis=0)
    zp = zp_ref[0]
    ext = jnp.concatenate([prev_ref[0, 0], zp], axis=0)
    s2 = ext + pltpu.roll(ext, 1, 0)
    s4 = s2 + pltpu.roll(s2, 2, 0)
    s8 = s4 + pltpu.roll(s4, 4, 0)
    s16 = s8 + pltpu.roll(s8, 8, 0)
    pgrp = lax.broadcasted_iota(jnp.int32, (tn, POOL_WIDTH), 1) // (POOL_WIDTH // POOL_GROUPS)
    pos = pos0 + j * tn + lax.broadcasted_iota(jnp.int32, (tn, POOL_WIDTH), 0)
    win = jnp.where(pgrp == 0, s2[HALO:], jnp.where(pgrp == 1, s4[HALO:], jnp.where(pgrp == 2, s8[HALO:], s16[HALO:])))
    width = jnp.where(pgrp == 0, 2, jnp.where(pgrp == 1, 4, jnp.where(pgrp == 2, 8, 16)))
    cnt = jnp.minimum(pos + 1, width).astype(f32)
    dlt = win / cnt - zp
    o_c = jnp.dot(dlt.astype(bf16), wpool_ref[...], preferred_element_type=f32) * pscale_ref[...]
    ya = jnp.dot(oa_ref[0], wbr_ref[:FOX_WIDTH, :], preferred_element_type=f32)
    yb = jnp.dot(o_b.astype(bf16), wbr_ref[FOX_WIDTH:FOX_WIDTH + GMLP_WIDTH, :], preferred_element_type=f32)
    yc = jnp.dot(o_c.astype(bf16), wbr_ref[FOX_WIDTH + GMLP_WIDTH:, :], preferred_element_type=f32)
    gates = [gate_ref[0, :, br * d:(br + 1) * d].astype(f32) for br in range(N_BRANCH)]
    merged = gates[0] * ya + gates[1] * yb + gates[2] * yc
    o_ref[0] = x_ref[0] + jnp.dot(merged.astype(bf16), wout_ref[...], preferred_element_type=f32)


def _merge(x, oa, u, vn, zp, prev, gate, w, l, *, tn, lc, pos0):
    nseq, L, d = x.shape
    seq = lambda width: pl.BlockSpec((1, tn, width), lambda b_, j: (b_, j, 0))
    weights = [w["ws"], w["bs"], w["wpool"], w["pscale"], w["wbr"], w["wout"]]
    return pl.pallas_call(
        functools.partial(_merge_body, tn=tn, lc=lc, pos0=pos0),
        grid=(nseq, L // tn),
        in_specs=[seq(d), seq(FOX_WIDTH), seq(GMLP_WIDTH), seq(GMLP_WIDTH), seq(POOL_WIDTH),
                  pl.BlockSpec((1, 1, HALO, POOL_WIDTH), lambda b_, j: (b_, j, 0, 0)), seq(N_BRANCH * d)]
        + [_layer_spec(a, l) for a in weights],
        out_specs=seq(d),
        out_shape=jax.ShapeDtypeStruct(x.shape, f32),
        compiler_params=_params("parallel", "parallel"),
        name="merge",
    )(x, oa, u, vn, zp, prev, gate, *weights)


def _merge_sort_pairs(n):
    def merge(lo, hi, r):
        step = r * 2
        if step < hi - lo:
            yield from merge(lo, hi, step)
            yield from merge(lo + r, hi, step)
            yield from [(i, i + r) for i in range(lo + r, hi - r, step)]
        else:
            yield (lo, lo + r)

    def sort(lo, hi):
        if hi - lo >= 1:
            mid = lo + (hi - lo) // 2
            yield from sort(lo, mid)
            yield from sort(mid + 1, hi)
            yield from merge(lo, hi, 1)

    return list(sort(0, n - 1))


def _top_values_sorted(block, k):
    n = block.shape[0] // SUBLANES
    groups = [block[g * SUBLANES:(g + 1) * SUBLANES, :] for g in range(n)]
    size = 1
    while size < n:
        size *= 2
    groups += [None] * (size - n)
    for i, j in _merge_sort_pairs(size):
        a, b = groups[i], groups[j]
        if a is None:
            groups[i], groups[j] = b, None
        elif b is not None:
            groups[i], groups[j] = jnp.maximum(a, b), jnp.minimum(a, b)
    groups = groups[:n]
    tops = []
    for rnd in range(k):
        m = jnp.max(groups[0], axis=0, keepdims=True)
        tops.append(m)
        hit = groups[0] == m
        depth = min(n, k - rnd)
        for g in range(depth - 1):
            groups[g] = jnp.where(hit, groups[g + 1], groups[g])
        groups[depth - 1] = jnp.where(hit, NEG_INF, groups[depth - 1])
    return tops


def _route_body(x_ref, g2_ref, wqt_ref, keys_ref, ht_ref, sc_ref, st_ref, top_ref, *, tn):
    h = _rms(x_ref[...], g2_ref[...]).astype(bf16)
    ht = h.T
    ht_ref[...] = ht
    qt = jnp.dot(wqt_ref[...], ht, preferred_element_type=f32).astype(bf16)
    n_sets = 2 * PEER_HEADS
    for r in range(n_sets):
        sc_ref[r * N_KEYS:(r + 1) * N_KEYS, :] = jnp.dot(
            keys_ref[r], qt[r * PEER_HALF:(r + 1) * PEER_HALF, :], preferred_element_type=f32)

    kk = PEER_TOPK + 1

    def half_top(r, _):
        base = pl.multiple_of(r * N_KEYS, N_KEYS)
        for tb in range(tn // LANES):
            blk = sc_ref[pl.ds(base, N_KEYS), tb * LANES:(tb + 1) * LANES]
            tops = _top_values_sorted(blk, kk)
            top_ref[r, :, tb * LANES:(tb + 1) * LANES] = jnp.concatenate(
                tops + [tops[-1]] * (TOP_ROWS - kk), axis=0)
        return 0

    lax.fori_loop(0, n_sets, half_top, 0)

    def head_stats(hd, _):
        for tb in range(tn // LANES):
            sl = slice(tb * LANES, (tb + 1) * LANES)
            a = top_ref[2 * hd, :, sl]
            b = top_ref[2 * hd + 1, :, sl]
            rows8 = lax.broadcasted_iota(jnp.int32, (SUBLANES, LANES), 0)
            never = jnp.full((SUBLANES, LANES), NEG_INF, f32)
            blocks = [(a[0:1], b[0:SUBLANES]), (a[0:1], b[SUBLANES:2 * SUBLANES])]
            for p in range(1, SUBLANES):
                blocks.append((a[p:p + 1], jnp.where(rows8 < kk // (p + 1), b[0:SUBLANES], never)))
            blocks.append((a[SUBLANES:2 * SUBLANES], b[0:1]))
            blocks.append((jnp.where(rows8 == 0, a[PEER_TOPK:PEER_TOPK + 1], a[0:1]),
                           jnp.where(rows8 == 0, b[0:1], jnp.where(rows8 == 1, b[PEER_TOPK:PEER_TOPK + 1], never))))
            cand = jnp.concatenate([ab + bb for ab, bb in blocks], axis=0)
            tops = _top_values_sorted(cand, kk)
            tau = 0.5 * (tops[-2] + tops[-1])
            z = jnp.zeros((1, LANES), f32)
            for ab, bb in blocks:
                kept = bb >= tau - ab
                z = z + jnp.sum(jnp.where(kept, jnp.exp((ab - a[0:1]) + (bb - b[0:1])), 0.0), axis=0, keepdims=True)
            st_ref[hd, :, sl] = jnp.concatenate(
                [a[0:1], b[0:1], tau, 0.5 / z, jnp.zeros((SUBLANES - 4, LANES), f32)], axis=0)
        return 0

    lax.fori_loop(0, PEER_HEADS, head_stats, 0)


def _route(x, w, l, tn):
    n, d = x.shape
    n_sets = 2 * PEER_HEADS
    return pl.pallas_call(
        functools.partial(_route_body, tn=tn),
        grid=(n // tn,),
        in_specs=[pl.BlockSpec((tn, d), lambda i: (i, 0)), _layer_spec(w["g2"], l), _layer_spec(w["wqt"], l),
                  _layer_spec(w["keys"], l)],
        out_specs=[pl.BlockSpec((d, tn), lambda i: (0, i)), pl.BlockSpec((n_sets * N_KEYS, tn), lambda i: (0, i)),
                   pl.BlockSpec((PEER_HEADS, SUBLANES, tn), lambda i: (0, 0, i))],
        out_shape=[jax.ShapeDtypeStruct((d, n), bf16), jax.ShapeDtypeStruct((n_sets * N_KEYS, n), f32),
                   jax.ShapeDtypeStruct((PEER_HEADS, SUBLANES, n), f32)],
        scratch_shapes=[pltpu.VMEM((n_sets, TOP_ROWS, tn), f32)],
        compiler_params=_params("parallel"),
        name="peer_route",
    )(x, w["g2"], w["wqt"], w["keys"])


def _n_halves(tn):
    return 2 if tn % (2 * LANES) == 0 else 1


def _experts_body(x_ref, ht_ref, sc_ref, st_ref, pu_ref, pvc_ref, fg_ref, o_ref, q_ref, act0_ref, act1_ref, acc_ref,
                  pvp_ref, *, tn, ec, final):
    c = pl.program_id(1)
    n_sub = ec // N_KEYS
    n_half = _n_halves(tn)
    half = tn // n_half
    acts = (act0_ref, act1_ref)

    @pl.when(c == 0)
    def _():
        for hd in range(PEER_HEADS):
            s2 = sc_ref[(2 * hd + 1) * N_KEYS:(2 * hd + 2) * N_KEYS, :]
            q_ref[hd] = jnp.exp(s2 - st_ref[hd, 1:2, :]) * st_ref[hd, 3:4, :]
        acc_ref[...] = jnp.zeros_like(acc_ref)
        acts[n_half - 1][...] = jnp.zeros_like(acts[n_half - 1])
        pvp_ref[...] = jnp.zeros_like(pvp_ref)

    i0 = pl.multiple_of(c * n_sub, n_sub)

    def hidden(hf):
        return jnp.dot(pu_ref[...], ht_ref[:, hf * half:(hf + 1) * half], preferred_element_type=f32)

    def values(pv_ref, hf):
        acc_ref[:, hf * half:(hf + 1) * half] += jnp.dot(pv_ref[...], acts[hf][...], preferred_element_type=f32)

    def weigh(hf, a):
        for tb in range(half // LANES):
            lanes = slice(hf * half + tb * LANES, hf * half + (tb + 1) * LANES)
            thr_rows, p_rows = [], []
            for hd in range(PEER_HEADS):
                s1 = sc_ref[pl.ds(2 * hd * N_KEYS + i0, n_sub), lanes]
                thr_rows.append(st_ref[hd, 2:3, lanes] - s1)
                p_rows.append(jnp.exp(s1 - st_ref[hd, 0:1, lanes]))
            for ii in range(n_sub):
                gsum = jnp.zeros((N_KEYS, LANES), f32)
                for hd in range(PEER_HEADS):
                    s2 = sc_ref[(2 * hd + 1) * N_KEYS:(2 * hd + 2) * N_KEYS, lanes]
                    kept = s2 >= thr_rows[hd][ii:ii + 1]
                    gsum = gsum + jnp.where(kept, q_ref[hd, :, lanes], 0.0) * p_rows[hd][ii:ii + 1]
                aa = a[ii * N_KEYS:(ii + 1) * N_KEYS, tb * LANES:(tb + 1) * LANES]
                acts[hf][ii * N_KEYS:(ii + 1) * N_KEYS, tb * LANES:(tb + 1) * LANES] = (
                    (aa + aa * lax.erf(aa * 0.7071067811865476)) * gsum).astype(bf16)

    a0 = hidden(0)
    values(pvp_ref, n_half - 1)
    weigh(0, a0)
    if n_half == 2:
        a1 = hidden(1)
        values(pvc_ref, 0)
        weigh(1, a1)
    pvp_ref[...] = pvc_ref[...]

    @pl.when(c == pl.num_programs(1) - 1)
    def _():
        values(pvc_ref, n_half - 1)
        y = x_ref[...] + acc_ref[...].T
        o_ref[...] = _rms(y, fg_ref[...]) if final else y


def _experts(x, ht, sc, st, w, l, fg, *, tn, ec, final):
    n, d = x.shape
    n_exp = w["pu"].shape[1]
    n_sets = 2 * PEER_HEADS
    return pl.pallas_call(
        functools.partial(_experts_body, tn=tn, ec=ec, final=final),
        grid=(n // tn, n_exp // ec),
        in_specs=[
            pl.BlockSpec((tn, d), lambda t, c: (t, 0)),
            pl.BlockSpec((d, tn), lambda t, c: (0, t)),
            pl.BlockSpec((n_sets * N_KEYS, tn), lambda t, c: (0, t)),
            pl.BlockSpec((PEER_HEADS, SUBLANES, tn), lambda t, c: (0, 0, t)),
            pl.BlockSpec((None, ec, d), lambda t, c: (l, c, 0)),
            pl.BlockSpec((None, d, ec), lambda t, c: (l, 0, c)),
            _const_spec(fg.shape),
        ],
        out_specs=pl.BlockSpec((tn, d), lambda t, c: (t, 0)),
        out_shape=jax.ShapeDtypeStruct((n, d), f32),
        scratch_shapes=[
            pltpu.VMEM((PEER_HEADS, N_KEYS, tn), f32),
            pltpu.VMEM((ec, tn // _n_halves(tn)), bf16), pltpu.VMEM((ec, tn // _n_halves(tn)), bf16),
            pltpu.VMEM((d, tn), f32), pltpu.VMEM((d, ec), bf16),
        ],
        compiler_params=_params("parallel", "arbitrary"),
        name="peer_experts",
    )(x, ht, sc, st, w["pu"], w["pvt"], fg)


def _gating_weights(w_s, b_s, lc):
    gdim = GMLP_WIDTH // GMLP_GROUPS
    return w_s[:, :, :lc, :lc], jnp.repeat(jnp.transpose(b_s[:, :, :lc], (0, 2, 1)), gdim, axis=2)


def _all_weights(norm1_g, w_in, b_f, ln_g, ln_b, w_s, b_s, w_pool, pool_scale, w_br, w_out, norm2_g, peer_wq,
                 peer_keys, peer_u, peer_v):
    depth, d = w_in.shape[:2]
    o_f = 3 * FOX_WIDTH
    o_u = o_f + FOX_HEADS
    o_g = o_u + 2 * GMLP_WIDTH + POOL_WIDTH
    wf = jnp.zeros((depth, d, LANES), f32).at[:, :, :FOX_HEADS].set(w_in[:, :, o_f:o_u])
    bf = jnp.zeros((depth, 1, LANES), f32).at[:, 0, :FOX_HEADS].set(b_f)
    pdim = POOL_WIDTH // POOL_GROUPS
    wpool = jnp.zeros((depth, POOL_WIDTH, POOL_WIDTH), f32)
    for g in range(POOL_GROUPS):
        wpool = wpool.at[:, g * pdim:(g + 1) * pdim, g * pdim:(g + 1) * pdim].set(w_pool[:, g])
    spare = LANES - HEAD_DIM

    def head_lanes(cols):
        cols = cols.reshape(depth, d, FOX_HEADS, HEAD_DIM)
        return jnp.pad(cols, ((0, 0), (0, 0), (0, 0), (0, spare))).reshape(depth, d, ATTN_WIDTH)

    qc = jnp.zeros((FOX_HEADS, LANES), f32).at[:, HEAD_DIM:HEAD_DIM + C_PIECES].set(-1.0).reshape(1, 1, ATTN_WIDTH)
    wqk = jnp.concatenate([head_lanes(w_in[:, :, :FOX_WIDTH]), head_lanes(w_in[:, :, FOX_WIDTH:2 * FOX_WIDTH])], axis=2)
    ws, bs = _gating_weights(w_s, b_s, GMLP_CHUNK)
    return {
        "g1": norm1_g[:, None], "wqk": wqk.astype(bf16), "qc": jnp.broadcast_to(qc, (depth, 1, ATTN_WIDTH)),
        "wkv": w_in[:, :, FOX_WIDTH:o_f].astype(bf16), "wf": wf.astype(bf16), "bf": bf,
        "wugp": w_in[:, :, o_u:o_g].astype(bf16), "lng": ln_g[:, None], "lnb": ln_b[:, None],
        "wg": w_in[:, :, o_g:].astype(bf16),
        "ws": ws, "bs": bs, "wpool": wpool.astype(bf16), "pscale": pool_scale[:, None],
        "wbr": w_br.astype(bf16), "wout": w_out.astype(bf16),
        "g2": norm2_g[:, None], "wqt": jnp.transpose(peer_wq, (0, 2, 1)).astype(bf16),
        "keys": peer_keys.reshape(depth, 2 * PEER_HEADS, N_KEYS, PEER_HALF).astype(bf16),
        "pu": peer_u.astype(bf16), "pvt": jnp.transpose(peer_v, (0, 2, 1)).astype(bf16),
    }


def _pick(n, prefs):
    for t in prefs:
        if n % t == 0:
            return t
    return n


def _layer(x, w, l, fg, final, *, k_hist, v_hist, logf_hist, pool_hist, lc, tiles):
    nseq, L, d = x.shape
    n = nseq * L
    q, k, v, kb, vb, logf, u, vn, zp, gate = _inproj(x.reshape(n, d), w, l, tiles["inproj"])
    r3 = lambda a: a.reshape(nseq, L, a.shape[-1])
    q, kb, vb, logf3, u, vn, zp, gate = map(r3, (q, kb, vb, logf, u, vn, zp, gate))
    past = 0 if k_hist is None else k_hist.shape[1]
    spare = LANES - HEAD_DIM
    if past:
        k_hist = jnp.pad(k_hist.astype(bf16), ((0, 0), (0, 0), (0, 0), (0, spare)))
        kb = jnp.concatenate([k_hist.reshape(nseq, past, ATTN_WIDTH), kb], axis=1)
        vb = jnp.concatenate([v_hist.reshape(nseq, past, FOX_WIDTH).astype(bf16), vb], axis=1)
        lf_all = jnp.concatenate([logf_hist.astype(f32), logf3], axis=1)
    else:
        lf_all = logf3
    lk = past + L
    tk = tiles["tk"]
    lk_pad = -(-lk // tk) * tk
    if lk_pad != lk:
        padk = ((0, 0), (0, lk_pad - lk), (0, 0))
        kb, vb, lf_all = jnp.pad(kb, padk), jnp.pad(vb, padk), jnp.pad(lf_all, padk)
    nh = ATTN_HEADS_PER_STEP
    c_t, c_hi, c_mid, c_lo = _cumsum_lanes(jnp.transpose(lf_all, (0, 2, 1)))
    cq = jnp.transpose(c_t.reshape(nseq, FOX_HEADS // nh, nh, lk_pad)[:, :, :, past:past + L], (0, 1, 3, 2))
    hg = FOX_HEADS // nh
    pieces = jnp.stack([c_hi, c_mid, c_lo], axis=-1).reshape(nseq, hg, nh, lk_pad, C_PIECES)
    pieces = jnp.transpose(pieces, (0, 3, 1, 2, 4)).reshape(nseq, lk_pad, hg, nh * C_PIECES)
    pieces = jnp.pad(pieces, ((0, 0), (0, 0), (0, 0), (0, LANES - nh * C_PIECES))).reshape(nseq, lk_pad, hg * LANES)
    src = jnp.arange(LANES)[:, None]
    dst = jnp.arange(nh * LANES)[None, :]
    place = ((dst // LANES == src // C_PIECES) & (dst % LANES == HEAD_DIM + src % C_PIECES)
             & (src < nh * C_PIECES)).astype(bf16)
    o_a = _attention(q, kb, pieces, place, vb, cq, tq=tiles["tq"], tk=tk, q_off=past)
    tn = tiles["merge"]
    nt = L // tn
    first = jnp.zeros((nseq, HALO, POOL_WIDTH), f32)
    if pool_hist is not None:
        first = first.at[:, HALO - POOL_HIST:].set(pool_hist.astype(f32))
    tails = zp.reshape(nseq, nt, tn, POOL_WIDTH)[:, :nt - 1, tn - HALO:, :]
    prev = jnp.concatenate([first[:, None], tails], axis=1)
    x_mid = _merge(x, o_a, u, vn, zp, prev, gate, w, l, tn=tn, lc=lc, pos0=past)
    ht, sc, st = _route(x_mid.reshape(n, d), w, l, tiles["route"])
    x_new = _experts(x_mid.reshape(n, d), ht, sc, st, w, l, fg, tn=tiles["experts"], ec=tiles["ec"], final=final)
    new_pool = zp[:, L - POOL_HIST:]
    return x_new.reshape(nseq, L, d), r3(k), r3(v), logf3, new_pool, vn


def kernel(x_prompt, x_sample, cache_fox_k, cache_fox_v, cache_fox_logf, state_pool, norm1_g, w_in, b_f, ln_g, ln_b,
           w_s, b_s, w_pool, pool_scale, w_br, w_out, norm2_g, peer_wq, peer_keys, peer_u, peer_v, final_g):
    depth = w_in.shape[0]
    bp, sp, d = x_prompt.shape
    bd, ls, _ = x_sample.shape
    assert sp % GMLP_CHUNK == 0 and ls <= GMLP_CHUNK and ls >= POOL_HIST and ls % 16 == 0
    params = (norm1_g, w_in, b_f, ln_g, ln_b, w_s, b_s, w_pool, pool_scale, w_br, w_out, norm2_g, peer_wq, peer_keys,
              peer_u, peer_v)
    tiles_p = {"inproj": _pick(bp * sp, (256,)), "tq": _pick(sp, (512, 256, 128)), "tk": _pick(sp, (1024, 512, 256, 128)),
               "merge": _pick(sp, (256, 128)), "route": _pick(bp * sp, (512, 256, 128)),
               "experts": _pick(bp * sp, (512, 256, 128)), "ec": EXPERT_CHUNK}
    ns = bd * ls
    keys_s = -(-(cache_fox_k.shape[2] + ls) // LANES) * LANES
    tiles_s = {"inproj": ns, "tq": ls, "tk": keys_s, "merge": ls, "route": ns, "experts": ns, "ec": EXPERT_CHUNK}
    xp, xs = x_prompt, x_sample
    outs_p, outs_s = [], []
    wp = _all_weights(*params)
    wsm = dict(wp)
    wsm["ws"], wsm["bs"] = _gating_weights(w_s, b_s, ls)
    g = final_g[None]
    for l in range(depth):
        last = l == depth - 1
        xp, k, v, lf, pool, _ = _layer(xp, wp, l, g, last, k_hist=None, v_hist=None, logf_hist=None, pool_hist=None,
                                       lc=GMLP_CHUNK, tiles=tiles_p)
        outs_p.append((k.reshape(bp, sp, FOX_HEADS, HEAD_DIM), v.reshape(bp, sp, FOX_HEADS, HEAD_DIM), lf, pool))
        xs, k, v, lf, pool, vn = _layer(xs, wsm, l, g, last, k_hist=cache_fox_k[l], v_hist=cache_fox_v[l],
                                        logf_hist=cache_fox_logf[l], pool_hist=state_pool[l], lc=ls, tiles=tiles_s)
        outs_s.append((k.reshape(bd, ls, FOX_HEADS, HEAD_DIM), v.reshape(bd, ls, FOX_HEADS, HEAD_DIM), lf, pool, vn))
    y_prompt, y_sample = xp, xs
    stack = lambda outs, i: jnp.stack([o[i] for o in outs])
    return (y_prompt, y_sample, stack(outs_p, 0), stack(outs_p, 1), stack(outs_p, 2), stack(outs_p, 3),
            stack(outs_s, 0), stack(outs_s, 1), stack(outs_s, 2), stack(outs_s, 3), stack(outs_s, 4))
```
